```python
import jax, jax.numpy as jnp
from jax import lax
import numpy as np

D_MODEL = 1024
BATCH = 32
SEQ = 256
DEPTH = 2
DEC_BATCH = 4
DEC_SEQ = 1024
PAST_LEN = 512

GRID_W = 64
MIX_W = D_MODEL
CONF_W = MIX_W // 4
ATTN_W = MIX_W // 2
LRU_W = MIX_W - CONF_W - ATTN_W
HEAD_DIM = 64
N_HEADS = ATTN_W // HEAD_DIM
N_KV_HEADS = 2
KV_W = N_KV_HEADS * HEAD_DIM
CONF_KERNEL = 31
LRU_CONV = 4
LRU_HEADS = 4
LRU_BLOCK = LRU_W // LRU_HEADS
LRU_C = 8.0
N_EXPERTS = 32
TOP_K = 4
D_FF = D_MODEL
SWIGLU_ALPHA = 1.702
SWIGLU_LIMIT = 7.0
ROPE_THETA = 10000.0
Q_BLOCK = 128
EPS = 1e-6
IN_W = 2 * CONF_W + ATTN_W + 2 * KV_W + 2 * LRU_W
SPLITS = (2 * CONF_W, 2 * CONF_W + ATTN_W, 2 * CONF_W + ATTN_W + KV_W,
          2 * CONF_W + ATTN_W + 2 * KV_W, 2 * CONF_W + ATTN_W + 2 * KV_W + LRU_W)

kernel_name = 'hybrid_dit_conformer_gqa_rglru_moe_step'


def rmsnorm(x, w):
    xf = x.astype(jnp.float32)
    y = xf * lax.rsqrt(jnp.mean(xf * xf, axis=-1, keepdims=True) + EPS)
    return (y * w.astype(jnp.float32)).astype(x.dtype)


def layernorm(x, w, b):
    xf = x.astype(jnp.float32)
    mu = jnp.mean(xf, axis=-1, keepdims=True)
    var = jnp.mean(jnp.square(xf - mu), axis=-1, keepdims=True)
    y = (xf - mu) * lax.rsqrt(var + EPS) * w.astype(jnp.float32) + b.astype(jnp.float32)
    return y.astype(x.dtype)


def depthwise_conv(x, w, b, pad):
    y = lax.conv_general_dilated(x, w[:, None, :].astype(x.dtype), window_strides=(1,), padding=[pad],
                                 dimension_numbers=('NWC', 'WIO', 'NWC'), feature_group_count=x.shape[-1])
    return y + b.astype(x.dtype)


def axial_angles(rows):
    row = jnp.repeat(jnp.arange(rows, dtype=jnp.float32), GRID_W)
    col = jnp.tile(jnp.arange(GRID_W, dtype=jnp.float32), rows)
    half = HEAD_DIM // 2
    freqs = ROPE_THETA ** (-jnp.arange(0, half, 2, dtype=jnp.float32) / half)
    return row[:, None] * freqs, col[:, None] * freqs


def rope_1d(x, ang):
    x1, x2 = jnp.split(x, 2, axis=-1)
    cos = jnp.cos(ang)[None, :, None, :]
    sin = jnp.sin(ang)[None, :, None, :]
    return jnp.concatenate([x1 * cos - x2 * sin, x2 * cos + x1 * sin], axis=-1)


def axial_rope(x, angles):
    ang_r, ang_c = angles
    xf = x.astype(jnp.float32)
    half = HEAD_DIM // 2
    y = jnp.concatenate([rope_1d(xf[..., :half], ang_r), rope_1d(xf[..., half:], ang_c)], axis=-1)
    return y.astype(x.dtype)


def block_attention(q, k, v):
    B, L, H, HD = q.shape
    G = H // N_KV_HEADS
    nblk = L // Q_BLOCK
    qb = q.reshape(B, nblk, Q_BLOCK, N_KV_HEADS, G, HD).transpose(1, 0, 2, 3, 4, 5)
    scale = HD ** -0.5

    def one_block(qi):
        s = jnp.einsum('bqkgd,bskd->bkgqs', qi, k).astype(jnp.float32) * scale
        p = jax.nn.softmax(s, axis=-1).astype(v.dtype)
        return jnp.einsum('bkgqs,bskd->bqkgd', p, v)

    o = lax.map(one_block, qb)
    return o.transpose(1, 0, 2, 3, 4, 5).reshape(B, L, H * HD)


def conformer_conv(u, conv_w, conv_b, ln_w, ln_b):
    a, g = jnp.split(u, 2, axis=-1)
    z = a * jax.nn.sigmoid(g)
    z = depthwise_conv(z, conv_w, conv_b, (CONF_KERNEL // 2, CONF_KERNEL - 1 - CONF_KERNEL // 2))
    return jax.nn.silu(layernorm(z, ln_w, ln_b))


def linear_scan(a, u, h0, reverse):
    def step(h, au):
        a_t, u_t = au
        h = a_t * h + u_t
        return h, h
    h_last, hs = lax.scan(step, h0, (a.swapaxes(0, 1), u.swapaxes(0, 1)), reverse=reverse)
    return hs.swapaxes(0, 1), h_last


def rglru_mixer(xr, gate, conv_w, conv_b, wa, ba, wx, bx, lam, h0):
    B, L, W = xr.shape
    xc = depthwise_conv(xr, conv_w, conv_b, (LRU_CONV // 2, LRU_CONV - 1 - LRU_CONV // 2))
    xb = xc.reshape(B, L, LRU_HEADS, LRU_BLOCK)
    r = jax.nn.sigmoid((jnp.einsum('blhi,dhij->bldhj', xb, wa).reshape(B, L, 2, W) + ba).astype(jnp.float32))
    i = jax.nn.sigmoid((jnp.einsum('blhi,dhij->bldhj', xb, wx).reshape(B, L, 2, W) + bx).astype(jnp.float32))
    log_a = -LRU_C * r * jax.nn.softplus(-lam.astype(jnp.float32))
    a = jnp.exp(log_a)
    u = jnp.sqrt(-jnp.expm1(2.0 * log_a)) * i * xc.astype(jnp.float32)[:, :, None, :]
    h0f = h0.astype(jnp.float32)
    h_f, last_f = linear_scan(a[:, :, 0], u[:, :, 0], h0f[:, 0], reverse=False)
    h_b, last_b = linear_scan(a[:, :, 1], u[:, :, 1], h0f[:, 1], reverse=True)
    y = ((h_f + h_b) * jax.nn.gelu(gate.astype(jnp.float32))).astype(xr.dtype)
    return y, jnp.stack([last_f, last_b], axis=1)


def moe(h, router_w, router_b, w_gu, b_gu, w_down, b_down):
    B, L, D = h.shape
    t = h.reshape(B * L, D)
    logits = (t @ router_w + router_b).astype(jnp.float32)
    top_v, top_i = lax.top_k(logits, TOP_K)
    gates = jax.nn.softmax(top_v, axis=-1)
    combine = jnp.einsum('nk,nke->ne', gates, jax.nn.one_hot(top_i, N_EXPERTS, dtype=jnp.float32)).astype(h.dtype)
    out = jnp.zeros_like(t)
    for e in range(N_EXPERTS):
        gu = t @ w_gu[e] + b_gu[e]
        x_glu, x_lin = jnp.split(gu, 2, axis=-1)
        x_glu = jnp.minimum(x_glu, SWIGLU_LIMIT)
        x_lin = jnp.clip(x_lin, -SWIGLU_LIMIT, SWIGLU_LIMIT)
        act = x_glu * jax.nn.sigmoid(SWIGLU_ALPHA * x_glu) * (x_lin + 1.0)
        out = out + combine[:, e:e + 1] * (act @ w_down[e] + b_down[e])
    return out.reshape(B, L, D)


def trunk_layer(x, mod, p, angles, ctx):
    B, L, _ = x.shape
    shift1, scale1, gate1, shift2, scale2, gate2 = jnp.split(mod, 6, axis=-1)
    h = rmsnorm(x, p['norm_mix_w']) * (1.0 + scale1) + shift1
    proj = h @ p['w_in']
    conf_u, q, k, v, lru_x, lru_g = jnp.split(proj, SPLITS, axis=-1)
    conf_o = conformer_conv(conf_u, p['conf_conv_w'], p['conf_conv_b'], p['conf_ln_w'], p['conf_ln_b'])
    q = rmsnorm(q.reshape(B, L, N_HEADS, HEAD_DIM), p['q_norm_w'])
    k = rmsnorm(k.reshape(B, L, N_KV_HEADS, HEAD_DIM), p['k_norm_w'])
    v = v.reshape(B, L, N_KV_HEADS, HEAD_DIM)
    if ctx is None:
        attn_o = block_attention(q, k, v)
        h0 = jnp.zeros((B, 2, LRU_W), jnp.float32)
    else:
        k_ctx, v_ctx, h0 = ctx
        q_rot = axial_rope(q, angles)
        k_all = jnp.concatenate([axial_rope(k, angles), k_ctx.astype(k.dtype)], axis=1)
        v_all = jnp.concatenate([v, v_ctx.astype(v.dtype)], axis=1)
        attn_o = block_attention(q_rot, k_all, v_all)
    lru_o, h_last = rglru_mixer(lru_x, lru_g, p['lru_conv_w'], p['lru_conv_b'], p['lru_wa'], p['lru_ba'],
                                p['lru_wx'], p['lru_bx'], p['lru_lambda'], h0)
    mixed = jnp.concatenate([conf_o, attn_o, lru_o], axis=-1) @ p['w_out']
    x = x + gate1 * mixed
    h2 = rmsnorm(x, p['norm_ffn_w']) * (1.0 + scale2) + shift2
    x = x + gate2 * moe(h2, p['router_w'], p['router_b'], p['w_gu'], p['b_gu'], p['w_down'], p['b_down'])
    return x, (k, v, h_last.astype(x.dtype))


def setup_inputs(seed: int = 0) -> dict:
    key = jax.random.key(seed)
    ks = jax.random.split(key, 40)
    D = D_MODEL

    def nrm(k, shape, scale):
        return jax.random.normal(k, shape, jnp.float32) * scale

    u = jax.random.uniform(ks[20], (DEPTH, 2, LRU_W), jnp.float32, minval=0.9, maxval=0.999)
    s = u ** (1.0 / LRU_C)
    lam = jnp.log(s) - jnp.log1p(-s)
    return {
        'x_prompt': nrm(ks[0], (BATCH, SEQ, D), 1.0),
        'x_sample': nrm(ks[1], (DEC_BATCH, DEC_SEQ, D), 1.0),
        'cache_k': nrm(ks[2], (DEC_BATCH, DEPTH, PAST_LEN, N_KV_HEADS, HEAD_DIM), 1.0),
        'cache_v': nrm(ks[3], (DEC_BATCH, DEPTH, PAST_LEN, N_KV_HEADS, HEAD_DIM), 1.0),
        'state_lru': nrm(ks[4], (DEC_BATCH, DEPTH, 2, LRU_W), 0.5),
        'c': nrm(ks[5], (DEC_BATCH, D), 1.0),
        'c_ctx': nrm(ks[6], (D,), 1.0),
        'w_mod': nrm(ks[7], (DEPTH, D, 6 * D), 0.5 * D ** -0.5),
        'b_mod': nrm(ks[8], (DEPTH, 6 * D), 0.02),
        'norm_mix_w': 1.0 + nrm(ks[9], (DEPTH, D), 0.02),
        'w_in': nrm(ks[10], (DEPTH, D, IN_W), D ** -0.5),
        'conf_conv_w': nrm(ks[11], (DEPTH, CONF_KERNEL, CONF_W), CONF_KERNEL ** -0.5),
        'conf_conv_b': nrm(ks[12], (DEPTH, CONF_W), 0.02),
        'conf_ln_w': 1.0 + nrm(ks[13], (DEPTH, CONF_W), 0.02),
        'conf_ln_b': nrm(ks[14], (DEPTH, CONF_W), 0.02),
        'q_norm_w': 1.0 + nrm(ks[15], (DEPTH, HEAD_DIM), 0.02),
        'k_norm_w': 1.0 + nrm(ks[16], (DEPTH, HEAD_DIM), 0.02),
        'lru_conv_w': nrm(ks[17], (DEPTH, LRU_CONV, LRU_W), LRU_CONV ** -0.5),
        'lru_conv_b': nrm(ks[18], (DEPTH, LRU_W), 0.02),
        'lru_wa': nrm(ks[19], (DEPTH, 2, LRU_HEADS, LRU_BLOCK, LRU_BLOCK), LRU_BLOCK ** -0.5),
        'lru_ba': nrm(ks[21], (DEPTH, 2, LRU_W), 0.02),
        'lru_wx': nrm(ks[22], (DEPTH, 2, LRU_HEADS, LRU_BLOCK, LRU_BLOCK), LRU_BLOCK ** -0.5),
        'lru_bx': nrm(ks[23], (DEPTH, 2, LRU_W), 0.02),
        'lru_lambda': lam,
        'w_out': nrm(ks[24], (DEPTH, MIX_W, D), MIX_W ** -0.5),
        'norm_ffn_w': 1.0 + nrm(ks[25], (DEPTH, D), 0.02),
        'router_w': nrm(ks[26], (DEPTH, D, N_EXPERTS), D ** -0.5),
        'router_b': nrm(ks[27], (DEPTH, N_EXPERTS), 0.01),
        'w_gu': nrm(ks[28], (DEPTH, N_EXPERTS, D, 2 * D_FF), D ** -0.5),
        'b_gu': nrm(ks[29], (DEPTH, N_EXPERTS, 2 * D_FF), 0.02),
        'w_down': nrm(ks[30], (DEPTH, N_EXPERTS, D_FF, D), D_FF ** -0.5),
        'b_down': nrm(ks[31], (DEPTH, N_EXPERTS, D), 0.02),
        'final_norm_w': 1.0 + nrm(ks[32], (D,), 0.02),
    }


def reference(x_prompt, x_sample, cache_k, cache_v, state_lru, c, c_ctx, w_mod, b_mod, norm_mix_w, w_in,
              conf_conv_w, conf_conv_b, conf_ln_w, conf_ln_b, q_norm_w, k_norm_w, lru_conv_w, lru_conv_b,
              lru_wa, lru_ba, lru_wx, lru_bx, lru_lambda, w_out, norm_ffn_w, router_w, router_b,
              w_gu, b_gu, w_down, b_down, final_norm_w):
    rows = x_sample.shape[1] // GRID_W
    angles = axial_angles(rows)
    yp, ys = x_prompt, x_sample
    new_k, new_v, new_h = [], [], []
    for l in range(DEPTH):
        p = {
            'norm_mix_w': norm_mix_w[l], 'w_in': w_in[l],
            'conf_conv_w': conf_conv_w[l], 'conf_conv_b': conf_conv_b[l],
            'conf_ln_w': conf_ln_w[l], 'conf_ln_b': conf_ln_b[l],
            'q_norm_w': q_norm_w[l], 'k_norm_w': k_norm_w[l],
            'lru_conv_w': lru_conv_w[l], 'lru_conv_b': lru_conv_b[l],
            'lru_wa': lru_wa[l], 'lru_ba': lru_ba[l], 'lru_wx': lru_wx[l], 'lru_bx': lru_bx[l],
            'lru_lambda': lru_lambda[l], 'w_out': w_out[l], 'norm_ffn_w': norm_ffn_w[l],
            'router_w': router_w[l], 'router_b': router_b[l],
            'w_gu': w_gu[l], 'b_gu': b_gu[l], 'w_down': w_down[l], 'b_down': b_down[l],
        }
        mod_ctx = jax.nn.silu(c_ctx) @ w_mod[l] + b_mod[l]
        mod_lat = (jax.nn.silu(c) @ w_mod[l] + b_mod[l])[:, None, :]
        yp, (k_l, v_l, h_l) = trunk_layer(yp, mod_ctx, p, None, None)
        new_k.append(k_l)
        new_v.append(v_l)
        new_h.append(h_l)
        ys, _ = trunk_layer(ys, mod_lat, p, angles, (cache_k[:, l], cache_v[:, l], state_lru[:, l]))
    y_prompt = rmsnorm(yp, final_norm_w)
    y_sample = rmsnorm(ys, final_norm_w)
    new_cache_k = jnp.stack(new_k, axis=1)
    new_cache_v = jnp.stack(new_v, axis=1)
    new_state_lru = jnp.stack(new_h, axis=1)
    return (y_prompt, y_sample, new_cache_k, new_cache_v, new_state_lru)
```

```python
import functools

import jax
import jax.numpy as jnp
from jax import lax
from jax.experimental import pallas as pl
from jax.experimental.pallas import tpu as pltpu

F32 = jnp.float32
BF16 = jnp.bfloat16

HEAD_DIM = 64
N_HEADS = 8
N_KV_HEADS = 2
HEADS_PER_KV = N_HEADS // N_KV_HEADS
CONF_KERNEL = 31
LRU_CONV = 4
LRU_C = 8.0
N_EXPERTS = 32
TOP_K = 4
SWIGLU_ALPHA = 1.702
SWIGLU_LIMIT = 7.0
ROPE_THETA = 10000.0
GRID_W = 64
EPS = 1e-6

LANES = 128
TOKEN_TILE = 256
EXPERT_TILE = 256
CONV_HALO = 16
ROW_CHUNK = 64
SCAN_CHUNK = 32
VMEM_LIMIT = 56 * 1024 * 1024


def _params(sem, vmem=None):
    return pltpu.CompilerParams(dimension_semantics=sem, vmem_limit_bytes=vmem)


def _split_bf16(x):
    hi = x.astype(BF16)
    lo = (x - hi.astype(F32)).astype(BF16)
    return hi, lo


def _dot(a, b):
    return jnp.dot(a, b, preferred_element_type=F32)


def _dot3(a, b):
    a_hi, a_lo = _split_bf16(a)
    b_hi, b_lo = _split_bf16(b)
    return _dot(a_hi, b_hi) + (_dot(a_hi, b_lo) + _dot(a_lo, b_hi))


def _mod_kernel(c_ref, w_ref, b_ref, o_ref):
    c = c_ref[...]
    s = c * jax.nn.sigmoid(c)
    o_ref[...] = _dot3(s, w_ref[...]) + b_ref[...]


def _modulation(cvec, w_mod, b_mod):
    depth, d, d6 = w_mod.shape
    tn = 768
    return pl.pallas_call(
        _mod_kernel,
        out_shape=jax.ShapeDtypeStruct((depth, cvec.shape[0], d6), F32),
        grid=(depth, d6 // tn),
        in_specs=[
            pl.BlockSpec(cvec.shape, lambda l, j: (0, 0)),
            pl.BlockSpec((None, d, tn), lambda l, j: (l, 0, j)),
            pl.BlockSpec((None, 1, tn), lambda l, j: (l, 0, j)),
        ],
        out_specs=pl.BlockSpec((None, cvec.shape[0], tn), lambda l, j: (l, 0, j)),
        compiler_params=_params(("arbitrary", "arbitrary")),
        name="modulation",
    )(cvec, w_mod, b_mod.reshape(depth, 1, d6))


def _pre_kernel(x_ref, mod_ref, nw_ref, win_ref, conf_ref, q_ref, k_ref, v_ref, lx_ref, lg_ref):
    x = x_ref[...]
    m = mod_ref[...]
    shift, scale = m[0:1], m[1:2]
    h = x * lax.rsqrt(jnp.mean(x * x, axis=-1, keepdims=True) + EPS) * nw_ref[...]
    h = h * (1.0 + scale) + shift
    proj = _dot(h.astype(BF16), win_ref[...])
    col = 0
    for ref in (conf_ref, q_ref, k_ref, v_ref, lx_ref, lg_ref):
        w = ref.shape[-1]
        ref[...] = proj[:, col:col + w]
        col += w


def _pre_mixer(x, mod_l, mod_row, norm_w, w_in_bf16, widths):
    n, d = x.shape
    return pl.pallas_call(
        _pre_kernel,
        out_shape=[jax.ShapeDtypeStruct((n, w), F32) for w in widths],
        grid=(n // TOKEN_TILE,),
        in_specs=[
            pl.BlockSpec((TOKEN_TILE, d), lambda i: (i, 0)),
            pl.BlockSpec((None, 6, d), lambda i: (mod_row(i), 0, 0)),
            pl.BlockSpec((1, d), lambda i: (0, 0)),
            pl.BlockSpec(w_in_bf16.shape, lambda i: (0, 0)),
        ],
        out_specs=[pl.BlockSpec((TOKEN_TILE, w), lambda i: (i, 0)) for w in widths],
        compiler_params=_params(("arbitrary",)),
        name="pre_mixer",
    )(x, mod_l, norm_w.reshape(1, d), w_in_bf16)


def _gelu_tanh(x):
    return 0.5 * x * (1.0 + jnp.tanh(0.7978845608028654 * (x + 0.044715 * (x * x * x))))


def _seq_kernel(seq_len, conf_ref, lx_ref, lg_ref, h0_ref, ccw_ref, ccb_ref, lnw_ref, lnb_ref,
                lcw_ref, lcb_ref, wg_ref, bg_ref, lam_ref,
                co_ref, lo_ref, hl_ref, pad_ref, af_ref, uf_ref, ab_ref, ub_ref):
    L = seq_len
    W = co_ref.shape[-1]
    pad = L // 2
    zeros_halo = jnp.zeros((CONV_HALO, W), F32)

    pad_ref[0:CONV_HALO, :] = zeros_halo
    pad_ref[CONV_HALO + L:2 * CONV_HALO + L, :] = zeros_halo
    for c in range(L // ROW_CHUNK):
        r0 = c * ROW_CHUNK
        u = conf_ref[r0:r0 + ROW_CHUNK, :]
        pad_ref[CONV_HALO + r0:CONV_HALO + r0 + ROW_CHUNK, :] = u[:, :W] * jax.nn.sigmoid(u[:, W:])
    left = CONF_KERNEL // 2
    for c in range(L // ROW_CHUNK):
        r0 = c * ROW_CHUNK
        acc = jnp.zeros((ROW_CHUNK, W), F32) + ccb_ref[...]
        for k in range(CONF_KERNEL):
            start = CONV_HALO - left + k + r0
            acc = acc + ccw_ref[k:k + 1, :] * pad_ref[start:start + ROW_CHUNK, :]
        mu = jnp.mean(acc, axis=-1, keepdims=True)
        cen = acc - mu
        var = jnp.mean(cen * cen, axis=-1, keepdims=True)
        y = cen * lax.rsqrt(var + EPS) * lnw_ref[...] + lnb_ref[...]
        co_ref[r0:r0 + ROW_CHUNK, :] = y * jax.nn.sigmoid(y)

    for c in range(L // ROW_CHUNK):
        r0 = c * ROW_CHUNK
        pad_ref[CONV_HALO + r0:CONV_HALO + r0 + ROW_CHUNK, :] = lx_ref[r0:r0 + ROW_CHUNK, :]
    ones_pad = jnp.ones((pad, W), F32)
    zeros_pad = jnp.zeros((pad, W), F32)
    af_ref[0:pad, :] = ones_pad
    uf_ref[0:pad, :] = zeros_pad
    ab_ref[L:L + pad, :] = ones_pad
    ub_ref[L:L + pad, :] = zeros_pad
    sp = jax.nn.softplus(-lam_ref[...])
    h0 = h0_ref[...]
    left = LRU_CONV // 2
    n_chunks = L // ROW_CHUNK
    for c in range(n_chunks):
        r0 = c * ROW_CHUNK
        xc = jnp.zeros((ROW_CHUNK, W), F32) + lcb_ref[...]
        for k in range(LRU_CONV):
            start = CONV_HALO - left + k + r0
            xc = xc + lcw_ref[k:k + 1, :] * pad_ref[start:start + ROW_CHUNK, :]
        g = _dot(xc.astype(BF16), wg_ref[...]) + bg_ref[...]
        r = jax.nn.sigmoid(g[:, :2 * W])
        i = jax.nn.sigmoid(g[:, 2 * W:])
        a = jnp.exp((-LRU_C) * r * sp)
        xc2 = jnp.concatenate([xc, xc], axis=-1)
        u = jnp.sqrt(1.0 - a * a) * i * xc2
        a_f, a_b, u_f, u_b = a[:, :W], a[:, W:], u[:, :W], u[:, W:]
        row = lax.broadcasted_iota(jnp.int32, (ROW_CHUNK, W), 0)
        if c == 0:
            u_f = jnp.where(row == 0, u_f + a_f * h0[0:1], u_f)
        if c == n_chunks - 1:
            u_b = jnp.where(row == ROW_CHUNK - 1, u_b + a_b * h0[1:2], u_b)
        af_ref[pad + r0:pad + r0 + ROW_CHUNK, :] = a_f
        uf_ref[pad + r0:pad + r0 + ROW_CHUNK, :] = u_f
        ab_ref[r0:r0 + ROW_CHUNK, :] = a_b
        ub_ref[r0:r0 + ROW_CHUNK, :] = u_b

    n_sc = L // SCAN_CHUNK
    s = 1
    while s < L:
        for c in reversed(range(n_sc)):
            r0 = c * SCAN_CHUNK
            if r0 + SCAN_CHUNK <= s:
                continue
            cur = slice(pad + r0, pad + r0 + SCAN_CHUNK)
            sh = slice(pad + r0 - s, pad + r0 - s + SCAN_CHUNK)
            a_cur = af_ref[cur, :]
            uf_ref[cur, :] = uf_ref[cur, :] + a_cur * uf_ref[sh, :]
            af_ref[cur, :] = a_cur * af_ref[sh, :]
        for c in range(n_sc):
            r0 = c * SCAN_CHUNK
            if r0 >= L - s:
                continue
            cur = slice(r0, r0 + SCAN_CHUNK)
            sh = slice(r0 + s, r0 + s + SCAN_CHUNK)
            a_cur = ab_ref[cur, :]
            ub_ref[cur, :] = ub_ref[cur, :] + a_cur * ub_ref[sh, :]
            ab_ref[cur, :] = a_cur * ab_ref[sh, :]
        s *= 2

    for c in range(n_chunks):
        r0 = c * ROW_CHUNK
        h = uf_ref[pad + r0:pad + r0 + ROW_CHUNK, :] + ub_ref[r0:r0 + ROW_CHUNK, :]
        lo_ref[r0:r0 + ROW_CHUNK, :] = h * _gelu_tanh(lg_ref[r0:r0 + ROW_CHUNK, :])
    hl_ref[0:1, :] = uf_ref[pad + L - 1:pad + L, :]
    hl_ref[1:2, :] = ub_ref[0:1, :]


def _seq_mixers(seq_len, n_seq, row0, conf_u, lru_x, lru_g, h0, p):
    w = lru_x.shape[-1]
    b0 = row0 // seq_len
    pad = seq_len // 2
    in_spec = lambda width: pl.BlockSpec((seq_len, width), lambda i: (b0 + i, 0))
    out_spec = pl.BlockSpec((seq_len, w), lambda i: (i, 0))
    state_spec = pl.BlockSpec((None, 2, w), lambda i: (i, 0, 0))
    full = lambda a: pl.BlockSpec(a.shape, lambda i: (0,) * a.ndim)
    weights = (p['conf_conv_w'], p['conf_conv_b'], p['conf_ln_w'], p['conf_ln_b'],
               p['lru_conv_w'], p['lru_conv_b'], p['lru_wg'], p['lru_bg'], p['lru_lam'])
    return pl.pallas_call(
        functools.partial(_seq_kernel, seq_len),
        out_shape=[jax.ShapeDtypeStruct((n_seq * seq_len, w), F32)] * 2
        + [jax.ShapeDtypeStruct((n_seq, 2, w), F32)],
        grid=(n_seq,),
        in_specs=[in_spec(2 * w), in_spec(w), in_spec(w), state_spec] + [full(a) for a in weights],
        out_specs=[out_spec, out_spec, state_spec],
        scratch_shapes=[pltpu.VMEM((seq_len + 2 * CONV_HALO, w), F32)]
        + [pltpu.VMEM((seq_len + pad, w), F32)] * 4,
        compiler_params=_params(("arbitrary",), VMEM_LIMIT),
        name=f"seq_mixers_{seq_len}",
    )(conf_u, lru_x, lru_g, h0, *weights)


def _head_rms(x):
    return lax.rsqrt(jnp.mean(x * x, axis=-1, keepdims=True) + EPS)


def _swap_halves(x):
    width = x.shape[-1]
    lane = lax.broadcasted_iota(jnp.int32, x.shape, x.ndim - 1)
    up = pltpu.roll(x, width - HEAD_DIM // 4, x.ndim - 1)
    down = pltpu.roll(x, HEAD_DIM // 4, x.ndim - 1)
    return jnp.where((lane % (HEAD_DIM // 2)) < HEAD_DIM // 4, up, down)


def _attend(q_heads, k_bf16, v_bf16):
    scale = HEAD_DIM ** -0.5
    s = lax.dot_general(q_heads.astype(BF16), k_bf16, (((1,), (1,)), ((), ())),
                        preferred_element_type=F32) * scale
    p = jnp.exp(s - jnp.max(s, axis=-1, keepdims=True))
    denom = jnp.sum(p, axis=-1, keepdims=True)
    return _dot(p.astype(BF16), v_bf16) / denom


def _attn_ctx_kernel(q_ref, k_ref, v_ref, qw_ref, kw_ref, o_ref, kn_ref):
    L = q_ref.shape[0]
    q, k, v = q_ref[...], k_ref[...], v_ref[...]
    qw, kw = qw_ref[...], kw_ref[...]
    k_out, o_out = [], []
    for g in range(N_KV_HEADS):
        kh = k[:, g * HEAD_DIM:(g + 1) * HEAD_DIM]
        kh = kh * _head_rms(kh) * kw
        k_out.append(kh)
        qs = []
        for j in range(HEADS_PER_KV):
            h = g * HEADS_PER_KV + j
            qh = q[:, h * HEAD_DIM:(h + 1) * HEAD_DIM]
            qs.append(qh * _head_rms(qh) * qw)
        o = _attend(jnp.concatenate(qs, axis=0), kh.astype(BF16),
                    v[:, g * HEAD_DIM:(g + 1) * HEAD_DIM].astype(BF16))
        o_out += [o[j * L:(j + 1) * L] for j in range(HEADS_PER_KV)]
    kn_ref[...] = jnp.concatenate(k_out, axis=-1)
    o_ref[...] = jnp.concatenate(o_out, axis=-1)


def _attention_ctx(seq_len, n_seq, q, k, v, qw, kw):
    kvw = k.shape[-1]
    row_spec = lambda width: pl.BlockSpec((seq_len, width), lambda i: (i, 0))
    full = lambda a: pl.BlockSpec(a.shape, lambda i: (0,) * a.ndim)
    return pl.pallas_call(
        _attn_ctx_kernel,
        out_shape=[jax.ShapeDtypeStruct((n_seq * seq_len, q.shape[-1]), F32),
                   jax.ShapeDtypeStruct((n_seq * seq_len, kvw), F32)],
        grid=(n_seq,),
        in_specs=[row_spec(q.shape[-1]), row_spec(kvw), row_spec(kvw), full(qw), full(kw)],
        out_specs=[row_spec(q.shape[-1]), row_spec(kvw)],
        compiler_params=_params(("arbitrary",), VMEM_LIMIT),
        name="attention_ctx",
    )(q, k, v, qw, kw)


def _attn_lat_kernel(q_ref, k_ref, v_ref, ck_ref, cv_ref, qw_ref, kw_ref, cq_ref, sq_ref, ck_t_ref,
                     sk_t_ref, o_ref, kall_ref, vall_ref):
    L = k_ref.shape[0]
    tq = q_ref.shape[0]

    @pl.when(pl.program_id(1) == 0)
    def _():
        k = k_ref[...]
        t = k * kw_ref[...]
        rot = t * ck_t_ref[...] + _swap_halves(t) * sk_t_ref[...]
        parts = []
        for g in range(N_KV_HEADS):
            sl = slice(g * HEAD_DIM, (g + 1) * HEAD_DIM)
            parts.append(rot[:, sl] * _head_rms(k[:, sl]))
        kall_ref[0:L, :] = jnp.concatenate(parts, axis=-1).astype(BF16)
        kall_ref[L:, :] = ck_ref[...].astype(BF16)
        vall_ref[0:L, :] = v_ref[...].astype(BF16)
        vall_ref[L:, :] = cv_ref[...].astype(BF16)

    q = q_ref[...]
    t = q * qw_ref[...]
    rot = t * cq_ref[...] + _swap_halves(t) * sq_ref[...]
    o_out = []
    for g in range(N_KV_HEADS):
        qs = []
        for j in range(HEADS_PER_KV):
            sl = slice((g * HEADS_PER_KV + j) * HEAD_DIM, (g * HEADS_PER_KV + j + 1) * HEAD_DIM)
            qs.append(rot[:, sl] * _head_rms(q[:, sl]))
        sl = slice(g * HEAD_DIM, (g + 1) * HEAD_DIM)
        o = _attend(jnp.concatenate(qs, axis=0), kall_ref[:, sl], vall_ref[:, sl])
        o_out += [o[j * tq:(j + 1) * tq] for j in range(HEADS_PER_KV)]
    o_ref[...] = jnp.concatenate(o_out, axis=-1)


def _attention_lat(seq_len, n_seq, row0, layer, q, k, v, cache_k, cache_v, qw8, kw2, rope):
    tq = TOKEN_TILE
    nq = seq_len // tq
    qwid, kvw = q.shape[-1], k.shape[-1]
    past = cache_k.shape[2]
    b0q = row0 // tq
    b0s = row0 // seq_len
    cq, sq, ck, sk = rope
    full = lambda a: pl.BlockSpec(a.shape, lambda b, j: (0,) * a.ndim)
    seq_spec = pl.BlockSpec((seq_len, kvw), lambda b, j: (b0s + b, 0))
    cache_spec = pl.BlockSpec((None, None, past, kvw), lambda b, j: (b, layer, 0, 0))
    q_spec = pl.BlockSpec((tq, qwid), lambda b, j: (b0q + b * nq + j, 0))
    rope_q_spec = pl.BlockSpec((tq, qwid), lambda b, j: (j, 0))
    return pl.pallas_call(
        _attn_lat_kernel,
        out_shape=jax.ShapeDtypeStruct((n_seq * seq_len, qwid), F32),
        grid=(n_seq, nq),
        in_specs=[q_spec, seq_spec, seq_spec, cache_spec, cache_spec, full(qw8), full(kw2),
                  rope_q_spec, rope_q_spec, full(ck), full(sk)],
        out_specs=pl.BlockSpec((tq, qwid), lambda b, j: (b * nq + j, 0)),
        scratch_shapes=[pltpu.VMEM((seq_len + past, kvw), BF16)] * 2,
        compiler_params=_params(("arbitrary", "arbitrary"), VMEM_LIMIT),
        name="attention_lat",
    )(q, k, v, cache_k, cache_v, qw8, kw2, cq, sq, ck, sk)


def _rope_tables(seq_len):
    t = jnp.arange(seq_len)
    row = (t // GRID_W).astype(F32)
    col = (t % GRID_W).astype(F32)
    half = HEAD_DIM // 2
    freqs = ROPE_THETA ** (-jnp.arange(0, half, 2, dtype=F32) / half)
    ang_r, ang_c = row[:, None] * freqs, col[:, None] * freqs
    cos = jnp.concatenate([jnp.cos(ang_r)] * 2 + [jnp.cos(ang_c)] * 2, axis=-1)
    sin = jnp.concatenate([-jnp.sin(ang_r), jnp.sin(ang_r), -jnp.sin(ang_c), jnp.sin(ang_c)], axis=-1)
    return (jnp.tile(cos, (1, N_HEADS)), jnp.tile(sin, (1, N_HEADS)),
            jnp.tile(cos, (1, N_KV_HEADS)), jnp.tile(sin, (1, N_KV_HEADS)))


def _post_kernel(ctx_tiles, x_ref, co_c_ref, ao_c_ref, lo_c_ref, co_l_ref, ao_l_ref, lo_l_ref, mod_ref,
                 wout_ref, nw_ref, rw_ref, rb_ref,
                 x1_ref, h2_ref, topi_ref, gates_ref, rank_ref, counts_ref, carry_ref):
    tm = x_ref.shape[0]
    cw, aw = co_c_ref.shape[-1], ao_c_ref.shape[-1]
    is_ctx = pl.program_id(0) < ctx_tiles

    @pl.when(pl.program_id(0) == 0)
    def _():
        carry_ref[...] = jnp.zeros_like(carry_ref)

    m = mod_ref[...]
    gate1, shift2, scale2 = m[2:3], m[3:4], m[4:5]
    pick = lambda c_ref, l_ref: jnp.where(is_ctx, c_ref[...], l_ref[...]).astype(BF16)
    mixed = (_dot(pick(co_c_ref, co_l_ref), wout_ref[0:cw, :])
             + _dot(pick(ao_c_ref, ao_l_ref), wout_ref[cw:cw + aw, :])
             + _dot(pick(lo_c_ref, lo_l_ref), wout_ref[cw + aw:, :]))
    x1 = x_ref[...] + gate1 * mixed
    x1_ref[...] = x1
    h2 = x1 * lax.rsqrt(jnp.mean(x1 * x1, axis=-1, keepdims=True) + EPS) * nw_ref[...]
    h2 = h2 * (1.0 + scale2) + shift2
    h2_ref[...] = h2

    lane = lax.broadcasted_iota(jnp.int32, (tm, LANES), 1)
    lane_f = lane.astype(F32)
    logits = jnp.where(lane < N_EXPERTS, _dot3(h2, rw_ref[...]) + rb_ref[...], -jnp.inf)
    top_v, onehots = [], []
    topi = jnp.zeros((tm, LANES), F32)
    for k in range(TOP_K):
        mx = jnp.max(logits, axis=-1, keepdims=True)
        idx = jnp.min(jnp.where(logits == mx, lane_f, float(LANES)), axis=-1, keepdims=True)
        hit = lane_f == idx
        logits = jnp.where(hit, -jnp.inf, logits)
        top_v.append(mx)
        onehots.append(hit)
        topi = jnp.where(lane == k, idx, topi)
    topi_ref[...] = topi.astype(jnp.int32)
    exps = [jnp.exp(v - top_v[0]) for v in top_v]
    denom = exps[0] + exps[1] + exps[2] + exps[3]
    gates = jnp.zeros((tm, LANES), F32)
    for k in range(TOP_K):
        gates = jnp.where(lane == k, exps[k] / denom, gates)
    gates_ref[...] = gates

    chosen = jnp.zeros((tm, LANES), F32)
    for hit in onehots:
        chosen = jnp.where(hit, 1.0, chosen)
    r_i = lax.broadcasted_iota(jnp.int32, (tm, tm), 0)
    c_i = lax.broadcasted_iota(jnp.int32, (tm, tm), 1)
    lower = jnp.where(c_i < r_i, 1.0, 0.0).astype(BF16)
    before = _dot(lower, chosen.astype(BF16)) + carry_ref[...]
    rank = jnp.zeros((tm, LANES), jnp.int32)
    for k, hit in enumerate(onehots):
        rk = jnp.sum(jnp.where(hit, before, 0.0), axis=-1, keepdims=True).astype(jnp.int32)
        rank = jnp.where(lane == k, rk, rank)
    rank_ref[...] = rank
    carry = carry_ref[...] + jnp.sum(chosen, axis=0, keepdims=True)
    carry_ref[...] = carry
    counts_ref[...] = carry


def _post_mixer(x, ctx_outs, lat_outs, mod_l, mod_row, w_out_bf16, norm_w, router_w, router_b):
    n, d = x.shape
    tm = TOKEN_TILE
    ctx_tiles = ctx_outs[0].shape[0] // tm
    row_spec = lambda a: pl.BlockSpec((tm, a.shape[-1]), lambda i: (i, 0))
    ctx_spec = lambda a: pl.BlockSpec((tm, a.shape[-1]), lambda i: (jnp.minimum(i, ctx_tiles - 1), 0))
    lat_spec = lambda a: pl.BlockSpec((tm, a.shape[-1]), lambda i: (jnp.maximum(i - ctx_tiles, 0), 0))
    full = lambda a: pl.BlockSpec(a.shape, lambda i: (0,) * a.ndim)
    rw = jnp.pad(router_w, ((0, 0), (0, LANES - N_EXPERTS)))
    rb = jnp.pad(router_b, (0, LANES - N_EXPERTS)).reshape(1, LANES)
    nw = norm_w.reshape(1, d)
    lane_tile = pl.BlockSpec((tm, LANES), lambda i: (i, 0))
    return pl.pallas_call(
        functools.partial(_post_kernel, ctx_tiles),
        out_shape=[jax.ShapeDtypeStruct((n, d), F32), jax.ShapeDtypeStruct((n, d), F32),
                   jax.ShapeDtypeStruct((n, LANES), jnp.int32), jax.ShapeDtypeStruct((n, LANES), F32),
                   jax.ShapeDtypeStruct((n, LANES), jnp.int32), jax.ShapeDtypeStruct((1, LANES), F32)],
        grid=(n // tm,),
        in_specs=[row_spec(x)] + [ctx_spec(a) for a in ctx_outs] + [lat_spec(a) for a in lat_outs]
        + [pl.BlockSpec((None, 6, d), lambda i: (mod_row(i), 0, 0)),
           full(w_out_bf16), full(nw), full(rw), full(rb)],
        out_specs=[row_spec(x), row_spec(x), lane_tile, lane_tile, lane_tile,
                   pl.BlockSpec((1, LANES), lambda i: (0, 0))],
        scratch_shapes=[pltpu.VMEM((1, LANES), F32)],
        compiler_params=_params(("arbitrary",), VMEM_LIMIT),
        name="post_mixer",
    )(x, *ctx_outs, *lat_outs, mod_l, w_out_bf16, nw, rw, rb)


def _row_copy(src_ref, src_row, dst_ref, dst_row, sem):
    return pltpu.make_async_copy(src_ref.at[pl.ds(src_row, 1)], dst_ref.at[pl.ds(dst_row, 1)], sem)


def _dispatch_kernel(h2_ref, pos_ref, xs_in_ref, xs_ref, idx_ref, idx_sem, row_sem):
    del xs_in_ref
    tm = h2_ref.shape[0]
    i = pl.program_id(0)
    idx_copy = pltpu.make_async_copy(pos_ref.at[pl.ds(i * tm * TOP_K, tm * TOP_K)], idx_ref, idx_sem)
    idx_copy.start()
    idx_copy.wait()

    def issue(r, carry):
        for k in range(TOP_K):
            _row_copy(h2_ref, r, xs_ref, idx_ref[r * TOP_K + k], row_sem).start()
        return carry

    lax.fori_loop(0, tm, issue, 0)

    def drain(r, carry):
        _row_copy(h2_ref, 0, xs_ref, 0, row_sem).wait()
        return carry

    lax.fori_loop(0, tm * TOP_K, drain, 0)


def _dispatch(h2, pos_flat, n_rows):
    n, d = h2.shape
    tm = TOKEN_TILE
    xs0 = jnp.zeros((n_rows, d), F32)
    return pl.pallas_call(
        _dispatch_kernel,
        out_shape=jax.ShapeDtypeStruct((n_rows, d), F32),
        grid=(n // tm,),
        in_specs=[pl.BlockSpec((tm, d), lambda i: (i, 0)),
                  pl.BlockSpec(memory_space=pl.ANY), pl.BlockSpec(memory_space=pl.ANY)],
        out_specs=pl.BlockSpec(memory_space=pl.ANY),
        scratch_shapes=[pltpu.SMEM((tm * TOP_K,), jnp.int32), pltpu.SemaphoreType.DMA,
                        pltpu.SemaphoreType.DMA],
        input_output_aliases={2: 0},
        compiler_params=_params(("arbitrary",)),
        name="moe_dispatch",
    )(h2, pos_flat, xs0)


def _expert_kernel(te_ref, nu_ref, xs_ref, wgu_ref, bgu_ref, wd_ref, bd_ref, ys_ref, wgu_bf, wd_bf):
    i = pl.program_id(0)
    dff = wd_ref.shape[0]

    @pl.when(i < nu_ref[0])
    def _():
        prev = te_ref[jnp.maximum(i - 1, 0)]

        @pl.when((i == 0) | (te_ref[i] != prev))
        def _():
            wgu_bf[...] = wgu_ref[...].astype(BF16)
            wd_bf[...] = wd_ref[...].astype(BF16)

        gu = _dot(xs_ref[...].astype(BF16), wgu_bf[...]) + bgu_ref[...]
        x_glu = jnp.minimum(gu[:, :dff], SWIGLU_LIMIT)
        x_lin = jnp.clip(gu[:, dff:], -SWIGLU_LIMIT, SWIGLU_LIMIT)
        act = x_glu * jax.nn.sigmoid(SWIGLU_ALPHA * x_glu) * (x_lin + 1.0)
        ys_ref[...] = _dot(act.astype(BF16), wd_bf[...]) + bd_ref[...]

    @pl.when(i >= nu_ref[0])
    def _():
        ys_ref[...] = jnp.zeros_like(ys_ref)


def _experts(layer, tile_expert, n_used, xs, w_gu, b_gu, w_down, b_down):
    n_rows, d = xs.shape
    tm = EXPERT_TILE
    dff2 = w_gu.shape[-1]
    dff = w_down.shape[-2]
    row = lambda i, te, nu: (jnp.minimum(i, nu[0] - 1), 0)
    grid_spec = pltpu.PrefetchScalarGridSpec(
        num_scalar_prefetch=2,
        grid=(n_rows // tm,),
        in_specs=[
            pl.BlockSpec((tm, d), row),
            pl.BlockSpec((None, None, d, dff2), lambda i, te, nu: (layer, te[i], 0, 0)),
            pl.BlockSpec((None, None, 1, dff2), lambda i, te, nu: (layer, te[i], 0, 0)),
            pl.BlockSpec((None, None, dff, d), lambda i, te, nu: (layer, te[i], 0, 0)),
            pl.BlockSpec((None, None, 1, d), lambda i, te, nu: (layer, te[i], 0, 0)),
        ],
        out_specs=pl.BlockSpec((tm, d), lambda i, te, nu: (i, 0)),
        scratch_shapes=[pltpu.VMEM((d, dff2), BF16), pltpu.VMEM((dff, d), BF16)],
    )
    depth, n_e = w_gu.shape[:2]
    return pl.pallas_call(
        _expert_kernel,
        out_shape=jax.ShapeDtypeStruct((n_rows, d), F32),
        grid_spec=grid_spec,
        compiler_params=_params(("arbitrary",), VMEM_LIMIT),
        name="moe_experts",
    )(tile_expert, n_used, xs, w_gu, b_gu.reshape(depth, n_e, 1, dff2), w_down,
      b_down.reshape(depth, n_e, 1, d))


def _combine_kernel(final, x1_ref, gates_ref, mod_ref, fw_ref, pos_ref, ys_ref, out_ref,
                    idx_ref, rows_ref, idx_sem, row_sem):
    tm = x1_ref.shape[0]
    i = pl.program_id(0)
    idx_copy = pltpu.make_async_copy(pos_ref.at[pl.ds(i * tm * TOP_K, tm * TOP_K)], idx_ref, idx_sem)
    idx_copy.start()
    idx_copy.wait()

    def issue(r, carry):
        for k in range(TOP_K):
            _row_copy(ys_ref, idx_ref[r * TOP_K + k], rows_ref.at[k], r, row_sem).start()
        return carry

    lax.fori_loop(0, tm, issue, 0)

    def drain(r, carry):
        _row_copy(ys_ref, 0, rows_ref.at[0], 0, row_sem).wait()
        return carry

    lax.fori_loop(0, tm * TOP_K, drain, 0)

    gates = gates_ref[...]
    moe = gates[:, 0:1] * rows_ref[0]
    for k in range(1, TOP_K):
        moe = moe + gates[:, k:k + 1] * rows_ref[k]
    x2 = x1_ref[...] + mod_ref[5:6, :] * moe
    if final:
        x2 = x2 * lax.rsqrt(jnp.mean(x2 * x2, axis=-1, keepdims=True) + EPS) * fw_ref[...]
    out_ref[...] = x2


def _combine(final, x1, gates, mod_l, mod_row, final_w, pos_flat, ys):
    n, d = x1.shape
    tm = TOKEN_TILE
    return pl.pallas_call(
        functools.partial(_combine_kernel, final),
        out_shape=jax.ShapeDtypeStruct((n, d), F32),
        grid=(n // tm,),
        in_specs=[pl.BlockSpec((tm, d), lambda i: (i, 0)),
                  pl.BlockSpec((tm, LANES), lambda i: (i, 0)),
                  pl.BlockSpec((None, 6, d), lambda i: (mod_row(i), 0, 0)),
                  pl.BlockSpec((1, d), lambda i: (0, 0)),
                  pl.BlockSpec(memory_space=pl.ANY), pl.BlockSpec(memory_space=pl.ANY)],
        out_specs=pl.BlockSpec((tm, d), lambda i: (i, 0)),
        scratch_shapes=[pltpu.SMEM((tm * TOP_K,), jnp.int32), pltpu.VMEM((TOP_K, tm, d), F32),
                        pltpu.SemaphoreType.DMA, pltpu.SemaphoreType.DMA],
        compiler_params=_params(("arbitrary",), VMEM_LIMIT),
        name="moe_combine",
    )(x1, gates, mod_l, final_w.reshape(1, d), pos_flat, ys)


def _routing_tables(counts, topi, rank, n_tiles):
    tm = EXPERT_TILE
    c = counts[0, :N_EXPERTS].astype(jnp.int32)
    tiles = (c + tm - 1) // tm
    tile_end = jnp.cumsum(tiles)
    offs = (tile_end - tiles) * tm
    pos = offs[topi[:, :TOP_K]] + rank[:, :TOP_K]
    n_used = tile_end[-1]
    t = jnp.arange(n_tiles, dtype=jnp.int32)
    te = jnp.searchsorted(tile_end, jnp.minimum(t, n_used - 1), side='right').astype(jnp.int32)
    return pos.reshape(-1), jnp.minimum(te, N_EXPERTS - 1), n_used.reshape(1).astype(jnp.int32)


def _block_diag(w):
    dirs, heads, blk, _ = w.shape
    eye = jnp.eye(heads, dtype=w.dtype)
    full = jnp.einsum('dhij,hg->hidgj', w, eye)
    return full.reshape(heads * blk, dirs * heads * blk)


def kernel(x_prompt, x_sample, cache_k, cache_v, state_lru, c, c_ctx, w_mod, b_mod, norm_mix_w, w_in, conf_conv_w, conf_conv_b, conf_ln_w, conf_ln_b, q_norm_w, k_norm_w, lru_conv_w, lru_conv_b, lru_wa, lru_ba, lru_wx, lru_bx, lru_lambda, w_out, norm_ffn_w, router_w, router_b, w_gu, b_gu, w_down, b_down, final_norm_w):
    batch, seq, d = x_prompt.shape
    dec_batch, dec_seq, _ = x_sample.shape
    depth = w_mod.shape[0]
    n_ctx, n_lat = batch * seq, dec_batch * dec_seq
    n = n_ctx + n_lat
    conf_w = conf_conv_w.shape[-1]
    lru_w = lru_conv_w.shape[-1]
    kv_w = N_KV_HEADS * HEAD_DIM
    attn_w = N_HEADS * HEAD_DIM
    widths = (2 * conf_w, attn_w, kv_w, kv_w, lru_w, lru_w)
    past = cache_k.shape[2]

    ctx_tiles = n_ctx // TOKEN_TILE
    lat_tiles_per_seq = dec_seq // TOKEN_TILE
    mod_row = lambda i: jnp.where(i < ctx_tiles, 0, 1 + (i - ctx_tiles) // lat_tiles_per_seq)

    n_cond = 8
    cvec = jnp.zeros((n_cond, d), F32).at[0].set(c_ctx).at[1:1 + dec_batch].set(c)
    mod = _modulation(cvec, w_mod, b_mod).reshape(depth, n_cond, 6, d)

    x = jnp.concatenate([x_prompt.reshape(n_ctx, d), x_sample.reshape(n_lat, d)], axis=0)
    rope = _rope_tables(dec_seq)
    cache_k4 = cache_k.reshape(dec_batch, depth, past, kv_w)
    cache_v4 = cache_v.reshape(dec_batch, depth, past, kv_w)
    h0_ctx = jnp.zeros((batch, 2, lru_w), F32)
    n_sorted_tiles = n * TOP_K // EXPERT_TILE + N_EXPERTS
    n_sorted = n_sorted_tiles * EXPERT_TILE

    new_k, new_v, new_h = [], [], []
    for l in range(depth):
        p = {
            'conf_conv_w': conf_conv_w[l], 'conf_conv_b': conf_conv_b[l].reshape(1, conf_w),
            'conf_ln_w': conf_ln_w[l].reshape(1, conf_w), 'conf_ln_b': conf_ln_b[l].reshape(1, conf_w),
            'lru_conv_w': lru_conv_w[l], 'lru_conv_b': lru_conv_b[l].reshape(1, lru_w),
            'lru_wg': jnp.concatenate([_block_diag(lru_wa[l]), _block_diag(lru_wx[l])], axis=-1).astype(BF16),
            'lru_bg': jnp.concatenate([lru_ba[l].reshape(-1), lru_bx[l].reshape(-1)]).reshape(1, 4 * lru_w),
            'lru_lam': lru_lambda[l].reshape(1, 2 * lru_w),
        }
        conf_u, q, k, v, lru_x, lru_g = _pre_mixer(x, mod[l], mod_row, norm_mix_w[l], w_in[l].astype(BF16), widths)

        conf_c, lru_c, h_last = _seq_mixers(seq, batch, 0, conf_u, lru_x, lru_g, h0_ctx, p)
        conf_l, lru_l, _ = _seq_mixers(dec_seq, dec_batch, n_ctx, conf_u, lru_x, lru_g, state_lru[:, l], p)

        qw = q_norm_w[l].reshape(1, HEAD_DIM)
        kw = k_norm_w[l].reshape(1, HEAD_DIM)
        attn_c, k_ctx = _attention_ctx(seq, batch, q, k, v, qw, kw)
        attn_l = _attention_lat(dec_seq, dec_batch, n_ctx, l, q, k, v, cache_k4, cache_v4,
                                jnp.tile(qw, (1, N_HEADS)), jnp.tile(kw, (1, N_KV_HEADS)), rope)
        new_k.append(k_ctx.reshape(batch, seq, N_KV_HEADS, HEAD_DIM))
        new_v.append(v[:n_ctx].reshape(batch, seq, N_KV_HEADS, HEAD_DIM))
        new_h.append(h_last)

        x1, h2, topi, gates, rank, counts = _post_mixer(
            x, (conf_c, attn_c, lru_c), (conf_l, attn_l, lru_l), mod[l], mod_row, w_out[l].astype(BF16),
            norm_ffn_w[l], router_w[l], router_b[l])
        pos, tile_expert, n_used = _routing_tables(counts, topi, rank, n_sorted_tiles)
        xs = _dispatch(h2, pos, n_sorted)
        ys = _experts(l, tile_expert, n_used, xs, w_gu, b_gu, w_down, b_down)
        x = _combine(l == depth - 1, x1, gates, mod[l], mod_row, final_norm_w, pos, ys)

    y_prompt = x[:n_ctx].reshape(batch, seq, d)
    y_sample = x[n_ctx:].reshape(dec_batch, dec_seq, d)
    return (y_prompt, y_sample, jnp.stack(new_k, axis=1), jnp.stack(new_v, axis=1), jnp.stack(new_h, axis=1))
```

```python
import functools

import jax
import jax.numpy as jnp
from jax import lax
from jax.experimental import pallas as pl
from jax.experimental.pallas import tpu as pltpu

F32 = jnp.float32
BF16 = jnp.bfloat16

HEAD_DIM = 64
N_HEADS = 8
N_KV_HEADS = 2
HEADS_PER_KV = N_HEADS // N_KV_HEADS
CONF_KERNEL = 31
LRU_CONV = 4
LRU_C = 8.0
N_EXPERTS = 32
TOP_K = 4
SWIGLU_ALPHA = 1.702
SWIGLU_LIMIT = 7.0
ROPE_THETA = 10000.0
GRID_W = 64
EPS = 1e-6

LANES = 128
SUBLANES = 8
ISSUE_UNROLL = 4
TOKEN_TILE = 256
EXPERT_TILE = 256
CONV_HALO = 16
ROW_CHUNK = 64
SCAN_CHUNK = 32
VMEM_LIMIT = 56 * 1024 * 1024


def _params(sem, vmem=None):
    return pltpu.CompilerParams(dimension_semantics=sem, vmem_limit_bytes=vmem)


def _split_bf16(x):
    hi = x.astype(BF16)
    lo = (x - hi.astype(F32)).astype(BF16)
    return hi, lo


def _dot(a, b):
    return jnp.dot(a, b, preferred_element_type=F32)


def _dot3(a, b):
    a_hi, a_lo = _split_bf16(a)
    b_hi, b_lo = _split_bf16(b)
    return _dot(a_hi, b_hi) + (_dot(a_hi, b_lo) + _dot(a_lo, b_hi))


def _mod_kernel(c_ref, w_ref, b_ref, o_ref):
    c = c_ref[...]
    s = c * jax.nn.sigmoid(c)
    o_ref[...] = _dot3(s, w_ref[...]) + b_ref[...]


def _modulation(cvec, w_mod, b_mod):
    depth, d, d6 = w_mod.shape
    tn = 768
    return pl.pallas_call(
        _mod_kernel,
        out_shape=jax.ShapeDtypeStruct((depth, cvec.shape[0], d6), F32),
        grid=(depth, d6 // tn),
        in_specs=[
            pl.BlockSpec(cvec.shape, lambda l, j: (0, 0)),
            pl.BlockSpec((None, d, tn), lambda l, j: (l, 0, j)),
            pl.BlockSpec((None, 1, tn), lambda l, j: (l, 0, j)),
        ],
        out_specs=pl.BlockSpec((None, cvec.shape[0], tn), lambda l, j: (l, 0, j)),
        compiler_params=_params(("arbitrary", "arbitrary")),
        name="modulation",
    )(cvec, w_mod, b_mod.reshape(depth, 1, d6))


def _pre_kernel(x_ref, mod_ref, nw_ref, win_ref, conf_ref, q_ref, k_ref, v_ref, lx_ref, lg_ref):
    x = x_ref[...]
    m = mod_ref[...]
    shift, scale = m[0:1], m[1:2]
    h = x * lax.rsqrt(jnp.mean(x * x, axis=-1, keepdims=True) + EPS) * nw_ref[...]
    h = h * (1.0 + scale) + shift
    proj = _dot(h.astype(BF16), win_ref[...])
    col = 0
    for ref in (conf_ref, q_ref, k_ref, v_ref, lx_ref, lg_ref):
        w = ref.shape[-1]
        ref[...] = proj[:, col:col + w]
        col += w


def _pre_mixer(x, mod_l, mod_row, norm_w, w_in_bf16, widths):
    n, d = x.shape
    return pl.pallas_call(
        _pre_kernel,
        out_shape=[jax.ShapeDtypeStruct((n, w), F32) for w in widths],
        grid=(n // TOKEN_TILE,),
        in_specs=[
            pl.BlockSpec((TOKEN_TILE, d), lambda i: (i, 0)),
            pl.BlockSpec((None, 6, d), lambda i: (mod_row(i), 0, 0)),
            pl.BlockSpec((1, d), lambda i: (0, 0)),
            pl.BlockSpec(w_in_bf16.shape, lambda i: (0, 0)),
        ],
        out_specs=[pl.BlockSpec((TOKEN_TILE, w), lambda i: (i, 0)) for w in widths],
        compiler_params=_params(("arbitrary",)),
        name="pre_mixer",
    )(x, mod_l, norm_w.reshape(1, d), w_in_bf16)


def _gelu_tanh(x):
    return 0.5 * x * (1.0 + jnp.tanh(0.7978845608028654 * (x + 0.044715 * (x * x * x))))


def _seq_kernel(seq_len, conf_ref, lx_ref, lg_ref, h0_ref, ccw_ref, ccb_ref, lnw_ref, lnb_ref,
                lcw_ref, lcb_ref, wg_ref, bg_ref, lam_ref,
                co_ref, lo_ref, hl_ref, pad_ref, af_ref, uf_ref, ab_ref, ub_ref):
    L = seq_len
    W = co_ref.shape[-1]
    pad = L // 2
    zeros_halo = jnp.zeros((CONV_HALO, W), F32)

    pad_ref[0:CONV_HALO, :] = zeros_halo
    pad_ref[CONV_HALO + L:2 * CONV_HALO + L, :] = zeros_halo
    for c in range(L // ROW_CHUNK):
        r0 = c * ROW_CHUNK
        u = conf_ref[r0:r0 + ROW_CHUNK, :]
        pad_ref[CONV_HALO + r0:CONV_HALO + r0 + ROW_CHUNK, :] = u[:, :W] * jax.nn.sigmoid(u[:, W:])
    left = CONF_KERNEL // 2
    for c in range(L // ROW_CHUNK):
        r0 = c * ROW_CHUNK
        acc = jnp.zeros((ROW_CHUNK, W), F32) + ccb_ref[...]
        for k in range(CONF_KERNEL):
            start = CONV_HALO - left + k + r0
            acc = acc + ccw_ref[k:k + 1, :] * pad_ref[start:start + ROW_CHUNK, :]
        mu = jnp.mean(acc, axis=-1, keepdims=True)
        cen = acc - mu
        var = jnp.mean(cen * cen, axis=-1, keepdims=True)
        y = cen * lax.rsqrt(var + EPS) * lnw_ref[...] + lnb_ref[...]
        co_ref[r0:r0 + ROW_CHUNK, :] = y * jax.nn.sigmoid(y)

    for c in range(L // ROW_CHUNK):
        r0 = c * ROW_CHUNK
        pad_ref[CONV_HALO + r0:CONV_HALO + r0 + ROW_CHUNK, :] = lx_ref[r0:r0 + ROW_CHUNK, :]
    ones_pad = jnp.ones((pad, W), F32)
    zeros_pad = jnp.zeros((pad, W), F32)
    af_ref[0:pad, :] = ones_pad
    uf_ref[0:pad, :] = zeros_pad
    ab_ref[L:L + pad, :] = ones_pad
    ub_ref[L:L + pad, :] = zeros_pad
    sp = jax.nn.softplus(-lam_ref[...])
    h0 = h0_ref[...]
    left = LRU_CONV // 2
    n_chunks = L // ROW_CHUNK
    for c in range(n_chunks):
        r0 = c * ROW_CHUNK
        xc = jnp.zeros((ROW_CHUNK, W), F32) + lcb_ref[...]
        for k in range(LRU_CONV):
            start = CONV_HALO - left + k + r0
            xc = xc + lcw_ref[k:k + 1, :] * pad_ref[start:start + ROW_CHUNK, :]
        g = _dot(xc.astype(BF16), wg_ref[...]) + bg_ref[...]
        r = jax.nn.sigmoid(g[:, :2 * W])
        i = jax.nn.sigmoid(g[:, 2 * W:])
        a = jnp.exp((-LRU_C) * r * sp)
        xc2 = jnp.concatenate([xc, xc], axis=-1)
        u = jnp.sqrt(1.0 - a * a) * i * xc2
        a_f, a_b, u_f, u_b = a[:, :W], a[:, W:], u[:, :W], u[:, W:]
        row = lax.broadcasted_iota(jnp.int32, (ROW_CHUNK, W), 0)
        if c == 0:
            u_f = jnp.where(row == 0, u_f + a_f * h0[0:1], u_f)
        if c == n_chunks - 1:
            u_b = jnp.where(row == ROW_CHUNK - 1, u_b + a_b * h0[1:2], u_b)
        af_ref[pad + r0:pad + r0 + ROW_CHUNK, :] = a_f
        uf_ref[pad + r0:pad + r0 + ROW_CHUNK, :] = u_f
        ab_ref[r0:r0 + ROW_CHUNK, :] = a_b
        ub_ref[r0:r0 + ROW_CHUNK, :] = u_b

    n_sc = L // SCAN_CHUNK
    s = 1
    while s < L:
        for c in reversed(range(n_sc)):
            r0 = c * SCAN_CHUNK
            if r0 + SCAN_CHUNK <= s:
                continue
            cur = slice(pad + r0, pad + r0 + SCAN_CHUNK)
            sh = slice(pad + r0 - s, pad + r0 - s + SCAN_CHUNK)
            a_cur = af_ref[cur, :]
            uf_ref[cur, :] = uf_ref[cur, :] + a_cur * uf_ref[sh, :]
            af_ref[cur, :] = a_cur * af_ref[sh, :]
        for c in range(n_sc):
            r0 = c * SCAN_CHUNK
            if r0 >= L - s:
                continue
            cur = slice(r0, r0 + SCAN_CHUNK)
            sh = slice(r0 + s, r0 + s + SCAN_CHUNK)
            a_cur = ab_ref[cur, :]
            ub_ref[cur, :] = ub_ref[cur, :] + a_cur * ub_ref[sh, :]
            ab_ref[cur, :] = a_cur * ab_ref[sh, :]
        s *= 2

    for c in range(n_chunks):
        r0 = c * ROW_CHUNK
        h = uf_ref[pad + r0:pad + r0 + ROW_CHUNK, :] + ub_ref[r0:r0 + ROW_CHUNK, :]
        lo_ref[r0:r0 + ROW_CHUNK, :] = h * _gelu_tanh(lg_ref[r0:r0 + ROW_CHUNK, :])
    hl_ref[0:1, :] = uf_ref[pad + L - 1:pad + L, :]
    hl_ref[1:2, :] = ub_ref[0:1, :]


def _seq_mixers(seq_len, n_seq, row0, conf_u, lru_x, lru_g, h0, p):
    w = lru_x.shape[-1]
    b0 = row0 // seq_len
    pad = seq_len // 2
    in_spec = lambda width: pl.BlockSpec((seq_len, width), lambda i: (b0 + i, 0))
    out_spec = pl.BlockSpec((seq_len, w), lambda i: (i, 0))
    state_spec = pl.BlockSpec((None, 2, w), lambda i: (i, 0, 0))
    full = lambda a: pl.BlockSpec(a.shape, lambda i: (0,) * a.ndim)
    weights = (p['conf_conv_w'], p['conf_conv_b'], p['conf_ln_w'], p['conf_ln_b'],
               p['lru_conv_w'], p['lru_conv_b'], p['lru_wg'], p['lru_bg'], p['lru_lam'])
    return pl.pallas_call(
        functools.partial(_seq_kernel, seq_len),
        out_shape=[jax.ShapeDtypeStruct((n_seq * seq_len, w), F32)] * 2
        + [jax.ShapeDtypeStruct((n_seq, 2, w), F32)],
        grid=(n_seq,),
        in_specs=[in_spec(2 * w), in_spec(w), in_spec(w), state_spec] + [full(a) for a in weights],
        out_specs=[out_spec, out_spec, state_spec],
        scratch_shapes=[pltpu.VMEM((seq_len + 2 * CONV_HALO, w), F32)]
        + [pltpu.VMEM((seq_len + pad, w), F32)] * 4,
        compiler_params=_params(("arbitrary",), VMEM_LIMIT),
        name=f"seq_mixers_{seq_len}",
    )(conf_u, lru_x, lru_g, h0, *weights)


def _head_rms(x):
    return lax.rsqrt(jnp.mean(x * x, axis=-1, keepdims=True) + EPS)


def _swap_halves(x):
    width = x.shape[-1]
    lane = lax.broadcasted_iota(jnp.int32, x.shape, x.ndim - 1)
    up = pltpu.roll(x, width - HEAD_DIM // 4, x.ndim - 1)
    down = pltpu.roll(x, HEAD_DIM // 4, x.ndim - 1)
    return jnp.where((lane % (HEAD_DIM // 2)) < HEAD_DIM // 4, up, down)


def _attend(q_heads, k_bf16, v_bf16):
    scale = HEAD_DIM ** -0.5
    s = lax.dot_general(q_heads.astype(BF16), k_bf16, (((1,), (1,)), ((), ())),
                        preferred_element_type=F32) * scale
    p = jnp.exp(s - jnp.max(s, axis=-1, keepdims=True))
    denom = jnp.sum(p, axis=-1, keepdims=True)
    return _dot(p.astype(BF16), v_bf16) / denom


def _attn_ctx_kernel(q_ref, k_ref, v_ref, qw_ref, kw_ref, o_ref, kn_ref):
    L = q_ref.shape[0]
    q, k, v = q_ref[...], k_ref[...], v_ref[...]
    qw, kw = qw_ref[...], kw_ref[...]
    k_out, o_out = [], []
    for g in range(N_KV_HEADS):
        kh = k[:, g * HEAD_DIM:(g + 1) * HEAD_DIM]
        kh = kh * _head_rms(kh) * kw
        k_out.append(kh)
        qs = []
        for j in range(HEADS_PER_KV):
            h = g * HEADS_PER_KV + j
            qh = q[:, h * HEAD_DIM:(h + 1) * HEAD_DIM]
            qs.append(qh * _head_rms(qh) * qw)
        o = _attend(jnp.concatenate(qs, axis=0), kh.astype(BF16),
                    v[:, g * HEAD_DIM:(g + 1) * HEAD_DIM].astype(BF16))
        o_out += [o[j * L:(j + 1) * L] for j in range(HEADS_PER_KV)]
    kn_ref[...] = jnp.concatenate(k_out, axis=-1)
    o_ref[...] = jnp.concatenate(o_out, axis=-1)


def _attention_ctx(seq_len, n_seq, q, k, v, qw, kw):
    kvw = k.shape[-1]
    row_spec = lambda width: pl.BlockSpec((seq_len, width), lambda i: (i, 0))
    full = lambda a: pl.BlockSpec(a.shape, lambda i: (0,) * a.ndim)
    return pl.pallas_call(
        _attn_ctx_kernel,
        out_shape=[jax.ShapeDtypeStruct((n_seq * seq_len, q.shape[-1]), F32),
                   jax.ShapeDtypeStruct((n_seq * seq_len, kvw), F32)],
        grid=(n_seq,),
        in_specs=[row_spec(q.shape[-1]), row_spec(kvw), row_spec(kvw), full(qw), full(kw)],
        out_specs=[row_spec(q.shape[-1]), row_spec(kvw)],
        compiler_params=_params(("arbitrary",), VMEM_LIMIT),
        name="attention_ctx",
    )(q, k, v, qw, kw)


def _attn_lat_kernel(q_ref, k_ref, v_ref, ck_ref, cv_ref, qw_ref, kw_ref, cq_ref, sq_ref, ck_t_ref,
                     sk_t_ref, o_ref, kall_ref, vall_ref):
    L = k_ref.shape[0]
    tq = q_ref.shape[0]

    @pl.when(pl.program_id(1) == 0)
    def _():
        k = k_ref[...]
        t = k * kw_ref[...]
        rot = t * ck_t_ref[...] + _swap_halves(t) * sk_t_ref[...]
        parts = []
        for g in range(N_KV_HEADS):
            sl = slice(g * HEAD_DIM, (g + 1) * HEAD_DIM)
            parts.append(rot[:, sl] * _head_rms(k[:, sl]))
        kall_ref[0:L, :] = jnp.concatenate(parts, axis=-1).astype(BF16)
        kall_ref[L:, :] = ck_ref[...].astype(BF16)
        vall_ref[0:L, :] = v_ref[...].astype(BF16)
        vall_ref[L:, :] = cv_ref[...].astype(BF16)

    q = q_ref[...]
    t = q * qw_ref[...]
    rot = t * cq_ref[...] + _swap_halves(t) * sq_ref[...]
    o_out = []
    for g in range(N_KV_HEADS):
        qs = []
        for j in range(HEADS_PER_KV):
            sl = slice((g * HEADS_PER_KV + j) * HEAD_DIM, (g * HEADS_PER_KV + j + 1) * HEAD_DIM)
            qs.append(rot[:, sl] * _head_rms(q[:, sl]))
        sl = slice(g * HEAD_DIM, (g + 1) * HEAD_DIM)
        o = _attend(jnp.concatenate(qs, axis=0), kall_ref[:, sl], vall_ref[:, sl])
        o_out += [o[j * tq:(j + 1) * tq] for j in range(HEADS_PER_KV)]
    o_ref[...] = jnp.concatenate(o_out, axis=-1)


def _attention_lat(seq_len, n_seq, row0, layer, q, k, v, cache_k, cache_v, qw8, kw2, rope):
    tq = TOKEN_TILE
    nq = seq_len // tq
    qwid, kvw = q.shape[-1], k.shape[-1]
    past = cache_k.shape[2]
    b0q = row0 // tq
    b0s = row0 // seq_len
    cq, sq, ck, sk = rope
    full = lambda a: pl.BlockSpec(a.shape, lambda b, j: (0,) * a.ndim)
    seq_spec = pl.BlockSpec((seq_len, kvw), lambda b, j: (b0s + b, 0))
    cache_spec = pl.BlockSpec((None, None, past, kvw), lambda b, j: (b, layer, 0, 0))
    q_spec = pl.BlockSpec((tq, qwid), lambda b, j: (b0q + b * nq + j, 0))
    rope_q_spec = pl.BlockSpec((tq, qwid), lambda b, j: (j, 0))
    return pl.pallas_call(
        _attn_lat_kernel,
        out_shape=jax.ShapeDtypeStruct((n_seq * seq_len, qwid), F32),
        grid=(n_seq, nq),
        in_specs=[q_spec, seq_spec, seq_spec, cache_spec, cache_spec, full(qw8), full(kw2),
                  rope_q_spec, rope_q_spec, full(ck), full(sk)],
        out_specs=pl.BlockSpec((tq, qwid), lambda b, j: (b * nq + j, 0)),
        scratch_shapes=[pltpu.VMEM((seq_len + past, kvw), BF16)] * 2,
        compiler_params=_params(("arbitrary", "arbitrary"), VMEM_LIMIT),
        name="attention_lat",
    )(q, k, v, cache_k, cache_v, qw8, kw2, cq, sq, ck, sk)


def _rope_tables(seq_len):
    t = jnp.arange(seq_len)
    row = (t // GRID_W).astype(F32)
    col = (t % GRID_W).astype(F32)
    half = HEAD_DIM // 2
    freqs = ROPE_THETA ** (-jnp.arange(0, half, 2, dtype=F32) / half)
    ang_r, ang_c = row[:, None] * freqs, col[:, None] * freqs
    cos = jnp.concatenate([jnp.cos(ang_r)] * 2 + [jnp.cos(ang_c)] * 2, axis=-1)
    sin = jnp.concatenate([-jnp.sin(ang_r), jnp.sin(ang_r), -jnp.sin(ang_c), jnp.sin(ang_c)], axis=-1)
    return (jnp.tile(cos, (1, N_HEADS)), jnp.tile(sin, (1, N_HEADS)),
            jnp.tile(cos, (1, N_KV_HEADS)), jnp.tile(sin, (1, N_KV_HEADS)))


def _post_kernel(ctx_tiles, x_ref, co_c_ref, ao_c_ref, lo_c_ref, co_l_ref, ao_l_ref, lo_l_ref, mod_ref,
                 wout_ref, nw_ref, rw_ref, rb_ref,
                 x1_ref, h2_ref, topi_ref, gates_ref, rank_ref, counts_ref, carry_ref):
    tm = x_ref.shape[0]
    cw, aw = co_c_ref.shape[-1], ao_c_ref.shape[-1]
    is_ctx = pl.program_id(0) < ctx_tiles

    @pl.when(pl.program_id(0) == 0)
    def _():
        carry_ref[...] = jnp.zeros_like(carry_ref)

    m = mod_ref[...]
    gate1, shift2, scale2 = m[2:3], m[3:4], m[4:5]
    pick = lambda c_ref, l_ref: jnp.where(is_ctx, c_ref[...], l_ref[...]).astype(BF16)
    mixed = (_dot(pick(co_c_ref, co_l_ref), wout_ref[0:cw, :])
             + _dot(pick(ao_c_ref, ao_l_ref), wout_ref[cw:cw + aw, :])
             + _dot(pick(lo_c_ref, lo_l_ref), wout_ref[cw + aw:, :]))
    x1 = x_ref[...] + gate1 * mixed
    x1_ref[...] = x1
    h2 = x1 * lax.rsqrt(jnp.mean(x1 * x1, axis=-1, keepdims=True) + EPS) * nw_ref[...]
    h2 = h2 * (1.0 + scale2) + shift2
    _store_slabs(h2_ref, h2)

    lane = lax.broadcasted_iota(jnp.int32, (tm, LANES), 1)
    lane_f = lane.astype(F32)
    logits = jnp.where(lane < N_EXPERTS, _dot3(h2, rw_ref[...]) + rb_ref[...], -jnp.inf)
    top_v, onehots = [], []
    topi = jnp.zeros((tm, LANES), F32)
    for k in range(TOP_K):
        mx = jnp.max(logits, axis=-1, keepdims=True)
        idx = jnp.min(jnp.where(logits == mx, lane_f, float(LANES)), axis=-1, keepdims=True)
        hit = lane_f == idx
        logits = jnp.where(hit, -jnp.inf, logits)
        top_v.append(mx)
        onehots.append(hit)
        topi = jnp.where(lane == k, idx, topi)
    topi_ref[...] = topi.astype(jnp.int32)
    exps = [jnp.exp(v - top_v[0]) for v in top_v]
    denom = exps[0] + exps[1] + exps[2] + exps[3]
    gates = jnp.zeros((tm, LANES), F32)
    for k in range(TOP_K):
        gates = jnp.where(lane == k, exps[k] / denom, gates)
    gates_ref[...] = gates

    chosen = jnp.zeros((tm, LANES), F32)
    for hit in onehots:
        chosen = jnp.where(hit, 1.0, chosen)
    r_i = lax.broadcasted_iota(jnp.int32, (tm, tm), 0)
    c_i = lax.broadcasted_iota(jnp.int32, (tm, tm), 1)
    lower = jnp.where(c_i < r_i, 1.0, 0.0).astype(BF16)
    before = _dot(lower, chosen.astype(BF16)) + carry_ref[...]
    rank = jnp.zeros((tm, LANES), jnp.int32)
    for k, hit in enumerate(onehots):
        rk = jnp.sum(jnp.where(hit, before, 0.0), axis=-1, keepdims=True).astype(jnp.int32)
        rank = jnp.where(lane == k, rk, rank)
    rank_ref[...] = rank
    carry = carry_ref[...] + jnp.sum(chosen, axis=0, keepdims=True)
    carry_ref[...] = carry
    counts_ref[...] = carry


def _post_mixer(x, ctx_outs, lat_outs, mod_l, mod_row, w_out_bf16, norm_w, router_w, router_b):
    n, d = x.shape
    tm = TOKEN_TILE
    ctx_tiles = ctx_outs[0].shape[0] // tm
    row_spec = lambda a: pl.BlockSpec((tm, a.shape[-1]), lambda i: (i, 0))
    ctx_spec = lambda a: pl.BlockSpec((tm, a.shape[-1]), lambda i: (jnp.minimum(i, ctx_tiles - 1), 0))
    lat_spec = lambda a: pl.BlockSpec((tm, a.shape[-1]), lambda i: (jnp.maximum(i - ctx_tiles, 0), 0))
    full = lambda a: pl.BlockSpec(a.shape, lambda i: (0,) * a.ndim)
    rw = jnp.pad(router_w, ((0, 0), (0, LANES - N_EXPERTS)))
    rb = jnp.pad(router_b, (0, LANES - N_EXPERTS)).reshape(1, LANES)
    nw = norm_w.reshape(1, d)
    lane_tile = pl.BlockSpec((tm, LANES), lambda i: (i, 0))
    return pl.pallas_call(
        functools.partial(_post_kernel, ctx_tiles),
        out_shape=[jax.ShapeDtypeStruct((n, d), F32), jax.ShapeDtypeStruct((n * SUBLANES, LANES), F32),
                   jax.ShapeDtypeStruct((n, LANES), jnp.int32), jax.ShapeDtypeStruct((n, LANES), F32),
                   jax.ShapeDtypeStruct((n, LANES), jnp.int32), jax.ShapeDtypeStruct((1, LANES), F32)],
        grid=(n // tm,),
        in_specs=[row_spec(x)] + [ctx_spec(a) for a in ctx_outs] + [lat_spec(a) for a in lat_outs]
        + [pl.BlockSpec((None, 6, d), lambda i: (mod_row(i), 0, 0)),
           full(w_out_bf16), full(nw), full(rw), full(rb)],
        out_specs=[row_spec(x), pl.BlockSpec((tm * SUBLANES, LANES), lambda i: (i, 0)),
                   lane_tile, lane_tile, lane_tile,
                   pl.BlockSpec((1, LANES), lambda i: (0, 0))],
        scratch_shapes=[pltpu.VMEM((1, LANES), F32)],
        compiler_params=_params(("arbitrary",), VMEM_LIMIT),
        name="post_mixer",
    )(x, *ctx_outs, *lat_outs, mod_l, w_out_bf16, nw, rw, rb)


def _slab_copy(src_ref, src_tok, dst_ref, dst_tok, sem):
    src = src_ref.at[pl.ds(pl.multiple_of(src_tok * SUBLANES, SUBLANES), SUBLANES)]
    dst = dst_ref.at[pl.ds(pl.multiple_of(dst_tok * SUBLANES, SUBLANES), SUBLANES)]
    return pltpu.make_async_copy(src, dst, sem)


def _load_slabs(ref, n_tok, lead=()):
    return jnp.concatenate(
        [ref[(*lead, pl.ds(j, n_tok, stride=SUBLANES), slice(None))] for j in range(SUBLANES)], axis=-1)


def _store_slabs(ref, val):
    for j in range(SUBLANES):
        ref[pl.ds(j, val.shape[0], stride=SUBLANES), :] = val[:, j * LANES:(j + 1) * LANES]


def _wait_slabs(hbm_ref, n_tok, sem):
    span = hbm_ref.at[pl.ds(0, n_tok * SUBLANES)]
    pltpu.make_async_copy(span, span, sem).wait()


def _fetch_positions(pos_ref, idx_ref, sem, n):
    copy = pltpu.make_async_copy(pos_ref.at[pl.ds(pl.program_id(0) * n, n)], idx_ref, sem)
    copy.start()
    copy.wait()


def _dispatch_kernel(tend_ref, h2_ref, pos_ref, xs_ref, idx_ref, zero_ref, idx_sem, row_sem):
    tm = h2_ref.shape[0] // SUBLANES
    tile_rows = EXPERT_TILE * SUBLANES

    @pl.when(pl.program_id(0) == 0)
    def _():
        zero_ref[...] = jnp.zeros_like(zero_ref)

        def last_tile_copy(e):
            start = pl.multiple_of((tend_ref[e] - 1) * tile_rows, tile_rows)
            return pltpu.make_async_copy(zero_ref, xs_ref.at[pl.ds(start, tile_rows)], row_sem)

        def has_tiles(e):
            return tend_ref[e] > (tend_ref[e - 1] if e else 0)

        for e in range(N_EXPERTS):
            pl.when(has_tiles(e))(lambda e=e: last_tile_copy(e).start())
        for e in range(N_EXPERTS):
            pl.when(has_tiles(e))(lambda e=e: last_tile_copy(e).wait())

        def spare_tile_copy(t):
            return pltpu.make_async_copy(
                zero_ref, xs_ref.at[pl.ds(pl.multiple_of(t * tile_rows, tile_rows), tile_rows)], row_sem)

        n_used, n_tiles = tend_ref[N_EXPERTS - 1], xs_ref.shape[0] // tile_rows
        lax.fori_loop(n_used, n_tiles, lambda t, c: (spare_tile_copy(t).start(), c)[1], 0)
        lax.fori_loop(n_used, n_tiles, lambda t, c: (spare_tile_copy(t).wait(), c)[1], 0)

    _fetch_positions(pos_ref, idx_ref, idx_sem, tm * TOP_K)

    def issue(r, carry):
        for k in range(TOP_K):
            _slab_copy(h2_ref, r, xs_ref, idx_ref[r * TOP_K + k], row_sem).start(priority=k % 2)
        return carry

    lax.fori_loop(0, tm, issue, 0, unroll=ISSUE_UNROLL)
    _wait_slabs(xs_ref, tm * TOP_K, row_sem)


def _dispatch(h2_slabs, pos_flat, tile_end, n_tiles):
    tm = TOKEN_TILE
    n = h2_slabs.shape[0] // SUBLANES
    grid_spec = pltpu.PrefetchScalarGridSpec(
        num_scalar_prefetch=1,
        grid=(n // tm,),
        in_specs=[pl.BlockSpec((tm * SUBLANES, LANES), lambda i, tend: (i, 0)),
                  pl.BlockSpec(memory_space=pl.ANY)],
        out_specs=pl.BlockSpec(memory_space=pl.ANY),
        scratch_shapes=[pltpu.SMEM((tm * TOP_K,), jnp.int32),
                        pltpu.VMEM((EXPERT_TILE * SUBLANES, LANES), F32),
                        pltpu.SemaphoreType.DMA, pltpu.SemaphoreType.DMA],
    )
    return pl.pallas_call(
        _dispatch_kernel,
        out_shape=jax.ShapeDtypeStruct((n_tiles * EXPERT_TILE * SUBLANES, LANES), F32),
        grid_spec=grid_spec,
        compiler_params=_params(("arbitrary",)),
        name="moe_dispatch",
    )(tile_end, h2_slabs, pos_flat)


def _expert_kernel(te_ref, nu_ref, xs_ref, wgu_ref, bgu_ref, wd_ref, bd_ref, ys_ref, wgu_bf, wd_bf):
    i = pl.program_id(0)
    dff = wd_ref.shape[0]

    @pl.when(i < nu_ref[0])
    def _():
        prev = te_ref[jnp.maximum(i - 1, 0)]

        @pl.when((i == 0) | (te_ref[i] != prev))
        def _():
            wgu_bf[...] = wgu_ref[...].astype(BF16)
            wd_bf[...] = wd_ref[...].astype(BF16)

        x = _load_slabs(xs_ref, EXPERT_TILE)
        gu = _dot(x.astype(BF16), wgu_bf[...]) + bgu_ref[...]
        x_glu = jnp.minimum(gu[:, :dff], SWIGLU_LIMIT)
        x_lin = jnp.clip(gu[:, dff:], -SWIGLU_LIMIT, SWIGLU_LIMIT)
        act = x_glu * jax.nn.sigmoid(SWIGLU_ALPHA * x_glu) * (x_lin + 1.0)
        _store_slabs(ys_ref, _dot(act.astype(BF16), wd_bf[...]) + bd_ref[...])

    @pl.when(i >= nu_ref[0])
    def _():
        ys_ref[...] = jnp.zeros_like(ys_ref)


def _experts(layer, tile_expert, n_used, xs, w_gu, b_gu, w_down, b_down):
    tm = EXPERT_TILE
    d, dff2 = w_gu.shape[-2:]
    dff = w_down.shape[-2]
    slab_tile = (tm * SUBLANES, LANES)
    row = lambda i, te, nu: (jnp.minimum(i, nu[0] - 1), 0)
    grid_spec = pltpu.PrefetchScalarGridSpec(
        num_scalar_prefetch=2,
        grid=(xs.shape[0] // slab_tile[0],),
        in_specs=[
            pl.BlockSpec(slab_tile, row),
            pl.BlockSpec((None, None, d, dff2), lambda i, te, nu: (layer, te[i], 0, 0)),
            pl.BlockSpec((None, None, 1, dff2), lambda i, te, nu: (layer, te[i], 0, 0)),
            pl.BlockSpec((None, None, dff, d), lambda i, te, nu: (layer, te[i], 0, 0)),
            pl.BlockSpec((None, None, 1, d), lambda i, te, nu: (layer, te[i], 0, 0)),
        ],
        out_specs=pl.BlockSpec(slab_tile, lambda i, te, nu: (i, 0)),
        scratch_shapes=[pltpu.VMEM((d, dff2), BF16), pltpu.VMEM((dff, d), BF16)],
    )
    depth, n_e = w_gu.shape[:2]
    return pl.pallas_call(
        _expert_kernel,
        out_shape=jax.ShapeDtypeStruct(xs.shape, F32),
        grid_spec=grid_spec,
        compiler_params=_params(("arbitrary",), VMEM_LIMIT),
        name="moe_experts",
    )(tile_expert, n_used, xs, w_gu, b_gu.reshape(depth, n_e, 1, dff2), w_down,
      b_down.reshape(depth, n_e, 1, d))


def _combine_kernel(final, x1_ref, gates_ref, mod_ref, fw_ref, pos_ref, ys_ref, out_ref,
                    idx_ref, rows_ref, idx_sem, row_sem):
    tm = x1_ref.shape[0]
    _fetch_positions(pos_ref, idx_ref, idx_sem, tm * TOP_K)

    def issue(r, carry):
        for k in range(TOP_K):
            _slab_copy(ys_ref, idx_ref[r * TOP_K + k], rows_ref.at[k], r, row_sem).start(priority=k % 2)
        return carry

    lax.fori_loop(0, tm, issue, 0, unroll=ISSUE_UNROLL)
    _wait_slabs(ys_ref, tm * TOP_K, row_sem)

    gates = gates_ref[...]
    moe = gates[:, 0:1] * _load_slabs(rows_ref, tm, (0,))
    for k in range(1, TOP_K):
        moe = moe + gates[:, k:k + 1] * _load_slabs(rows_ref, tm, (k,))
    x2 = x1_ref[...] + mod_ref[5:6, :] * moe
    if final:
        x2 = x2 * lax.rsqrt(jnp.mean(x2 * x2, axis=-1, keepdims=True) + EPS) * fw_ref[...]
    out_ref[...] = x2


def _combine(final, x1, gates, mod_l, mod_row, final_w, pos_flat, ys):
    n, d = x1.shape
    tm = TOKEN_TILE
    return pl.pallas_call(
        functools.partial(_combine_kernel, final),
        out_shape=jax.ShapeDtypeStruct((n, d), F32),
        grid=(n // tm,),
        in_specs=[pl.BlockSpec((tm, d), lambda i: (i, 0)),
                  pl.BlockSpec((tm, LANES), lambda i: (i, 0)),
                  pl.BlockSpec((None, 6, d), lambda i: (mod_row(i), 0, 0)),
                  pl.BlockSpec((1, d), lambda i: (0, 0)),
                  pl.BlockSpec(memory_space=pl.ANY), pl.BlockSpec(memory_space=pl.ANY)],
        out_specs=pl.BlockSpec((tm, d), lambda i: (i, 0)),
        scratch_shapes=[pltpu.SMEM((tm * TOP_K,), jnp.int32),
                        pltpu.VMEM((TOP_K, tm * SUBLANES, LANES), F32),
                        pltpu.SemaphoreType.DMA, pltpu.SemaphoreType.DMA],
        compiler_params=_params(("arbitrary",), VMEM_LIMIT),
        name="moe_combine",
    )(x1, gates, mod_l, final_w.reshape(1, d), pos_flat, ys)


def _routing_tables(counts, topi, rank, n_tiles):
    tm = EXPERT_TILE
    c = counts[0, :N_EXPERTS].astype(jnp.int32)
    tiles = (c + tm - 1) // tm
    tile_end = jnp.cumsum(tiles)
    offs = (tile_end - tiles) * tm
    pos = offs[topi[:, :TOP_K]] + rank[:, :TOP_K]
    n_used = tile_end[-1]
    t = jnp.minimum(jnp.arange(n_tiles, dtype=jnp.int32), n_used - 1)
    te = jnp.sum((tile_end[None, :] <= t[:, None]).astype(jnp.int32), axis=1)
    return (pos.reshape(-1), tile_end.astype(jnp.int32), jnp.minimum(te, N_EXPERTS - 1),
            n_used.reshape(1).astype(jnp.int32))


def _block_diag(w):
    dirs, heads, blk, _ = w.shape
    eye = jnp.eye(heads, dtype=w.dtype)
    full = jnp.einsum('dhij,hg->hidgj', w, eye)
    return full.reshape(heads * blk, dirs * heads * blk)


def kernel(x_prompt, x_sample, cache_k, cache_v, state_lru, c, c_ctx, w_mod, b_mod, norm_mix_w, w_in, conf_conv_w, conf_conv_b, conf_ln_w, conf_ln_b, q_norm_w, k_norm_w, lru_conv_w, lru_conv_b, lru_wa, lru_ba, lru_wx, lru_bx, lru_lambda, w_out, norm_ffn_w, router_w, router_b, w_gu, b_gu, w_down, b_down, final_norm_w):
    batch, seq, d = x_prompt.shape
    dec_batch, dec_seq, _ = x_sample.shape
    depth = w_mod.shape[0]
    n_ctx, n_lat = batch * seq, dec_batch * dec_seq
    n = n_ctx + n_lat
    conf_w = conf_conv_w.shape[-1]
    lru_w = lru_conv_w.shape[-1]
    kv_w = N_KV_HEADS * HEAD_DIM
    attn_w = N_HEADS * HEAD_DIM
    widths = (2 * conf_w, attn_w, kv_w, kv_w, lru_w, lru_w)
    past = cache_k.shape[2]

    ctx_tiles = n_ctx // TOKEN_TILE
    lat_tiles_per_seq = dec_seq // TOKEN_TILE
    mod_row = lambda i: jnp.where(i < ctx_tiles, 0, 1 + (i - ctx_tiles) // lat_tiles_per_seq)

    n_cond = 8
    cvec = jnp.zeros((n_cond, d), F32).at[0].set(c_ctx).at[1:1 + dec_batch].set(c)
    mod = _modulation(cvec, w_mod, b_mod).reshape(depth, n_cond, 6, d)

    x = jnp.concatenate([x_prompt.reshape(n_ctx, d), x_sample.reshape(n_lat, d)], axis=0)
    rope = _rope_tables(dec_seq)
    cache_k4 = cache_k.reshape(dec_batch, depth, past, kv_w)
    cache_v4 = cache_v.reshape(dec_batch, depth, past, kv_w)
    h0_ctx = jnp.zeros((batch, 2, lru_w), F32)
    assert d == SUBLANES * LANES, "row tables are moved as one (8, 128) tile per token"
    n_sorted_tiles = n * TOP_K // EXPERT_TILE + N_EXPERTS

    new_k, new_v, new_h = [], [], []
    for l in range(depth):
        p = {
            'conf_conv_w': conf_conv_w[l], 'conf_conv_b': conf_conv_b[l].reshape(1, conf_w),
            'conf_ln_w': conf_ln_w[l].reshape(1, conf_w), 'conf_ln_b': conf_ln_b[l].reshape(1, conf_w),
            'lru_conv_w': lru_conv_w[l], 'lru_conv_b': lru_conv_b[l].reshape(1, lru_w),
            'lru_wg': jnp.concatenate([_block_diag(lru_wa[l]), _block_diag(lru_wx[l])], axis=-1).astype(BF16),
            'lru_bg': jnp.concatenate([lru_ba[l].reshape(-1), lru_bx[l].reshape(-1)]).reshape(1, 4 * lru_w),
            'lru_lam': lru_lambda[l].reshape(1, 2 * lru_w),
        }
        conf_u, q, k, v, lru_x, lru_g = _pre_mixer(x, mod[l], mod_row, norm_mix_w[l], w_in[l].astype(BF16), widths)

        conf_c, lru_c, h_last = _seq_mixers(seq, batch, 0, conf_u, lru_x, lru_g, h0_ctx, p)
        conf_l, lru_l, _ = _seq_mixers(dec_seq, dec_batch, n_ctx, conf_u, lru_x, lru_g, state_lru[:, l], p)

        qw = q_norm_w[l].reshape(1, HEAD_DIM)
        kw = k_norm_w[l].reshape(1, HEAD_DIM)
        attn_c, k_ctx = _attention_ctx(seq, batch, q, k, v, qw, kw)
        attn_l = _attention_lat(dec_seq, dec_batch, n_ctx, l, q, k, v, cache_k4, cache_v4,
                                jnp.tile(qw, (1, N_HEADS)), jnp.tile(kw, (1, N_KV_HEADS)), rope)
        new_k.append(k_ctx.reshape(batch, seq, N_KV_HEADS, HEAD_DIM))
        new_v.append(v[:n_ctx].reshape(batch, seq, N_KV_HEADS, HEAD_DIM))
        new_h.append(h_last)

        x1, h2, topi, gates, rank, counts = _post_mixer(
            x, (conf_c, attn_c, lru_c), (conf_l, attn_l, lru_l), mod[l], mod_row, w_out[l].astype(BF16),
            norm_ffn_w[l], router_w[l], router_b[l])
        pos, tile_end, tile_expert, n_used = _routing_tables(counts, topi, rank, n_sorted_tiles)
        xs = _dispatch(h2, pos, tile_end, n_sorted_tiles)
        ys = _experts(l, tile_expert, n_used, xs, w_gu, b_gu, w_down, b_down)
        x = _combine(l == depth - 1, x1, gates, mod[l], mod_row, final_norm_w, pos, ys)

    y_prompt = x[:n_ctx].reshape(batch, seq, d)
    y_sample = x[n_ctx:].reshape(dec_batch, dec_seq, d)
    return (y_prompt, y_sample, jnp.stack(new_k, axis=1), jnp.stack(new_v, axis=1), jnp.stack(new_h, axis=1))
```

```python
import functools

import jax
import jax.numpy as jnp
from jax import lax
from jax.experimental import pallas as pl
from jax.experimental.pallas import tpu as pltpu

F32 = jnp.float32
BF16 = jnp.bfloat16

HEAD_DIM = 64
N_HEADS = 8
N_KV_HEADS = 2
HEADS_PER_KV = N_HEADS // N_KV_HEADS
CONF_KERNEL = 31
LRU_CONV = 4
LRU_C = 8.0
N_EXPERTS = 32
TOP_K = 4
SWIGLU_ALPHA = 1.702
SWIGLU_LIMIT = 7.0
ROPE_THETA = 10000.0
GRID_W = 64
EPS = 1e-6

LANES = 128
SUBLANES = 8
ISSUE_UNROLL = 4
TABLE_SLOTS = 4
TOKEN_TILE = 256
EXPERT_TILE = 256
CONV_HALO = 16
ROW_CHUNK = 64
SCAN_CHUNK = 32
VMEM_LIMIT = 56 * 1024 * 1024


def _params(sem, vmem=None):
    return pltpu.CompilerParams(dimension_semantics=sem, vmem_limit_bytes=vmem)


def _split_bf16(x):
    hi = x.astype(BF16)
    lo = (x - hi.astype(F32)).astype(BF16)
    return hi, lo


def _dot(a, b):
    return jnp.dot(a, b, preferred_element_type=F32)


def _dot3(a, b):
    a_hi, a_lo = _split_bf16(a)
    b_hi, b_lo = _split_bf16(b)
    return _dot(a_hi, b_hi) + (_dot(a_hi, b_lo) + _dot(a_lo, b_hi))


def _mod_kernel(c_ref, w_ref, b_ref, o_ref):
    c = c_ref[...]
    s = c * jax.nn.sigmoid(c)
    o_ref[...] = _dot3(s, w_ref[...]) + b_ref[...]


def _modulation(cvec, w_mod, b_mod):
    depth, d, d6 = w_mod.shape
    tn = 768
    return pl.pallas_call(
        _mod_kernel,
        out_shape=jax.ShapeDtypeStruct((depth, cvec.shape[0], d6), F32),
        grid=(depth, d6 // tn),
        in_specs=[
            pl.BlockSpec(cvec.shape, lambda l, j: (0, 0)),
            pl.BlockSpec((None, d, tn), lambda l, j: (l, 0, j)),
            pl.BlockSpec((None, 1, tn), lambda l, j: (l, 0, j)),
        ],
        out_specs=pl.BlockSpec((None, cvec.shape[0], tn), lambda l, j: (l, 0, j)),
        compiler_params=_params(("arbitrary", "arbitrary")),
        name="modulation",
    )(cvec, w_mod, b_mod.reshape(depth, 1, d6))


def _pre_kernel(x_ref, mod_ref, nw_ref, win_ref, conf_ref, q_ref, k_ref, v_ref, lx_ref, lg_ref):
    x = x_ref[...]
    m = mod_ref[...]
    shift, scale = m[0:1], m[1:2]
    h = x * lax.rsqrt(jnp.mean(x * x, axis=-1, keepdims=True) + EPS) * nw_ref[...]
    h = h * (1.0 + scale) + shift
    proj = _dot(h.astype(BF16), win_ref[...])
    col = 0
    for ref in (conf_ref, q_ref, k_ref, v_ref, lx_ref, lg_ref):
        w = ref.shape[-1]
        ref[...] = proj[:, col:col + w]
        col += w


def _pre_mixer(x, mod_l, mod_row, norm_w, w_in_bf16, widths):
    n, d = x.shape
    return pl.pallas_call(
        _pre_kernel,
        out_shape=[jax.ShapeDtypeStruct((n, w), F32) for w in widths],
        grid=(n // TOKEN_TILE,),
        in_specs=[
            pl.BlockSpec((TOKEN_TILE, d), lambda i: (i, 0)),
            pl.BlockSpec((None, 6, d), lambda i: (mod_row(i), 0, 0)),
            pl.BlockSpec((1, d), lambda i: (0, 0)),
            pl.BlockSpec(w_in_bf16.shape, lambda i: (0, 0)),
        ],
        out_specs=[pl.BlockSpec((TOKEN_TILE, w), lambda i: (i, 0)) for w in widths],
        compiler_params=_params(("arbitrary",)),
        name="pre_mixer",
    )(x, mod_l, norm_w.reshape(1, d), w_in_bf16)


def _gelu_tanh(x):
    return 0.5 * x * (1.0 + jnp.tanh(0.7978845608028654 * (x + 0.044715 * (x * x * x))))


def _seq_kernel(seq_len, conf_ref, lx_ref, lg_ref, h0_ref, ccw_ref, ccb_ref, lnw_ref, lnb_ref,
                lcw_ref, lcb_ref, wg_ref, bg_ref, lam_ref,
                co_ref, lo_ref, hl_ref, pad_ref, af_ref, uf_ref, ab_ref, ub_ref):
    L = seq_len
    W = co_ref.shape[-1]
    pad = L // 2
    zeros_halo = jnp.zeros((CONV_HALO, W), F32)

    pad_ref[0:CONV_HALO, :] = zeros_halo
    pad_ref[CONV_HALO + L:2 * CONV_HALO + L, :] = zeros_halo
    for c in range(L // ROW_CHUNK):
        r0 = c * ROW_CHUNK
        u = conf_ref[r0:r0 + ROW_CHUNK, :]
        pad_ref[CONV_HALO + r0:CONV_HALO + r0 + ROW_CHUNK, :] = u[:, :W] * jax.nn.sigmoid(u[:, W:])
    left = CONF_KERNEL // 2
    for c in range(L // ROW_CHUNK):
        r0 = c * ROW_CHUNK
        acc = jnp.zeros((ROW_CHUNK, W), F32) + ccb_ref[...]
        for k in range(CONF_KERNEL):
            start = CONV_HALO - left + k + r0
            acc = acc + ccw_ref[k:k + 1, :] * pad_ref[start:start + ROW_CHUNK, :]
        mu = jnp.mean(acc, axis=-1, keepdims=True)
        cen = acc - mu
        var = jnp.mean(cen * cen, axis=-1, keepdims=True)
        y = cen * lax.rsqrt(var + EPS) * lnw_ref[...] + lnb_ref[...]
        co_ref[r0:r0 + ROW_CHUNK, :] = y * jax.nn.sigmoid(y)

    for c in range(L // ROW_CHUNK):
        r0 = c * ROW_CHUNK
        pad_ref[CONV_HALO + r0:CONV_HALO + r0 + ROW_CHUNK, :] = lx_ref[r0:r0 + ROW_CHUNK, :]
    ones_pad = jnp.ones((pad, W), F32)
    zeros_pad = jnp.zeros((pad, W), F32)
    af_ref[0:pad, :] = ones_pad
    uf_ref[0:pad, :] = zeros_pad
    ab_ref[L:L + pad, :] = ones_pad
    ub_ref[L:L + pad, :] = zeros_pad
    sp = jax.nn.softplus(-lam_ref[...])
    h0 = h0_ref[...]
    left = LRU_CONV // 2
    n_chunks = L // ROW_CHUNK
    for c in range(n_chunks):
        r0 = c * ROW_CHUNK
        xc = jnp.zeros((ROW_CHUNK, W), F32) + lcb_ref[...]
        for k in range(LRU_CONV):
            start = CONV_HALO - left + k + r0
            xc = xc + lcw_ref[k:k + 1, :] * pad_ref[start:start + ROW_CHUNK, :]
        g = _dot(xc.astype(BF16), wg_ref[...]) + bg_ref[...]
        r = jax.nn.sigmoid(g[:, :2 * W])
        i = jax.nn.sigmoid(g[:, 2 * W:])
        a = jnp.exp((-LRU_C) * r * sp)
        xc2 = jnp.concatenate([xc, xc], axis=-1)
        u = jnp.sqrt(1.0 - a * a) * i * xc2
        a_f, a_b, u_f, u_b = a[:, :W], a[:, W:], u[:, :W], u[:, W:]
        row = lax.broadcasted_iota(jnp.int32, (ROW_CHUNK, W), 0)
        if c == 0:
            u_f = jnp.where(row == 0, u_f + a_f * h0[0:1], u_f)
        if c == n_chunks - 1:
            u_b = jnp.where(row == ROW_CHUNK - 1, u_b + a_b * h0[1:2], u_b)
        af_ref[pad + r0:pad + r0 + ROW_CHUNK, :] = a_f
        uf_ref[pad + r0:pad + r0 + ROW_CHUNK, :] = u_f
        ab_ref[r0:r0 + ROW_CHUNK, :] = a_b
        ub_ref[r0:r0 + ROW_CHUNK, :] = u_b

    n_sc = L // SCAN_CHUNK
    s = 1
    while s < L:
        for c in reversed(range(n_sc)):
            r0 = c * SCAN_CHUNK
            if r0 + SCAN_CHUNK <= s:
                continue
            cur = slice(pad + r0, pad + r0 + SCAN_CHUNK)
            sh = slice(pad + r0 - s, pad + r0 - s + SCAN_CHUNK)
            a_cur = af_ref[cur, :]
            uf_ref[cur, :] = uf_ref[cur, :] + a_cur * uf_ref[sh, :]
            af_ref[cur, :] = a_cur * af_ref[sh, :]
        for c in range(n_sc):
            r0 = c * SCAN_CHUNK
            if r0 >= L - s:
                continue
            cur = slice(r0, r0 + SCAN_CHUNK)
            sh = slice(r0 + s, r0 + s + SCAN_CHUNK)
            a_cur = ab_ref[cur, :]
            ub_ref[cur, :] = ub_ref[cur, :] + a_cur * ub_ref[sh, :]
            ab_ref[cur, :] = a_cur * ab_ref[sh, :]
        s *= 2

    for c in range(n_chunks):
        r0 = c * ROW_CHUNK
        h = uf_ref[pad + r0:pad + r0 + ROW_CHUNK, :] + ub_ref[r0:r0 + ROW_CHUNK, :]
        lo_ref[r0:r0 + ROW_CHUNK, :] = h * _gelu_tanh(lg_ref[r0:r0 + ROW_CHUNK, :])
    hl_ref[0:1, :] = uf_ref[pad + L - 1:pad + L, :]
    hl_ref[1:2, :] = ub_ref[0:1, :]


def _seq_mixers(seq_len, n_seq, row0, conf_u, lru_x, lru_g, h0, p):
    w = lru_x.shape[-1]
    b0 = row0 // seq_len
    pad = seq_len // 2
    in_spec = lambda width: pl.BlockSpec((seq_len, width), lambda i: (b0 + i, 0))
    out_spec = pl.BlockSpec((seq_len, w), lambda i: (i, 0))
    state_spec = pl.BlockSpec((None, 2, w), lambda i: (i, 0, 0))
    full = lambda a: pl.BlockSpec(a.shape, lambda i: (0,) * a.ndim)
    weights = (p['conf_conv_w'], p['conf_conv_b'], p['conf_ln_w'], p['conf_ln_b'],
               p['lru_conv_w'], p['lru_conv_b'], p['lru_wg'], p['lru_bg'], p['lru_lam'])
    return pl.pallas_call(
        functools.partial(_seq_kernel, seq_len),
        out_shape=[jax.ShapeDtypeStruct((n_seq * seq_len, w), F32)] * 2
        + [jax.ShapeDtypeStruct((n_seq, 2, w), F32)],
        grid=(n_seq,),
        in_specs=[in_spec(2 * w), in_spec(w), in_spec(w), state_spec] + [full(a) for a in weights],
        out_specs=[out_spec, out_spec, state_spec],
        scratch_shapes=[pltpu.VMEM((seq_len + 2 * CONV_HALO, w), F32)]
        + [pltpu.VMEM((seq_len + pad, w), F32)] * 4,
        compiler_params=_params(("arbitrary",), VMEM_LIMIT),
        name=f"seq_mixers_{seq_len}",
    )(conf_u, lru_x, lru_g, h0, *weights)


def _head_rms(x):
    return lax.rsqrt(jnp.mean(x * x, axis=-1, keepdims=True) + EPS)


def _swap_halves(x):
    width = x.shape[-1]
    lane = lax.broadcasted_iota(jnp.int32, x.shape, x.ndim - 1)
    up = pltpu.roll(x, width - HEAD_DIM // 4, x.ndim - 1)
    down = pltpu.roll(x, HEAD_DIM // 4, x.ndim - 1)
    return jnp.where((lane % (HEAD_DIM // 2)) < HEAD_DIM // 4, up, down)


def _attend(q_heads, k_bf16, v_bf16):
    scale = HEAD_DIM ** -0.5
    s = lax.dot_general(q_heads.astype(BF16), k_bf16, (((1,), (1,)), ((), ())),
                        preferred_element_type=F32) * scale
    p = jnp.exp(s - jnp.max(s, axis=-1, keepdims=True))
    denom = jnp.sum(p, axis=-1, keepdims=True)
    return _dot(p.astype(BF16), v_bf16) / denom


def _attn_ctx_kernel(q_ref, k_ref, v_ref, qw_ref, kw_ref, o_ref, kn_ref):
    L = q_ref.shape[0]
    q, k, v = q_ref[...], k_ref[...], v_ref[...]
    qw, kw = qw_ref[...], kw_ref[...]
    k_out, o_out = [], []
    for g in range(N_KV_HEADS):
        kh = k[:, g * HEAD_DIM:(g + 1) * HEAD_DIM]
        kh = kh * _head_rms(kh) * kw
        k_out.append(kh)
        qs = []
        for j in range(HEADS_PER_KV):
            h = g * HEADS_PER_KV + j
            qh = q[:, h * HEAD_DIM:(h + 1) * HEAD_DIM]
            qs.append(qh * _head_rms(qh) * qw)
        o = _attend(jnp.concatenate(qs, axis=0), kh.astype(BF16),
                    v[:, g * HEAD_DIM:(g + 1) * HEAD_DIM].astype(BF16))
        o_out += [o[j * L:(j + 1) * L] for j in range(HEADS_PER_KV)]
    kn_ref[...] = jnp.concatenate(k_out, axis=-1)
    o_ref[...] = jnp.concatenate(o_out, axis=-1)


def _attention_ctx(seq_len, n_seq, q, k, v, qw, kw):
    kvw = k.shape[-1]
    row_spec = lambda width: pl.BlockSpec((seq_len, width), lambda i: (i, 0))
    full = lambda a: pl.BlockSpec(a.shape, lambda i: (0,) * a.ndim)
    return pl.pallas_call(
        _attn_ctx_kernel,
        out_shape=[jax.ShapeDtypeStruct((n_seq * seq_len, q.shape[-1]), F32),
                   jax.ShapeDtypeStruct((n_seq * seq_len, kvw), F32)],
        grid=(n_seq,),
        in_specs=[row_spec(q.shape[-1]), row_spec(kvw), row_spec(kvw), full(qw), full(kw)],
        out_specs=[row_spec(q.shape[-1]), row_spec(kvw)],
        compiler_params=_params(("arbitrary",), VMEM_LIMIT),
        name="attention_ctx",
    )(q, k, v, qw, kw)


def _attn_lat_kernel(q_ref, k_ref, v_ref, ck_ref, cv_ref, qw_ref, kw_ref, cq_ref, sq_ref, ck_t_ref,
                     sk_t_ref, o_ref, kall_ref, vall_ref):
    L = k_ref.shape[0]
    tq = q_ref.shape[0]

    @pl.when(pl.program_id(1) == 0)
    def _():
        k = k_ref[...]
        t = k * kw_ref[...]
        rot = t * ck_t_ref[...] + _swap_halves(t) * sk_t_ref[...]
        parts = []
        for g in range(N_KV_HEADS):
            sl = slice(g * HEAD_DIM, (g + 1) * HEAD_DIM)
            parts.append(rot[:, sl] * _head_rms(k[:, sl]))
        kall_ref[0:L, :] = jnp.concatenate(parts, axis=-1).astype(BF16)
        kall_ref[L:, :] = ck_ref[...].astype(BF16)
        vall_ref[0:L, :] = v_ref[...].astype(BF16)
        vall_ref[L:, :] = cv_ref[...].astype(BF16)

    q = q_ref[...]
    t = q * qw_ref[...]
    rot = t * cq_ref[...] + _swap_halves(t) * sq_ref[...]
    o_out = []
    for g in range(N_KV_HEADS):
        qs = []
        for j in range(HEADS_PER_KV):
            sl = slice((g * HEADS_PER_KV + j) * HEAD_DIM, (g * HEADS_PER_KV + j + 1) * HEAD_DIM)
            qs.append(rot[:, sl] * _head_rms(q[:, sl]))
        sl = slice(g * HEAD_DIM, (g + 1) * HEAD_DIM)
        o = _attend(jnp.concatenate(qs, axis=0), kall_ref[:, sl], vall_ref[:, sl])
        o_out += [o[j * tq:(j + 1) * tq] for j in range(HEADS_PER_KV)]
    o_ref[...] = jnp.concatenate(o_out, axis=-1)


def _attention_lat(seq_len, n_seq, row0, layer, q, k, v, cache_k, cache_v, qw8, kw2, rope):
    tq = TOKEN_TILE
    nq = seq_len // tq
    qwid, kvw = q.shape[-1], k.shape[-1]
    past = cache_k.shape[2]
    b0q = row0 // tq
    b0s = row0 // seq_len
    cq, sq, ck, sk = rope
    full = lambda a: pl.BlockSpec(a.shape, lambda b, j: (0,) * a.ndim)
    seq_spec = pl.BlockSpec((seq_len, kvw), lambda b, j: (b0s + b, 0))
    cache_spec = pl.BlockSpec((None, None, past, kvw), lambda b, j: (b, layer, 0, 0))
    q_spec = pl.BlockSpec((tq, qwid), lambda b, j: (b0q + b * nq + j, 0))
    rope_q_spec = pl.BlockSpec((tq, qwid), lambda b, j: (j, 0))
    return pl.pallas_call(
        _attn_lat_kernel,
        out_shape=jax.ShapeDtypeStruct((n_seq * seq_len, qwid), F32),
        grid=(n_seq, nq),
        in_specs=[q_spec, seq_spec, seq_spec, cache_spec, cache_spec, full(qw8), full(kw2),
                  rope_q_spec, rope_q_spec, full(ck), full(sk)],
        out_specs=pl.BlockSpec((tq, qwid), lambda b, j: (b * nq + j, 0)),
        scratch_shapes=[pltpu.VMEM((seq_len + past, kvw), BF16)] * 2,
        compiler_params=_params(("arbitrary", "arbitrary"), VMEM_LIMIT),
        name="attention_lat",
    )(q, k, v, cache_k, cache_v, qw8, kw2, cq, sq, ck, sk)


def _rope_tables(seq_len):
    t = jnp.arange(seq_len)
    row = (t // GRID_W).astype(F32)
    col = (t % GRID_W).astype(F32)
    half = HEAD_DIM // 2
    freqs = ROPE_THETA ** (-jnp.arange(0, half, 2, dtype=F32) / half)
    ang_r, ang_c = row[:, None] * freqs, col[:, None] * freqs
    cos = jnp.concatenate([jnp.cos(ang_r)] * 2 + [jnp.cos(ang_c)] * 2, axis=-1)
    sin = jnp.concatenate([-jnp.sin(ang_r), jnp.sin(ang_r), -jnp.sin(ang_c), jnp.sin(ang_c)], axis=-1)
    return (jnp.tile(cos, (1, N_HEADS)), jnp.tile(sin, (1, N_HEADS)),
            jnp.tile(cos, (1, N_KV_HEADS)), jnp.tile(sin, (1, N_KV_HEADS)))


def _post_kernel(ctx_tiles, x_ref, co_c_ref, ao_c_ref, lo_c_ref, co_l_ref, ao_l_ref, lo_l_ref, mod_ref,
                 wout_ref, nw_ref, rw_ref, rb_ref,
                 x1_ref, h2_ref, topi_ref, gates_ref, rank_ref, counts_ref, carry_ref):
    tm = x_ref.shape[0]
    cw, aw = co_c_ref.shape[-1], ao_c_ref.shape[-1]
    is_ctx = pl.program_id(0) < ctx_tiles

    @pl.when(pl.program_id(0) == 0)
    def _():
        carry_ref[...] = jnp.zeros_like(carry_ref)

    m = mod_ref[...]
    gate1, shift2, scale2 = m[2:3], m[3:4], m[4:5]
    pick = lambda c_ref, l_ref: jnp.where(is_ctx, c_ref[...], l_ref[...]).astype(BF16)
    mixed = (_dot(pick(co_c_ref, co_l_ref), wout_ref[0:cw, :])
             + _dot(pick(ao_c_ref, ao_l_ref), wout_ref[cw:cw + aw, :])
             + _dot(pick(lo_c_ref, lo_l_ref), wout_ref[cw + aw:, :]))
    x1 = x_ref[...] + gate1 * mixed
    x1_ref[...] = x1
    h2 = x1 * lax.rsqrt(jnp.mean(x1 * x1, axis=-1, keepdims=True) + EPS) * nw_ref[...]
    h2 = h2 * (1.0 + scale2) + shift2
    _store_slabs(h2_ref, h2)

    lane = lax.broadcasted_iota(jnp.int32, (tm, LANES), 1)
    lane_f = lane.astype(F32)
    logits = jnp.where(lane < N_EXPERTS, _dot3(h2, rw_ref[...]) + rb_ref[...], -jnp.inf)
    top_v, onehots = [], []
    topi = jnp.zeros((tm, LANES), F32)
    for k in range(TOP_K):
        mx = jnp.max(logits, axis=-1, keepdims=True)
        idx = jnp.min(jnp.where(logits == mx, lane_f, float(LANES)), axis=-1, keepdims=True)
        hit = lane_f == idx
        logits = jnp.where(hit, -jnp.inf, logits)
        top_v.append(mx)
        onehots.append(hit)
        topi = jnp.where(lane == k, idx, topi)
    topi_ref[...] = topi.astype(jnp.int32)
    exps = [jnp.exp(v - top_v[0]) for v in top_v]
    denom = exps[0] + exps[1] + exps[2] + exps[3]
    gates = jnp.zeros((tm, LANES), F32)
    for k in range(TOP_K):
        gates = jnp.where(lane == k, exps[k] / denom, gates)
    gates_ref[...] = gates

    chosen = jnp.zeros((tm, LANES), F32)
    for hit in onehots:
        chosen = jnp.where(hit, 1.0, chosen)
    r_i = lax.broadcasted_iota(jnp.int32, (tm, tm), 0)
    c_i = lax.broadcasted_iota(jnp.int32, (tm, tm), 1)
    lower = jnp.where(c_i < r_i, 1.0, 0.0).astype(BF16)
    before = _dot(lower, chosen.astype(BF16)) + carry_ref[...]
    rank = jnp.zeros((tm, LANES), jnp.int32)
    for k, hit in enumerate(onehots):
        rk = jnp.sum(jnp.where(hit, before, 0.0), axis=-1, keepdims=True).astype(jnp.int32)
        rank = jnp.where(lane == k, rk, rank)
    rank_ref[...] = rank
    carry = carry_ref[...] + jnp.sum(chosen, axis=0, keepdims=True)
    carry_ref[...] = carry
    counts_ref[...] = carry


def _post_mixer(x, ctx_outs, lat_outs, mod_l, mod_row, w_out_bf16, norm_w, router_w, router_b):
    n, d = x.shape
    tm = TOKEN_TILE
    ctx_tiles = ctx_outs[0].shape[0] // tm
    row_spec = lambda a: pl.BlockSpec((tm, a.shape[-1]), lambda i: (i, 0))
    ctx_spec = lambda a: pl.BlockSpec((tm, a.shape[-1]), lambda i: (jnp.minimum(i, ctx_tiles - 1), 0))
    lat_spec = lambda a: pl.BlockSpec((tm, a.shape[-1]), lambda i: (jnp.maximum(i - ctx_tiles, 0), 0))
    full = lambda a: pl.BlockSpec(a.shape, lambda i: (0,) * a.ndim)
    rw = jnp.pad(router_w, ((0, 0), (0, LANES - N_EXPERTS)))
    rb = jnp.pad(router_b, (0, LANES - N_EXPERTS)).reshape(1, LANES)
    nw = norm_w.reshape(1, d)
    lane_tile = pl.BlockSpec((tm, LANES), lambda i: (i, 0))
    return pl.pallas_call(
        functools.partial(_post_kernel, ctx_tiles),
        out_shape=[jax.ShapeDtypeStruct((n, d), F32), jax.ShapeDtypeStruct((n * SUBLANES, LANES), F32),
                   jax.ShapeDtypeStruct((n, LANES), jnp.int32), jax.ShapeDtypeStruct((n, LANES), F32),
                   jax.ShapeDtypeStruct((n, LANES), jnp.int32), jax.ShapeDtypeStruct((1, LANES), F32)],
        grid=(n // tm,),
        in_specs=[row_spec(x)] + [ctx_spec(a) for a in ctx_outs] + [lat_spec(a) for a in lat_outs]
        + [pl.BlockSpec((None, 6, d), lambda i: (mod_row(i), 0, 0)),
           full(w_out_bf16), full(nw), full(rw), full(rb)],
        out_specs=[row_spec(x), pl.BlockSpec((tm * SUBLANES, LANES), lambda i: (i, 0)),
                   lane_tile, lane_tile, lane_tile,
                   pl.BlockSpec((1, LANES), lambda i: (0, 0))],
        scratch_shapes=[pltpu.VMEM((1, LANES), F32)],
        compiler_params=_params(("arbitrary",), VMEM_LIMIT),
        name="post_mixer",
    )(x, *ctx_outs, *lat_outs, mod_l, w_out_bf16, nw, rw, rb)


def _slab_copy(src_ref, src_tok, dst_ref, dst_tok, sem):
    src = src_ref.at[pl.ds(pl.multiple_of(src_tok * SUBLANES, SUBLANES), SUBLANES)]
    dst = dst_ref.at[pl.ds(pl.multiple_of(dst_tok * SUBLANES, SUBLANES), SUBLANES)]
    return pltpu.make_async_copy(src, dst, sem)


def _load_slabs(ref, n_tok, lead=()):
    return jnp.concatenate(
        [ref[(*lead, pl.ds(j, n_tok, stride=SUBLANES), slice(None))] for j in range(SUBLANES)], axis=-1)


def _store_slabs(ref, val):
    for j in range(SUBLANES):
        ref[pl.ds(j, val.shape[0], stride=SUBLANES), :] = val[:, j * LANES:(j + 1) * LANES]


def _wait_slabs(hbm_ref, n_tok, sem):
    span = hbm_ref.at[pl.ds(0, n_tok * SUBLANES)]
    pltpu.make_async_copy(span, span, sem).wait()


def _fetch_positions(pos_ref, idx_ref, sem, n):
    copy = pltpu.make_async_copy(pos_ref.at[pl.ds(pl.program_id(0) * n, n)], idx_ref, sem)
    copy.start()
    copy.wait()


def _invert_kernel(pos_ref, init_ref, inv_ref, idx_ref, tab_ref, idx_sem, tab_sem):
    i = pl.program_id(0)
    n = idx_ref.shape[0]

    @pl.when(i == 0)
    def _():
        init = pltpu.make_async_copy(init_ref, tab_ref, tab_sem)
        init.start()
        init.wait()

    _fetch_positions(pos_ref, idx_ref, idx_sem, n)

    def place(p, carry):
        tab_ref[idx_ref[p]] = i * n + p
        return carry

    lax.fori_loop(0, n, place, 0, unroll=ISSUE_UNROLL * TOP_K)

    @pl.when(i == pl.num_programs(0) - 1)
    def _():
        out = pltpu.make_async_copy(tab_ref, inv_ref, tab_sem)
        out.start()
        out.wait()


def _invert(pos_flat, n_rows):
    chunk = TOKEN_TILE * TOP_K
    return pl.pallas_call(
        _invert_kernel,
        out_shape=jax.ShapeDtypeStruct((n_rows,), jnp.int32),
        grid=(pos_flat.shape[0] // chunk,),
        in_specs=[pl.BlockSpec(memory_space=pl.ANY), pl.BlockSpec(memory_space=pl.ANY)],
        out_specs=pl.BlockSpec(memory_space=pl.ANY),
        scratch_shapes=[pltpu.SMEM((chunk,), jnp.int32), pltpu.SMEM((n_rows,), jnp.int32),
                        pltpu.SemaphoreType.DMA, pltpu.SemaphoreType.DMA],
        compiler_params=_params(("arbitrary",)),
        name="moe_invert",
    )(pos_flat, jnp.full((n_rows,), -1, jnp.int32))


def _expert_kernel(dump0, te_ref, nu_ref, tab_ref, h2_ref, wgu_ref, bgu_ref, wd_ref, bd_ref, y4_ref,
                   wgu_bf, wd_bf, tab_smem, xbuf, ybuf, tab_sem, gat_sem, sca_sem):
    i = pl.program_id(0)
    tm = EXPERT_TILE
    dff = wd_ref.shape[0]

    def table_copy(t):
        slot = t & (TABLE_SLOTS - 1)
        return pltpu.make_async_copy(tab_ref.at[t], tab_smem.at[slot], tab_sem.at[slot])

    def gather(t, r):
        return _slab_copy(h2_ref, tab_smem[t & (TABLE_SLOTS - 1), r], xbuf.at[t & 1], r, gat_sem.at[t & 1])

    def scatter(t, r):
        return _slab_copy(ybuf.at[t & 1], r, y4_ref, tab_smem[t & (TABLE_SLOTS - 1), tm + r],
                          sca_sem.at[t & 1])

    @pl.when(i < nu_ref[0])
    def _():
        @pl.when(i == 0)
        def _():
            xbuf[...] = jnp.zeros_like(xbuf)
            ybuf[...] = jnp.zeros_like(ybuf)
            first = table_copy(0)
            first.start()
            first.wait()
            table_copy(1).start()
            lax.fori_loop(0, tm, lambda r, c: (gather(0, r).start(), c)[1], 0, unroll=ISSUE_UNROLL)
            for s in range(2):
                lax.fori_loop(
                    0, tm,
                    lambda r, c, s=s: (_slab_copy(ybuf.at[s], r, y4_ref, dump0 + s * tm + r,
                                                  sca_sem.at[s]).start(), c)[1],
                    0, unroll=ISSUE_UNROLL)

        prev = te_ref[jnp.maximum(i - 1, 0)]

        @pl.when((i == 0) | (te_ref[i] != prev))
        def _():
            wgu_bf[...] = wgu_ref[...].astype(BF16)
            wd_bf[...] = wd_ref[...].astype(BF16)

        table_copy(i + 1).wait()
        table_copy(i + 2).start()
        for r in range(tm):
            gather(i + 1, r).start()
        _wait_slabs(h2_ref, tm, gat_sem.at[i & 1])
        x = _load_slabs(xbuf, tm, (i & 1,))
        gu = _dot(x.astype(BF16), wgu_bf[...]) + bgu_ref[...]
        x_glu = jnp.minimum(gu[:, :dff], SWIGLU_LIMIT)
        x_lin = jnp.clip(gu[:, dff:], -SWIGLU_LIMIT, SWIGLU_LIMIT)
        act = x_glu * jax.nn.sigmoid(SWIGLU_ALPHA * x_glu) * (x_lin + 1.0)
        y = _dot(act.astype(BF16), wd_bf[...]) + bd_ref[...]
        _wait_slabs(y4_ref, tm, sca_sem.at[i & 1])
        _store_slabs(ybuf.at[i & 1], y)
        for r in range(tm):
            scatter(i, r).start(priority=r % 2)

        @pl.when(i == nu_ref[0] - 1)
        def _():
            table_copy(i + 2).wait()
            _wait_slabs(h2_ref, tm, gat_sem.at[(i + 1) & 1])
            _wait_slabs(y4_ref, tm, sca_sem.at[(i + 1) & 1])
            _wait_slabs(y4_ref, tm, sca_sem.at[i & 1])


def _experts(layer, n_pairs, tile_expert, n_used, tab, h2_slabs, w_gu, b_gu, w_down, b_down):
    tm = EXPERT_TILE
    d, dff2 = w_gu.shape[-2:]
    dff = w_down.shape[-2]
    n_slabs = n_pairs + 2 * tm
    hbm = pl.BlockSpec(memory_space=pl.ANY)
    grid_spec = pltpu.PrefetchScalarGridSpec(
        num_scalar_prefetch=2,
        grid=(tile_expert.shape[0],),
        in_specs=[
            hbm, hbm,
            pl.BlockSpec((None, None, d, dff2), lambda i, te, nu: (layer, te[i], 0, 0)),
            pl.BlockSpec((None, None, 1, dff2), lambda i, te, nu: (layer, te[i], 0, 0)),
            pl.BlockSpec((None, None, dff, d), lambda i, te, nu: (layer, te[i], 0, 0)),
            pl.BlockSpec((None, None, 1, d), lambda i, te, nu: (layer, te[i], 0, 0)),
        ],
        out_specs=hbm,
        scratch_shapes=[pltpu.VMEM((d, dff2), BF16), pltpu.VMEM((dff, d), BF16),
                        pltpu.SMEM((TABLE_SLOTS, 2 * tm), jnp.int32),
                        pltpu.VMEM((2, tm * SUBLANES, LANES), F32),
                        pltpu.VMEM((2, tm * SUBLANES, LANES), F32),
                        pltpu.SemaphoreType.DMA((TABLE_SLOTS,)), pltpu.SemaphoreType.DMA((2,)),
                        pltpu.SemaphoreType.DMA((2,))],
    )
    depth, n_e = w_gu.shape[:2]
    return pl.pallas_call(
        functools.partial(_expert_kernel, n_pairs),
        out_shape=jax.ShapeDtypeStruct((n_slabs * SUBLANES, LANES), F32),
        grid_spec=grid_spec,
        compiler_params=_params(("arbitrary",), VMEM_LIMIT),
        name="moe_experts",
    )(tile_expert, n_used, tab, h2_slabs, w_gu, b_gu.reshape(depth, n_e, 1, dff2), w_down,
      b_down.reshape(depth, n_e, 1, d))


def _combine_kernel(final, x1_ref, gates_ref, mod_ref, fw_ref, *refs):
    y_refs, out_ref = refs[:TOP_K], refs[TOP_K]
    tm = x1_ref.shape[0]
    gates = gates_ref[...]
    moe = gates[:, 0:1] * _load_slabs(y_refs[0], tm)
    for k in range(1, TOP_K):
        moe = moe + gates[:, k:k + 1] * _load_slabs(y_refs[k], tm)
    x2 = x1_ref[...] + mod_ref[5:6, :] * moe
    if final:
        x2 = x2 * lax.rsqrt(jnp.mean(x2 * x2, axis=-1, keepdims=True) + EPS) * fw_ref[...]
    out_ref[...] = x2


def _combine(final, x1, gates, mod_l, mod_row, final_w, y4):
    n, d = x1.shape
    tm = TOKEN_TILE
    tiles = n // tm
    y_specs = [pl.BlockSpec((tm * SUBLANES, LANES), lambda i, k=k: (k * tiles + i, 0)) for k in range(TOP_K)]
    return pl.pallas_call(
        functools.partial(_combine_kernel, final),
        out_shape=jax.ShapeDtypeStruct((n, d), F32),
        grid=(tiles,),
        in_specs=[pl.BlockSpec((tm, d), lambda i: (i, 0)),
                  pl.BlockSpec((tm, LANES), lambda i: (i, 0)),
                  pl.BlockSpec((None, 6, d), lambda i: (mod_row(i), 0, 0)),
                  pl.BlockSpec((1, d), lambda i: (0, 0))] + y_specs,
        out_specs=pl.BlockSpec((tm, d), lambda i: (i, 0)),
        compiler_params=_params(("arbitrary",), VMEM_LIMIT),
        name="moe_combine",
    )(x1, gates, mod_l, final_w.reshape(1, d), *([y4] * TOP_K))


def _routing_tables(counts, topi, rank, n_tiles):
    tm = EXPERT_TILE
    c = counts[0, :N_EXPERTS].astype(jnp.int32)
    tiles = (c + tm - 1) // tm
    tile_end = jnp.cumsum(tiles)
    offs = (tile_end - tiles) * tm
    pos = offs[topi[:, :TOP_K]] + rank[:, :TOP_K]
    n_used = tile_end[-1]
    t = jnp.minimum(jnp.arange(n_tiles, dtype=jnp.int32), n_used - 1)
    te = jnp.sum((tile_end[None, :] <= t[:, None]).astype(jnp.int32), axis=1)
    return pos.reshape(-1), jnp.minimum(te, N_EXPERTS - 1), n_used.reshape(1).astype(jnp.int32)


def _tile_tables(inv, n):
    tm = EXPERT_TILE
    r = jnp.arange(inv.shape[0], dtype=jnp.int32)
    valid = inv >= 0
    tok, choice = inv // TOP_K, inv % TOP_K
    src = jnp.where(valid, tok, 0)
    spare = TOP_K * n + ((r // tm) % 2) * tm + r % tm
    dst = jnp.where(valid, choice * n + tok, spare)
    return jnp.concatenate([src.reshape(-1, tm), dst.reshape(-1, tm)], axis=1)


def _block_diag(w):
    dirs, heads, blk, _ = w.shape
    eye = jnp.eye(heads, dtype=w.dtype)
    full = jnp.einsum('dhij,hg->hidgj', w, eye)
    return full.reshape(heads * blk, dirs * heads * blk)


def kernel(x_prompt, x_sample, cache_k, cache_v, state_lru, c, c_ctx, w_mod, b_mod, norm_mix_w, w_in, conf_conv_w, conf_conv_b, conf_ln_w, conf_ln_b, q_norm_w, k_norm_w, lru_conv_w, lru_conv_b, lru_wa, lru_ba, lru_wx, lru_bx, lru_lambda, w_out, norm_ffn_w, router_w, router_b, w_gu, b_gu, w_down, b_down, final_norm_w):
    batch, seq, d = x_prompt.shape
    dec_batch, dec_seq, _ = x_sample.shape
    depth = w_mod.shape[0]
    n_ctx, n_lat = batch * seq, dec_batch * dec_seq
    n = n_ctx + n_lat
    conf_w = conf_conv_w.shape[-1]
    lru_w = lru_conv_w.shape[-1]
    kv_w = N_KV_HEADS * HEAD_DIM
    attn_w = N_HEADS * HEAD_DIM
    widths = (2 * conf_w, attn_w, kv_w, kv_w, lru_w, lru_w)
    past = cache_k.shape[2]

    ctx_tiles = n_ctx // TOKEN_TILE
    lat_tiles_per_seq = dec_seq // TOKEN_TILE
    mod_row = lambda i: jnp.where(i < ctx_tiles, 0, 1 + (i - ctx_tiles) // lat_tiles_per_seq)

    n_cond = 8
    cvec = jnp.zeros((n_cond, d), F32).at[0].set(c_ctx).at[1:1 + dec_batch].set(c)
    mod = _modulation(cvec, w_mod, b_mod).reshape(depth, n_cond, 6, d)

    x = jnp.concatenate([x_prompt.reshape(n_ctx, d), x_sample.reshape(n_lat, d)], axis=0)
    rope = _rope_tables(dec_seq)
    cache_k4 = cache_k.reshape(dec_batch, depth, past, kv_w)
    cache_v4 = cache_v.reshape(dec_batch, depth, past, kv_w)
    h0_ctx = jnp.zeros((batch, 2, lru_w), F32)
    assert d == SUBLANES * LANES, "row tables are moved as one (8, 128) tile per token"
    n_sorted_tiles = n * TOP_K // EXPERT_TILE + N_EXPERTS
    n_table_tiles = -(-(n_sorted_tiles + 2) // 4) * 4

    new_k, new_v, new_h = [], [], []
    for l in range(depth):
        p = {
            'conf_conv_w': conf_conv_w[l], 'conf_conv_b': conf_conv_b[l].reshape(1, conf_w),
            'conf_ln_w': conf_ln_w[l].reshape(1, conf_w), 'conf_ln_b': conf_ln_b[l].reshape(1, conf_w),
            'lru_conv_w': lru_conv_w[l], 'lru_conv_b': lru_conv_b[l].reshape(1, lru_w),
            'lru_wg': jnp.concatenate([_block_diag(lru_wa[l]), _block_diag(lru_wx[l])], axis=-1).astype(BF16),
            'lru_bg': jnp.concatenate([lru_ba[l].reshape(-1), lru_bx[l].reshape(-1)]).reshape(1, 4 * lru_w),
            'lru_lam': lru_lambda[l].reshape(1, 2 * lru_w),
        }
        conf_u, q, k, v, lru_x, lru_g = _pre_mixer(x, mod[l], mod_row, norm_mix_w[l], w_in[l].astype(BF16), widths)

        conf_c, lru_c, h_last = _seq_mixers(seq, batch, 0, conf_u, lru_x, lru_g, h0_ctx, p)
        conf_l, lru_l, _ = _seq_mixers(dec_seq, dec_batch, n_ctx, conf_u, lru_x, lru_g, state_lru[:, l], p)

        qw = q_norm_w[l].reshape(1, HEAD_DIM)
        kw = k_norm_w[l].reshape(1, HEAD_DIM)
        attn_c, k_ctx = _attention_ctx(seq, batch, q, k, v, qw, kw)
        attn_l = _attention_lat(dec_seq, dec_batch, n_ctx, l, q, k, v, cache_k4, cache_v4,
                                jnp.tile(qw, (1, N_HEADS)), jnp.tile(kw, (1, N_KV_HEADS)), rope)
        new_k.append(k_ctx.reshape(batch, seq, N_KV_HEADS, HEAD_DIM))
        new_v.append(v[:n_ctx].reshape(batch, seq, N_KV_HEADS, HEAD_DIM))
        new_h.append(h_last)

        x1, h2, topi, gates, rank, counts = _post_mixer(
            x, (conf_c, attn_c, lru_c), (conf_l, attn_l, lru_l), mod[l], mod_row, w_out[l].astype(BF16),
            norm_ffn_w[l], router_w[l], router_b[l])
        pos, tile_expert, n_used = _routing_tables(counts, topi, rank, n_sorted_tiles)
        tab = _tile_tables(_invert(pos, n_table_tiles * EXPERT_TILE), n)
        y4 = _experts(l, n * TOP_K, tile_expert, n_used, tab, h2, w_gu, b_gu, w_down, b_down)
        x = _combine(l == depth - 1, x1, gates, mod[l], mod_row, final_norm_w, y4)

    y_prompt = x[:n_ctx].reshape(batch, seq, d)
    y_sample = x[n_ctx:].reshape(dec_batch, dec_seq, d)
    return (y_prompt, y_sample, jnp.stack(new_k, axis=1), jnp.stack(new_v, axis=1), jnp.stack(new_h, axis=1))
```

```python
import functools

import jax
import jax.numpy as jnp
from jax import lax
from jax.experimental import pallas as pl
from jax.experimental.pallas import tpu as pltpu

F32 = jnp.float32
BF16 = jnp.bfloat16

HEAD_DIM = 64
N_HEADS = 8
N_KV_HEADS = 2
HEADS_PER_KV = N_HEADS // N_KV_HEADS
CONF_KERNEL = 31
LRU_CONV = 4
LRU_C = 8.0
N_EXPERTS = 32
TOP_K = 4
SWIGLU_ALPHA = 1.702
SWIGLU_LIMIT = 7.0
ROPE_THETA = 10000.0
GRID_W = 64
EPS = 1e-6

LANES = 128
SUBLANES = 8
ISSUE_UNROLL = 4
TOKEN_TILE = 256
EXPERT_TILE = 256
CONV_HALO = 16
ROW_CHUNK = 64
SCAN_CHUNK = 32
VMEM_LIMIT = 56 * 1024 * 1024


def _params(sem, vmem=None):
    return pltpu.CompilerParams(dimension_semantics=sem, vmem_limit_bytes=vmem)


def _split_bf16(x):
    hi = x.astype(BF16)
    lo = (x - hi.astype(F32)).astype(BF16)
    return hi, lo


def _dot(a, b):
    return jnp.dot(a, b, preferred_element_type=F32)


def _dot3(a, b):
    a_hi, a_lo = _split_bf16(a)
    b_hi, b_lo = _split_bf16(b)
    return _dot(a_hi, b_hi) + (_dot(a_hi, b_lo) + _dot(a_lo, b_hi))


def _mod_kernel(c_ref, w_ref, b_ref, o_ref):
    c = c_ref[...]
    s = c * jax.nn.sigmoid(c)
    o_ref[...] = _dot3(s, w_ref[...]) + b_ref[...]


def _modulation(cvec, w_mod, b_mod):
    depth, d, d6 = w_mod.shape
    tn = 768
    return pl.pallas_call(
        _mod_kernel,
        out_shape=jax.ShapeDtypeStruct((depth, cvec.shape[0], d6), F32),
        grid=(depth, d6 // tn),
        in_specs=[
            pl.BlockSpec(cvec.shape, lambda l, j: (0, 0)),
            pl.BlockSpec((None, d, tn), lambda l, j: (l, 0, j)),
            pl.BlockSpec((None, 1, tn), lambda l, j: (l, 0, j)),
        ],
        out_specs=pl.BlockSpec((None, cvec.shape[0], tn), lambda l, j: (l, 0, j)),
        compiler_params=_params(("arbitrary", "arbitrary")),
        name="modulation",
    )(cvec, w_mod, b_mod.reshape(depth, 1, d6))


def _pre_kernel(ctx_tiles, xc_ref, xl_ref, mod_ref, nw_ref, win_ref, conf_ref, q_ref, k_ref, v_ref, lx_ref,
                lg_ref):
    x = jnp.where(pl.program_id(0) < ctx_tiles, xc_ref[...], xl_ref[...])
    m = mod_ref[...]
    shift, scale = m[0:1], m[1:2]
    h = x * lax.rsqrt(jnp.mean(x * x, axis=-1, keepdims=True) + EPS) * nw_ref[...]
    h = h * (1.0 + scale) + shift
    proj = _dot(h.astype(BF16), win_ref[...])
    col = 0
    for ref in (conf_ref, q_ref, k_ref, v_ref, lx_ref, lg_ref):
        w = ref.shape[-1]
        ref[...] = proj[:, col:col + w]
        col += w


def _path_specs(x_pair, tm):
    ctx_tiles = x_pair[0].shape[0] // tm
    return ctx_tiles, [
        pl.BlockSpec((tm, x_pair[0].shape[-1]), lambda i: (jnp.minimum(i, ctx_tiles - 1), 0)),
        pl.BlockSpec((tm, x_pair[1].shape[-1]), lambda i: (jnp.maximum(i - ctx_tiles, 0), 0))]


def _pre_mixer(x_pair, mod_l, mod_row, norm_w, w_in_bf16, widths):
    n, d = x_pair[0].shape[0] + x_pair[1].shape[0], x_pair[0].shape[1]
    ctx_tiles, x_specs = _path_specs(x_pair, TOKEN_TILE)
    return pl.pallas_call(
        functools.partial(_pre_kernel, ctx_tiles),
        out_shape=[jax.ShapeDtypeStruct((n, w), F32) for w in widths],
        grid=(n // TOKEN_TILE,),
        in_specs=x_specs + [
            pl.BlockSpec((None, 6, d), lambda i: (mod_row(i), 0, 0)),
            pl.BlockSpec((1, d), lambda i: (0, 0)),
            pl.BlockSpec(w_in_bf16.shape, lambda i: (0, 0)),
        ],
        out_specs=[pl.BlockSpec((TOKEN_TILE, w), lambda i: (i, 0)) for w in widths],
        compiler_params=_params(("arbitrary",)),
        name="pre_mixer",
    )(*x_pair, mod_l, norm_w.reshape(1, d), w_in_bf16)


def _gelu_tanh(x):
    return 0.5 * x * (1.0 + jnp.tanh(0.7978845608028654 * (x + 0.044715 * (x * x * x))))


def _seq_kernel(seq_len, conf_ref, lx_ref, lg_ref, h0_ref, ccw_ref, ccb_ref, lnw_ref, lnb_ref,
                lcw_ref, lcb_ref, wg_ref, bg_ref, lam_ref,
                co_ref, lo_ref, hl_ref, pad_ref, rot_ref, af_ref, uf_ref, ab_ref, ub_ref):
    L = seq_len
    W = co_ref.shape[-1]
    pad = L // 2
    zeros_halo = jnp.zeros((CONV_HALO, W), F32)

    pad_ref[0:CONV_HALO, :] = zeros_halo
    pad_ref[CONV_HALO + L:2 * CONV_HALO + L, :] = zeros_halo
    for c in range(L // ROW_CHUNK):
        r0 = c * ROW_CHUNK
        u = conf_ref[r0:r0 + ROW_CHUNK, :]
        pad_ref[CONV_HALO + r0:CONV_HALO + r0 + ROW_CHUNK, :] = u[:, :W] * jax.nn.sigmoid(u[:, W:])
    shifted_rows = L + 2 * CONV_HALO - SUBLANES
    for s in range(1, SUBLANES):
        for r0 in range(0, shifted_rows, ROW_CHUNK):
            rows = min(ROW_CHUNK, shifted_rows - r0)
            rot_ref[s - 1, r0:r0 + rows, :] = pad_ref[r0 + s:r0 + s + rows, :]
    left = CONF_KERNEL // 2
    for c in range(L // ROW_CHUNK):
        r0 = c * ROW_CHUNK
        acc = jnp.zeros((ROW_CHUNK, W), F32) + ccb_ref[...]
        for k in range(CONF_KERNEL):
            whole, s = divmod(CONV_HALO - left + k, SUBLANES)
            start = whole * SUBLANES + r0
            src = pad_ref if s == 0 else rot_ref.at[s - 1]
            acc = acc + ccw_ref[k:k + 1, :] * src[start:start + ROW_CHUNK, :]
        mu = jnp.mean(acc, axis=-1, keepdims=True)
        cen = acc - mu
        var = jnp.mean(cen * cen, axis=-1, keepdims=True)
        y = cen * lax.rsqrt(var + EPS) * lnw_ref[...] + lnb_ref[...]
        co_ref[r0:r0 + ROW_CHUNK, :] = y * jax.nn.sigmoid(y)

    for c in range(L // ROW_CHUNK):
        r0 = c * ROW_CHUNK
        pad_ref[CONV_HALO + r0:CONV_HALO + r0 + ROW_CHUNK, :] = lx_ref[r0:r0 + ROW_CHUNK, :]
    ones_pad = jnp.ones((pad, W), F32)
    zeros_pad = jnp.zeros((pad, W), F32)
    af_ref[0:pad, :] = ones_pad
    uf_ref[0:pad, :] = zeros_pad
    ab_ref[L:L + pad, :] = ones_pad
    ub_ref[L:L + pad, :] = zeros_pad
    sp = jax.nn.softplus(-lam_ref[...])
    h0 = h0_ref[...]
    left = LRU_CONV // 2
    n_chunks = L // ROW_CHUNK
    for c in range(n_chunks):
        r0 = c * ROW_CHUNK
        xc = jnp.zeros((ROW_CHUNK, W), F32) + lcb_ref[...]
        for k in range(LRU_CONV):
            start = CONV_HALO - left + k + r0
            xc = xc + lcw_ref[k:k + 1, :] * pad_ref[start:start + ROW_CHUNK, :]
        g = _dot(xc.astype(BF16), wg_ref[...]) + bg_ref[...]
        r = jax.nn.sigmoid(g[:, :2 * W])
        i = jax.nn.sigmoid(g[:, 2 * W:])
        a = jnp.exp((-LRU_C) * r * sp)
        xc2 = jnp.concatenate([xc, xc], axis=-1)
        u = jnp.sqrt(1.0 - a * a) * i * xc2
        a_f, a_b, u_f, u_b = a[:, :W], a[:, W:], u[:, :W], u[:, W:]
        row = lax.broadcasted_iota(jnp.int32, (ROW_CHUNK, W), 0)
        if c == 0:
            u_f = jnp.where(row == 0, u_f + a_f * h0[0:1], u_f)
        if c == n_chunks - 1:
            u_b = jnp.where(row == ROW_CHUNK - 1, u_b + a_b * h0[1:2], u_b)
        af_ref[pad + r0:pad + r0 + ROW_CHUNK, :] = a_f
        uf_ref[pad + r0:pad + r0 + ROW_CHUNK, :] = u_f
        ab_ref[r0:r0 + ROW_CHUNK, :] = a_b
        ub_ref[r0:r0 + ROW_CHUNK, :] = u_b

    n_sc = L // SCAN_CHUNK
    s = 1
    while s < L:
        for c in reversed(range(n_sc)):
            r0 = c * SCAN_CHUNK
            if r0 + SCAN_CHUNK <= s:
                continue
            cur = slice(pad + r0, pad + r0 + SCAN_CHUNK)
            sh = slice(pad + r0 - s, pad + r0 - s + SCAN_CHUNK)
            a_cur = af_ref[cur, :]
            uf_ref[cur, :] = uf_ref[cur, :] + a_cur * uf_ref[sh, :]
            af_ref[cur, :] = a_cur * af_ref[sh, :]
        for c in range(n_sc):
            r0 = c * SCAN_CHUNK
            if r0 >= L - s:
                continue
            cur = slice(r0, r0 + SCAN_CHUNK)
            sh = slice(r0 + s, r0 + s + SCAN_CHUNK)
            a_cur = ab_ref[cur, :]
            ub_ref[cur, :] = ub_ref[cur, :] + a_cur * ub_ref[sh, :]
            ab_ref[cur, :] = a_cur * ab_ref[sh, :]
        s *= 2

    for c in range(n_chunks):
        r0 = c * ROW_CHUNK
        h = uf_ref[pad + r0:pad + r0 + ROW_CHUNK, :] + ub_ref[r0:r0 + ROW_CHUNK, :]
        lo_ref[r0:r0 + ROW_CHUNK, :] = h * _gelu_tanh(lg_ref[r0:r0 + ROW_CHUNK, :])
    hl_ref[0:1, :] = uf_ref[pad + L - 1:pad + L, :]
    hl_ref[1:2, :] = ub_ref[0:1, :]


def _seq_mixers(seq_len, n_seq, row0, conf_u, lru_x, lru_g, h0, p):
    w = lru_x.shape[-1]
    b0 = row0 // seq_len
    pad = seq_len // 2
    in_spec = lambda width: pl.BlockSpec((seq_len, width), lambda i: (b0 + i, 0))
    out_spec = pl.BlockSpec((seq_len, w), lambda i: (i, 0))
    state_spec = pl.BlockSpec((None, 2, w), lambda i: (i, 0, 0))
    full = lambda a: pl.BlockSpec(a.shape, lambda i: (0,) * a.ndim)
    weights = (p['conf_conv_w'], p['conf_conv_b'], p['conf_ln_w'], p['conf_ln_b'],
               p['lru_conv_w'], p['lru_conv_b'], p['lru_wg'], p['lru_bg'], p['lru_lam'])
    padded = seq_len + 2 * CONV_HALO
    return pl.pallas_call(
        functools.partial(_seq_kernel, seq_len),
        out_shape=[jax.ShapeDtypeStruct((n_seq * seq_len, w), F32)] * 2
        + [jax.ShapeDtypeStruct((n_seq, 2, w), F32)],
        grid=(n_seq,),
        in_specs=[in_spec(2 * w), in_spec(w), in_spec(w), state_spec] + [full(a) for a in weights],
        out_specs=[out_spec, out_spec, state_spec],
        scratch_shapes=[pltpu.VMEM((padded, w), F32), pltpu.VMEM((SUBLANES - 1, padded, w), F32)]
        + [pltpu.VMEM((seq_len + pad, w), F32)] * 4,
        compiler_params=_params(("arbitrary",), VMEM_LIMIT),
        name=f"seq_mixers_{seq_len}",
    )(conf_u, lru_x, lru_g, h0, *weights)


def _head_rms(x):
    return lax.rsqrt(jnp.mean(x * x, axis=-1, keepdims=True) + EPS)


def _swap_halves(x):
    width = x.shape[-1]
    lane = lax.broadcasted_iota(jnp.int32, x.shape, x.ndim - 1)
    up = pltpu.roll(x, width - HEAD_DIM // 4, x.ndim - 1)
    down = pltpu.roll(x, HEAD_DIM // 4, x.ndim - 1)
    return jnp.where((lane % (HEAD_DIM // 2)) < HEAD_DIM // 4, up, down)


def _attend(q_heads, k_bf16, v_bf16):
    scale = HEAD_DIM ** -0.5
    s = lax.dot_general(q_heads.astype(BF16), k_bf16, (((1,), (1,)), ((), ())),
                        preferred_element_type=F32) * scale
    p = jnp.exp(s - jnp.max(s, axis=-1, keepdims=True))
    denom = jnp.sum(p, axis=-1, keepdims=True)
    return _dot(p.astype(BF16), v_bf16) / denom


def _attn_ctx_kernel(q_ref, k_ref, v_ref, qw_ref, kw_ref, o_ref, kn_ref):
    L = q_ref.shape[0]
    q, k, v = q_ref[...], k_ref[...], v_ref[...]
    qw, kw = qw_ref[...], kw_ref[...]
    k_out, o_out = [], []
    for g in range(N_KV_HEADS):
        kh = k[:, g * HEAD_DIM:(g + 1) * HEAD_DIM]
        kh = kh * _head_rms(kh) * kw
        k_out.append(kh)
        qs = []
        for j in range(HEADS_PER_KV):
            h = g * HEADS_PER_KV + j
            qh = q[:, h * HEAD_DIM:(h + 1) * HEAD_DIM]
            qs.append(qh * _head_rms(qh) * qw)
        o = _attend(jnp.concatenate(qs, axis=0), kh.astype(BF16),
                    v[:, g * HEAD_DIM:(g + 1) * HEAD_DIM].astype(BF16))
        o_out += [o[j * L:(j + 1) * L] for j in range(HEADS_PER_KV)]
    kn_ref[...] = jnp.concatenate(k_out, axis=-1)
    o_ref[...] = jnp.concatenate(o_out, axis=-1)


def _attention_ctx(seq_len, n_seq, q, k, v, qw, kw):
    kvw = k.shape[-1]
    row_spec = lambda width: pl.BlockSpec((seq_len, width), lambda i: (i, 0))
    full = lambda a: pl.BlockSpec(a.shape, lambda i: (0,) * a.ndim)
    return pl.pallas_call(
        _attn_ctx_kernel,
        out_shape=[jax.ShapeDtypeStruct((n_seq * seq_len, q.shape[-1]), F32),
                   jax.ShapeDtypeStruct((n_seq * seq_len, kvw), F32)],
        grid=(n_seq,),
        in_specs=[row_spec(q.shape[-1]), row_spec(kvw), row_spec(kvw), full(qw), full(kw)],
        out_specs=[row_spec(q.shape[-1]), row_spec(kvw)],
        compiler_params=_params(("arbitrary",), VMEM_LIMIT),
        name="attention_ctx",
    )(q, k, v, qw, kw)


def _attn_lat_kernel(q_ref, k_ref, v_ref, ck_ref, cv_ref, qw_ref, kw_ref, cq_ref, sq_ref, ck_t_ref,
                     sk_t_ref, o_ref, kall_ref, vall_ref):
    L = k_ref.shape[0]
    tq = q_ref.shape[0]

    @pl.when(pl.program_id(1) == 0)
    def _():
        k = k_ref[...]
        t = k * kw_ref[...]
        rot = t * ck_t_ref[...] + _swap_halves(t) * sk_t_ref[...]
        parts = []
        for g in range(N_KV_HEADS):
            sl = slice(g * HEAD_DIM, (g + 1) * HEAD_DIM)
            parts.append(rot[:, sl] * _head_rms(k[:, sl]))
        kall_ref[0:L, :] = jnp.concatenate(parts, axis=-1).astype(BF16)
        kall_ref[L:, :] = ck_ref[...].astype(BF16)
        vall_ref[0:L, :] = v_ref[...].astype(BF16)
        vall_ref[L:, :] = cv_ref[...].astype(BF16)

    q = q_ref[...]
    t = q * qw_ref[...]
    rot = t * cq_ref[...] + _swap_halves(t) * sq_ref[...]
    o_out = []
    for g in range(N_KV_HEADS):
        qs = []
        for j in range(HEADS_PER_KV):
            sl = slice((g * HEADS_PER_KV + j) * HEAD_DIM, (g * HEADS_PER_KV + j + 1) * HEAD_DIM)
            qs.append(rot[:, sl] * _head_rms(q[:, sl]))
        sl = slice(g * HEAD_DIM, (g + 1) * HEAD_DIM)
        o = _attend(jnp.concatenate(qs, axis=0), kall_ref[:, sl], vall_ref[:, sl])
        o_out += [o[j * tq:(j + 1) * tq] for j in range(HEADS_PER_KV)]
    o_ref[...] = jnp.concatenate(o_out, axis=-1)


def _attention_lat(seq_len, n_seq, row0, layer, q, k, v, cache_k, cache_v, qw8, kw2, rope):
    tq = TOKEN_TILE
    nq = seq_len // tq
    qwid, kvw = q.shape[-1], k.shape[-1]
    past = cache_k.shape[2]
    b0q = row0 // tq
    b0s = row0 // seq_len
    cq, sq, ck, sk = rope
    full = lambda a: pl.BlockSpec(a.shape, lambda b, j: (0,) * a.ndim)
    seq_spec = pl.BlockSpec((seq_len, kvw), lambda b, j: (b0s + b, 0))
    cache_spec = pl.BlockSpec((None, None, past, kvw), lambda b, j: (b, layer, 0, 0))
    q_spec = pl.BlockSpec((tq, qwid), lambda b, j: (b0q + b * nq + j, 0))
    rope_q_spec = pl.BlockSpec((tq, qwid), lambda b, j: (j, 0))
    return pl.pallas_call(
        _attn_lat_kernel,
        out_shape=jax.ShapeDtypeStruct((n_seq * seq_len, qwid), F32),
        grid=(n_seq, nq),
        in_specs=[q_spec, seq_spec, seq_spec, cache_spec, cache_spec, full(qw8), full(kw2),
                  rope_q_spec, rope_q_spec, full(ck), full(sk)],
        out_specs=pl.BlockSpec((tq, qwid), lambda b, j: (b * nq + j, 0)),
        scratch_shapes=[pltpu.VMEM((seq_len + past, kvw), BF16)] * 2,
        compiler_params=_params(("arbitrary", "arbitrary"), VMEM_LIMIT),
        name="attention_lat",
    )(q, k, v, cache_k, cache_v, qw8, kw2, cq, sq, ck, sk)


def _rope_tables(seq_len):
    t = jnp.arange(seq_len)
    row = (t // GRID_W).astype(F32)
    col = (t % GRID_W).astype(F32)
    half = HEAD_DIM // 2
    freqs = ROPE_THETA ** (-jnp.arange(0, half, 2, dtype=F32) / half)
    ang_r, ang_c = row[:, None] * freqs, col[:, None] * freqs
    cos = jnp.concatenate([jnp.cos(ang_r)] * 2 + [jnp.cos(ang_c)] * 2, axis=-1)
    sin = jnp.concatenate([-jnp.sin(ang_r), jnp.sin(ang_r), -jnp.sin(ang_c), jnp.sin(ang_c)], axis=-1)
    return (jnp.tile(cos, (1, N_HEADS)), jnp.tile(sin, (1, N_HEADS)),
            jnp.tile(cos, (1, N_KV_HEADS)), jnp.tile(sin, (1, N_KV_HEADS)))


def _slab_copy(src_ref, src_tok, dst_ref, dst_tok, sem):
    src = src_ref.at[pl.ds(pl.multiple_of(src_tok * SUBLANES, SUBLANES), SUBLANES)]
    dst = dst_ref.at[pl.ds(pl.multiple_of(dst_tok * SUBLANES, SUBLANES), SUBLANES)]
    return pltpu.make_async_copy(src, dst, sem)


def _load_slabs(ref, n_tok, lead=()):
    return jnp.concatenate(
        [ref[(*lead, pl.ds(j, n_tok, stride=SUBLANES), slice(None))] for j in range(SUBLANES)], axis=-1)


def _store_slabs(ref, val):
    for j in range(SUBLANES):
        ref[pl.ds(j, val.shape[0], stride=SUBLANES), :] = val[:, j * LANES:(j + 1) * LANES]


def _wait_slabs(hbm_ref, n_tok, sem):
    span = hbm_ref.at[pl.ds(0, n_tok * SUBLANES)]
    pltpu.make_async_copy(span, span, sem).wait()


def _fetch_positions(pos_ref, idx_ref, sem, n, tile0=0):
    copy = pltpu.make_async_copy(pos_ref.at[pl.ds((pl.program_id(0) + tile0) * n, n)], idx_ref, sem)
    copy.start()
    copy.wait()


def _post_kernel(ctx_tiles, xc_ref, xl_ref, co_c_ref, ao_c_ref, lo_c_ref, co_l_ref, ao_l_ref, lo_l_ref, mod_ref,
                 wout_ref, nw_ref, rw_ref, rb_ref,
                 x1_ref, h2_ref, topi_ref, gates_ref, rank_ref, counts_ref, carry_ref):
    tm = xc_ref.shape[0]
    cw, aw = co_c_ref.shape[-1], ao_c_ref.shape[-1]
    is_ctx = pl.program_id(0) < ctx_tiles

    @pl.when(pl.program_id(0) == 0)
    def _():
        carry_ref[...] = jnp.zeros_like(carry_ref)

    m = mod_ref[...]
    gate1, shift2, scale2 = m[2:3], m[3:4], m[4:5]
    pick = lambda c_ref, l_ref: jnp.where(is_ctx, c_ref[...], l_ref[...]).astype(BF16)
    x = jnp.where(is_ctx, xc_ref[...], xl_ref[...])
    mixed = (_dot(pick(co_c_ref, co_l_ref), wout_ref[0:cw, :])
             + _dot(pick(ao_c_ref, ao_l_ref), wout_ref[cw:cw + aw, :])
             + _dot(pick(lo_c_ref, lo_l_ref), wout_ref[cw + aw:, :]))
    x1 = x + gate1 * mixed
    x1_ref[...] = x1
    h2 = x1 * lax.rsqrt(jnp.mean(x1 * x1, axis=-1, keepdims=True) + EPS) * nw_ref[...]
    h2 = h2 * (1.0 + scale2) + shift2
    _store_slabs(h2_ref, h2)

    lane = lax.broadcasted_iota(jnp.int32, (tm, LANES), 1)
    lane_f = lane.astype(F32)
    logits = jnp.where(lane < N_EXPERTS, _dot3(h2, rw_ref[...]) + rb_ref[...], -jnp.inf)
    top_v, onehots = [], []
    topi = jnp.zeros((tm, LANES), F32)
    for k in range(TOP_K):
        mx = jnp.max(logits, axis=-1, keepdims=True)
        idx = jnp.min(jnp.where(logits == mx, lane_f, float(LANES)), axis=-1, keepdims=True)
        hit = lane_f == idx
        logits = jnp.where(hit, -jnp.inf, logits)
        top_v.append(mx)
        onehots.append(hit)
        topi = jnp.where(lane == k, idx, topi)
    topi_ref[...] = topi.astype(jnp.int32)
    exps = [jnp.exp(v - top_v[0]) for v in top_v]
    denom = exps[0] + exps[1] + exps[2] + exps[3]
    gates = jnp.zeros((tm, LANES), F32)
    for k in range(TOP_K):
        gates = jnp.where(lane == k, exps[k] / denom, gates)
    gates_ref[...] = gates

    chosen = jnp.zeros((tm, LANES), F32)
    for hit in onehots:
        chosen = jnp.where(hit, 1.0, chosen)
    r_i = lax.broadcasted_iota(jnp.int32, (tm, tm), 0)
    c_i = lax.broadcasted_iota(jnp.int32, (tm, tm), 1)
    lower = jnp.where(c_i < r_i, 1.0, 0.0).astype(BF16)
    before = _dot(lower, chosen.astype(BF16)) + carry_ref[...]
    rank = jnp.zeros((tm, LANES), jnp.int32)
    for k, hit in enumerate(onehots):
        rk = jnp.sum(jnp.where(hit, before, 0.0), axis=-1, keepdims=True).astype(jnp.int32)
        rank = jnp.where(lane == k, rk, rank)
    rank_ref[...] = rank
    carry = carry_ref[...] + jnp.sum(chosen, axis=0, keepdims=True)
    carry_ref[...] = carry
    counts_ref[...] = carry


def _post_mixer(x_pair, ctx_outs, lat_outs, mod_l, mod_row, w_out_bf16, norm_w, router_w, router_b):
    n, d = x_pair[0].shape[0] + x_pair[1].shape[0], x_pair[0].shape[1]
    tm = TOKEN_TILE
    ctx_tiles, x_specs = _path_specs(x_pair, tm)
    row_spec = lambda width: pl.BlockSpec((tm, width), lambda i: (i, 0))
    ctx_spec = lambda a: pl.BlockSpec((tm, a.shape[-1]), lambda i: (jnp.minimum(i, ctx_tiles - 1), 0))
    lat_spec = lambda a: pl.BlockSpec((tm, a.shape[-1]), lambda i: (jnp.maximum(i - ctx_tiles, 0), 0))
    full = lambda a: pl.BlockSpec(a.shape, lambda i: (0,) * a.ndim)
    rw = jnp.pad(router_w, ((0, 0), (0, LANES - N_EXPERTS)))
    rb = jnp.pad(router_b, (0, LANES - N_EXPERTS)).reshape(1, LANES)
    nw = norm_w.reshape(1, d)
    lane_tile = pl.BlockSpec((tm, LANES), lambda i: (i, 0))
    return pl.pallas_call(
        functools.partial(_post_kernel, ctx_tiles),
        out_shape=[jax.ShapeDtypeStruct((n, d), F32), jax.ShapeDtypeStruct((n * SUBLANES, LANES), F32),
                   jax.ShapeDtypeStruct((n, LANES), jnp.int32), jax.ShapeDtypeStruct((n, LANES), F32),
                   jax.ShapeDtypeStruct((n, LANES), jnp.int32), jax.ShapeDtypeStruct((1, LANES), F32)],
        grid=(n // tm,),
        in_specs=x_specs + [ctx_spec(a) for a in ctx_outs] + [lat_spec(a) for a in lat_outs]
        + [pl.BlockSpec((None, 6, d), lambda i: (mod_row(i), 0, 0)),
           full(w_out_bf16), full(nw), full(rw), full(rb)],
        out_specs=[row_spec(d), pl.BlockSpec((tm * SUBLANES, LANES), lambda i: (i, 0)),
                   lane_tile, lane_tile, lane_tile,
                   pl.BlockSpec((1, LANES), lambda i: (0, 0))],
        scratch_shapes=[pltpu.VMEM((1, LANES), F32)],
        compiler_params=_params(("arbitrary",), VMEM_LIMIT),
        name="post_mixer",
    )(*x_pair, *ctx_outs, *lat_outs, mod_l, w_out_bf16, nw, rw, rb)


def _pos_kernel(topi_ref, rank_ref, offs_ref, pos_ref):
    lane = lax.broadcasted_iota(jnp.int32, topi_ref.shape, 1)
    lane_f = lane.astype(F32)
    topi = topi_ref[...].astype(F32)
    offs = offs_ref[...]
    pos = rank_ref[...]
    for k in range(TOP_K):
        off_k = jnp.sum(jnp.where(lane_f == topi[:, k:k + 1], offs, 0.0), axis=-1, keepdims=True)
        pos = pos + jnp.where(lane == k, off_k.astype(jnp.int32), 0)
    pos_ref[...] = pos


def _positions(topi, rank, offs):
    n = topi.shape[0]
    tm = TOKEN_TILE
    lane_tile = pl.BlockSpec((tm, LANES), lambda i: (i, 0))
    offs_row = jnp.pad(offs.astype(F32), (0, LANES - N_EXPERTS)).reshape(1, LANES)
    pos = pl.pallas_call(
        _pos_kernel,
        out_shape=jax.ShapeDtypeStruct((n, LANES), jnp.int32),
        grid=(n // tm,),
        in_specs=[lane_tile, lane_tile, pl.BlockSpec((1, LANES), lambda i: (0, 0))],
        out_specs=lane_tile,
        compiler_params=_params(("arbitrary",)),
        name="moe_positions",
    )(topi, rank, offs_row)
    return pos[:, :TOP_K].reshape(-1)


def _dispatch_kernel(tend_ref, h2_ref, pos_ref, xs_ref, idx_ref, zero_ref, idx_sem, row_sem):
    tm = h2_ref.shape[0] // SUBLANES
    tile_rows = EXPERT_TILE * SUBLANES

    @pl.when(pl.program_id(0) == 0)
    def _():
        zero_ref[...] = jnp.zeros_like(zero_ref)

        def last_tile_copy(e):
            start = pl.multiple_of((tend_ref[e] - 1) * tile_rows, tile_rows)
            return pltpu.make_async_copy(zero_ref, xs_ref.at[pl.ds(start, tile_rows)], row_sem)

        def has_tiles(e):
            return tend_ref[e] > (tend_ref[e - 1] if e else 0)

        for e in range(N_EXPERTS):
            pl.when(has_tiles(e))(lambda e=e: last_tile_copy(e).start())
        for e in range(N_EXPERTS):
            pl.when(has_tiles(e))(lambda e=e: last_tile_copy(e).wait())

        def spare_tile_copy(t):
            return pltpu.make_async_copy(
                zero_ref, xs_ref.at[pl.ds(pl.multiple_of(t * tile_rows, tile_rows), tile_rows)], row_sem)

        n_used, n_tiles = tend_ref[N_EXPERTS - 1], xs_ref.shape[0] // tile_rows
        lax.fori_loop(n_used, n_tiles, lambda t, c: (spare_tile_copy(t).start(), c)[1], 0)
        lax.fori_loop(n_used, n_tiles, lambda t, c: (spare_tile_copy(t).wait(), c)[1], 0)

    _fetch_positions(pos_ref, idx_ref, idx_sem, tm * TOP_K)

    def issue(r, carry):
        for k in range(TOP_K):
            _slab_copy(h2_ref, r, xs_ref, idx_ref[r * TOP_K + k], row_sem).start(priority=k % 2)
        return carry

    lax.fori_loop(0, tm, issue, 0, unroll=ISSUE_UNROLL)
    _wait_slabs(xs_ref, tm * TOP_K, row_sem)


def _dispatch(h2_slabs, pos_flat, tile_end, n_tiles):
    tm = TOKEN_TILE
    n = h2_slabs.shape[0] // SUBLANES
    grid_spec = pltpu.PrefetchScalarGridSpec(
        num_scalar_prefetch=1,
        grid=(n // tm,),
        in_specs=[pl.BlockSpec((tm * SUBLANES, LANES), lambda i, tend: (i, 0)),
                  pl.BlockSpec(memory_space=pl.ANY)],
        out_specs=pl.BlockSpec(memory_space=pl.ANY),
        scratch_shapes=[pltpu.SMEM((tm * TOP_K,), jnp.int32),
                        pltpu.VMEM((EXPERT_TILE * SUBLANES, LANES), F32),
                        pltpu.SemaphoreType.DMA, pltpu.SemaphoreType.DMA],
    )
    return pl.pallas_call(
        _dispatch_kernel,
        out_shape=jax.ShapeDtypeStruct((n_tiles * EXPERT_TILE * SUBLANES, LANES), F32),
        grid_spec=grid_spec,
        compiler_params=_params(("arbitrary",)),
        name="moe_dispatch",
    )(tile_end, h2_slabs, pos_flat)


def _expert_kernel(layer, te_ref, nxt_ref, par_ref, nu_ref, xs_ref, wgu_hbm, bgu_ref, wd_hbm, bd_ref,
                   ys_ref, wgu_buf, wd_buf, wgu_bf, wd_bf, w_sem):
    i = pl.program_id(0)
    dff = wd_hbm.shape[-2]

    def weight_copies(e, slot):
        return (pltpu.make_async_copy(wgu_hbm.at[layer, e], wgu_buf.at[slot], w_sem.at[slot]),
                pltpu.make_async_copy(wd_hbm.at[layer, e], wd_buf.at[slot], w_sem.at[2 + slot]))

    @pl.when(i < nu_ref[0])
    def _():
        prev = te_ref[jnp.maximum(i - 1, 0)]

        @pl.when((i == 0) | (te_ref[i] != prev))
        def _():
            slot = par_ref[i]

            @pl.when(i == 0)
            def _():
                for copy in weight_copies(te_ref[0], slot):
                    copy.start()

            for copy in weight_copies(te_ref[i], slot):
                copy.wait()

            @pl.when(nxt_ref[i] >= 0)
            def _():
                for copy in weight_copies(nxt_ref[i], 1 - slot):
                    copy.start()

            wgu_bf[...] = wgu_buf[slot].astype(BF16)
            wd_bf[...] = wd_buf[slot].astype(BF16)

        x = _load_slabs(xs_ref, EXPERT_TILE)
        gu = _dot(x.astype(BF16), wgu_bf[...]) + bgu_ref[...]
        x_glu = jnp.minimum(gu[:, :dff], SWIGLU_LIMIT)
        x_lin = jnp.clip(gu[:, dff:], -SWIGLU_LIMIT, SWIGLU_LIMIT)
        act = x_glu * jax.nn.sigmoid(SWIGLU_ALPHA * x_glu) * (x_lin + 1.0)
        _store_slabs(ys_ref, _dot(act.astype(BF16), wd_bf[...]) + bd_ref[...])

    @pl.when(i >= nu_ref[0])
    def _():
        ys_ref[...] = jnp.zeros_like(ys_ref)


def _experts(layer, tile_expert, next_expert, parity, n_used, xs, w_gu, b_gu, w_down, b_down):
    tm = EXPERT_TILE
    d, dff2 = w_gu.shape[-2:]
    dff = w_down.shape[-2]
    slab_tile = (tm * SUBLANES, LANES)
    hbm = pl.BlockSpec(memory_space=pl.ANY)
    grid_spec = pltpu.PrefetchScalarGridSpec(
        num_scalar_prefetch=4,
        grid=(xs.shape[0] // slab_tile[0],),
        in_specs=[
            pl.BlockSpec(slab_tile, lambda i, te, nx, pa, nu: (jnp.minimum(i, nu[0] - 1), 0)),
            hbm,
            pl.BlockSpec((None, None, 1, dff2), lambda i, te, nx, pa, nu: (layer, te[i], 0, 0)),
            hbm,
            pl.BlockSpec((None, None, 1, d), lambda i, te, nx, pa, nu: (layer, te[i], 0, 0)),
        ],
        out_specs=pl.BlockSpec(slab_tile, lambda i, te, nx, pa, nu: (i, 0)),
        scratch_shapes=[pltpu.VMEM((2, d, dff2), F32), pltpu.VMEM((2, dff, d), F32),
                        pltpu.VMEM((d, dff2), BF16), pltpu.VMEM((dff, d), BF16),
                        pltpu.SemaphoreType.DMA((4,))],
    )
    depth, n_e = w_gu.shape[:2]
    return pl.pallas_call(
        functools.partial(_expert_kernel, layer),
        out_shape=jax.ShapeDtypeStruct(xs.shape, F32),
        grid_spec=grid_spec,
        compiler_params=_params(("arbitrary",), VMEM_LIMIT),
        name="moe_experts",
    )(tile_expert, next_expert, parity, n_used, xs, w_gu, b_gu.reshape(depth, n_e, 1, dff2), w_down,
      b_down.reshape(depth, n_e, 1, d))


def _combine_kernel(final, tile0, x1_ref, gates_ref, mod_ref, fw_ref, pos_ref, ys_ref, out_ref,
                    idx_ref, rows_ref, idx_sem, row_sem):
    tm = x1_ref.shape[0]
    _fetch_positions(pos_ref, idx_ref, idx_sem, tm * TOP_K, tile0)

    def issue(r, carry):
        for k in range(TOP_K):
            _slab_copy(ys_ref, idx_ref[r * TOP_K + k], rows_ref.at[k], r, row_sem).start(priority=k % 2)
        return carry

    lax.fori_loop(0, tm, issue, 0, unroll=ISSUE_UNROLL)
    _wait_slabs(ys_ref, tm * TOP_K, row_sem)

    gates = gates_ref[...]
    moe = gates[:, 0:1] * _load_slabs(rows_ref, tm, (0,))
    for k in range(1, TOP_K):
        moe = moe + gates[:, k:k + 1] * _load_slabs(rows_ref, tm, (k,))
    x2 = x1_ref[...] + mod_ref[5:6, :] * moe
    if final:
        x2 = x2 * lax.rsqrt(jnp.mean(x2 * x2, axis=-1, keepdims=True) + EPS) * fw_ref[...]
    out_ref[...] = x2


def _combine(final, row0, n_rows, x1, gates, mod_l, mod_row, final_w, pos_flat, ys):
    d = x1.shape[1]
    tm = TOKEN_TILE
    tile0 = row0 // tm
    return pl.pallas_call(
        functools.partial(_combine_kernel, final, tile0),
        out_shape=jax.ShapeDtypeStruct((n_rows, d), F32),
        grid=(n_rows // tm,),
        in_specs=[pl.BlockSpec((tm, d), lambda i: (tile0 + i, 0)),
                  pl.BlockSpec((tm, LANES), lambda i: (tile0 + i, 0)),
                  pl.BlockSpec((None, 6, d), lambda i: (mod_row(tile0 + i), 0, 0)),
                  pl.BlockSpec((1, d), lambda i: (0, 0)),
                  pl.BlockSpec(memory_space=pl.ANY), pl.BlockSpec(memory_space=pl.ANY)],
        out_specs=pl.BlockSpec((tm, d), lambda i: (i, 0)),
        scratch_shapes=[pltpu.SMEM((tm * TOP_K,), jnp.int32),
                        pltpu.VMEM((TOP_K, tm * SUBLANES, LANES), F32),
                        pltpu.SemaphoreType.DMA, pltpu.SemaphoreType.DMA],
        compiler_params=_params(("arbitrary",), VMEM_LIMIT),
        name="moe_combine",
    )(x1, gates, mod_l, final_w.reshape(1, d), pos_flat, ys)


def _routing_tables(counts, n_tiles):
    tm = EXPERT_TILE
    c = counts[0, :N_EXPERTS].astype(jnp.int32)
    tiles = (c + tm - 1) // tm
    tile_end = jnp.cumsum(tiles)
    offs = (tile_end - tiles) * tm
    n_used = tile_end[-1]
    t = jnp.minimum(jnp.arange(n_tiles, dtype=jnp.int32), n_used - 1)
    te = jnp.minimum(jnp.sum((tile_end[None, :] <= t[:, None]).astype(jnp.int32), axis=1), N_EXPERTS - 1)
    e = jnp.arange(N_EXPERTS, dtype=jnp.int32)
    later = (e[None, :] > e[:, None]) & (tiles[None, :] > 0)
    nxt = jnp.min(jnp.where(later, e[None, :], N_EXPERTS), axis=1)
    nxt = jnp.where(nxt == N_EXPERTS, -1, nxt)
    parity = (jnp.cumsum((tiles > 0).astype(jnp.int32)) - 1) % 2
    return (offs, tile_end.astype(jnp.int32), te, nxt[te].astype(jnp.int32),
            parity[te].astype(jnp.int32), n_used.reshape(1).astype(jnp.int32))


def _block_diag(w):
    dirs, heads, blk, _ = w.shape
    eye = jnp.eye(heads, dtype=w.dtype)
    full = jnp.einsum('dhij,hg->hidgj', w, eye)
    return full.reshape(heads * blk, dirs * heads * blk)


def kernel(x_prompt, x_sample, cache_k, cache_v, state_lru, c, c_ctx, w_mod, b_mod, norm_mix_w, w_in, conf_conv_w, conf_conv_b, conf_ln_w, conf_ln_b, q_norm_w, k_norm_w, lru_conv_w, lru_conv_b, lru_wa, lru_ba, lru_wx, lru_bx, lru_lambda, w_out, norm_ffn_w, router_w, router_b, w_gu, b_gu, w_down, b_down, final_norm_w):
    batch, seq, d = x_prompt.shape
    dec_batch, dec_seq, _ = x_sample.shape
    depth = w_mod.shape[0]
    n_ctx, n_lat = batch * seq, dec_batch * dec_seq
    n = n_ctx + n_lat
    conf_w = conf_conv_w.shape[-1]
    lru_w = lru_conv_w.shape[-1]
    kv_w = N_KV_HEADS * HEAD_DIM
    attn_w = N_HEADS * HEAD_DIM
    widths = (2 * conf_w, attn_w, kv_w, kv_w, lru_w, lru_w)
    past = cache_k.shape[2]

    ctx_tiles = n_ctx // TOKEN_TILE
    lat_tiles_per_seq = dec_seq // TOKEN_TILE
    mod_row = lambda i: jnp.where(i < ctx_tiles, 0, 1 + (i - ctx_tiles) // lat_tiles_per_seq)

    n_cond = 8
    cvec = jnp.zeros((n_cond, d), F32).at[0].set(c_ctx).at[1:1 + dec_batch].set(c)
    mod = _modulation(cvec, w_mod, b_mod).reshape(depth, n_cond, 6, d)

    x = (x_prompt.reshape(n_ctx, d), x_sample.reshape(n_lat, d))
    rope = _rope_tables(dec_seq)
    cache_k4 = cache_k.reshape(dec_batch, depth, past, kv_w)
    cache_v4 = cache_v.reshape(dec_batch, depth, past, kv_w)
    h0_ctx = jnp.zeros((batch, 2, lru_w), F32)
    assert d == SUBLANES * LANES, "row tables are moved as one (8, 128) tile per token"
    n_sorted_tiles = n * TOP_K // EXPERT_TILE + N_EXPERTS

    new_k, new_v, new_h = [], [], []
    for l in range(depth):
        p = {
            'conf_conv_w': conf_conv_w[l], 'conf_conv_b': conf_conv_b[l].reshape(1, conf_w),
            'conf_ln_w': conf_ln_w[l].reshape(1, conf_w), 'conf_ln_b': conf_ln_b[l].reshape(1, conf_w),
            'lru_conv_w': lru_conv_w[l], 'lru_conv_b': lru_conv_b[l].reshape(1, lru_w),
            'lru_wg': jnp.concatenate([_block_diag(lru_wa[l]), _block_diag(lru_wx[l])], axis=-1).astype(BF16),
            'lru_bg': jnp.concatenate([lru_ba[l].reshape(-1), lru_bx[l].reshape(-1)]).reshape(1, 4 * lru_w),
            'lru_lam': lru_lambda[l].reshape(1, 2 * lru_w),
        }
        conf_u, q, k, v, lru_x, lru_g = _pre_mixer(x, mod[l], mod_row, norm_mix_w[l], w_in[l].astype(BF16), widths)

        conf_c, lru_c, h_last = _seq_mixers(seq, batch, 0, conf_u, lru_x, lru_g, h0_ctx, p)
        conf_l, lru_l, _ = _seq_mixers(dec_seq, dec_batch, n_ctx, conf_u, lru_x, lru_g, state_lru[:, l], p)

        qw = q_norm_w[l].reshape(1, HEAD_DIM)
        kw = k_norm_w[l].reshape(1, HEAD_DIM)
        attn_c, k_ctx = _attention_ctx(seq, batch, q, k, v, qw, kw)
        attn_l = _attention_lat(dec_seq, dec_batch, n_ctx, l, q, k, v, cache_k4, cache_v4,
                                jnp.tile(qw, (1, N_HEADS)), jnp.tile(kw, (1, N_KV_HEADS)), rope)
        new_k.append(k_ctx.reshape(batch, seq, N_KV_HEADS, HEAD_DIM))
        new_v.append(v[:n_ctx].reshape(batch, seq, N_KV_HEADS, HEAD_DIM))
        new_h.append(h_last)

        x1, h2, topi, gates, rank, counts = _post_mixer(
            x, (conf_c, attn_c, lru_c), (conf_l, attn_l, lru_l), mod[l], mod_row, w_out[l].astype(BF16),
            norm_ffn_w[l], router_w[l], router_b[l])
        offs, tile_end, tile_expert, next_expert, parity, n_used = _routing_tables(counts, n_sorted_tiles)
        pos = _positions(topi, rank, offs)
        xs = _dispatch(h2, pos, tile_end, n_sorted_tiles)
        ys = _experts(l, tile_expert, next_expert, parity, n_used, xs, w_gu, b_gu, w_down, b_down)
        x = tuple(_combine(l == depth - 1, row0, rows, x1, gates, mod[l], mod_row, final_norm_w, pos, ys)
                  for row0, rows in ((0, n_ctx), (n_ctx, n_lat)))

    y_prompt = x[0].reshape(batch, seq, d)
    y_sample = x[1].reshape(dec_batch, dec_seq, d)
    return (y_prompt, y_sample, jnp.stack(new_k, axis=1), jnp.stack(new_v, axis=1), jnp.stack(new_h, axis=1))
```

```python
import functools

import jax
import jax.numpy as jnp
from jax import lax
from jax.experimental import pallas as pl
from jax.experimental.pallas import tpu as pltpu

F32 = jnp.float32
BF16 = jnp.bfloat16

HEAD_DIM = 64
N_HEADS = 8
N_KV_HEADS = 2
HEADS_PER_KV = N_HEADS // N_KV_HEADS
CONF_KERNEL = 31
LRU_CONV = 4
LRU_C = 8.0
N_EXPERTS = 32
TOP_K = 4
SWIGLU_ALPHA = 1.702
SWIGLU_LIMIT = 7.0
ROPE_THETA = 10000.0
GRID_W = 64
EPS = 1e-6

LANES = 128
SUBLANES = 8
ISSUE_UNROLL = 4
TOKEN_TILE = 256
EXPERT_TILE = 256
CONV_HALO = 16
ROW_CHUNK = 64
SCAN_CHUNK = 32
VMEM_LIMIT = 56 * 1024 * 1024


def _params(sem, vmem=None):
    return pltpu.CompilerParams(dimension_semantics=sem, vmem_limit_bytes=vmem)


def _split_bf16(x):
    hi = x.astype(BF16)
    lo = (x - hi.astype(F32)).astype(BF16)
    return hi, lo


def _dot(a, b):
    return jnp.dot(a, b, preferred_element_type=F32)


def _dot3(a, b):
    a_hi, a_lo = _split_bf16(a)
    b_hi, b_lo = _split_bf16(b)
    return _dot(a_hi, b_hi) + (_dot(a_hi, b_lo) + _dot(a_lo, b_hi))


def _mod_kernel(c_ref, w_ref, b_ref, o_ref):
    c = c_ref[...]
    s = c * jax.nn.sigmoid(c)
    o_ref[...] = _dot3(s, w_ref[...]) + b_ref[...]


def _modulation(cvec, w_mod, b_mod):
    depth, d, d6 = w_mod.shape
    tn = 768
    return pl.pallas_call(
        _mod_kernel,
        out_shape=jax.ShapeDtypeStruct((depth, cvec.shape[0], d6), F32),
        grid=(depth, d6 // tn),
        in_specs=[
            pl.BlockSpec(cvec.shape, lambda l, j: (0, 0)),
            pl.BlockSpec((None, d, tn), lambda l, j: (l, 0, j)),
            pl.BlockSpec((None, 1, tn), lambda l, j: (l, 0, j)),
        ],
        out_specs=pl.BlockSpec((None, cvec.shape[0], tn), lambda l, j: (l, 0, j)),
        compiler_params=_params(("arbitrary", "arbitrary")),
        name="modulation",
    )(cvec, w_mod, b_mod.reshape(depth, 1, d6))


def _pre_kernel(ctx_tiles, xc_ref, xl_ref, mod_ref, nw_ref, win_f32_ref, conf_ref, q_ref, k_ref, v_ref, lx_ref,
                lg_ref, win_ref):
    @pl.when(pl.program_id(0) == 0)
    def _():
        win_ref[...] = win_f32_ref[...].astype(BF16)

    x = jnp.where(pl.program_id(0) < ctx_tiles, xc_ref[...], xl_ref[...])
    m = mod_ref[...]
    shift, scale = m[0:1], m[1:2]
    h = x * lax.rsqrt(jnp.mean(x * x, axis=-1, keepdims=True) + EPS) * nw_ref[...]
    h = h * (1.0 + scale) + shift
    proj = _dot(h.astype(BF16), win_ref[...])
    col = 0
    for ref in (conf_ref, q_ref, k_ref, v_ref, lx_ref, lg_ref):
        w = ref.shape[-1]
        ref[...] = proj[:, col:col + w]
        col += w


def _path_specs(x_pair, tm):
    ctx_tiles = x_pair[0].shape[0] // tm
    return ctx_tiles, [
        pl.BlockSpec((tm, x_pair[0].shape[-1]), lambda i: (jnp.minimum(i, ctx_tiles - 1), 0)),
        pl.BlockSpec((tm, x_pair[1].shape[-1]), lambda i: (jnp.maximum(i - ctx_tiles, 0), 0))]


def _pre_mixer(x_pair, mod_l, mod_row, norm_w, w_in, widths):
    n, d = x_pair[0].shape[0] + x_pair[1].shape[0], x_pair[0].shape[1]
    ctx_tiles, x_specs = _path_specs(x_pair, TOKEN_TILE)
    return pl.pallas_call(
        functools.partial(_pre_kernel, ctx_tiles),
        out_shape=[jax.ShapeDtypeStruct((n, w), F32) for w in widths],
        grid=(n // TOKEN_TILE,),
        in_specs=x_specs + [
            pl.BlockSpec((None, 6, d), lambda i: (mod_row(i), 0, 0)),
            pl.BlockSpec((1, d), lambda i: (0, 0)),
            pl.BlockSpec(w_in.shape, lambda i: (0, 0)),
        ],
        out_specs=[pl.BlockSpec((TOKEN_TILE, w), lambda i: (i, 0)) for w in widths],
        scratch_shapes=[pltpu.VMEM(w_in.shape, BF16)],
        compiler_params=_params(("arbitrary",), VMEM_LIMIT),
        name="pre_mixer",
    )(*x_pair, mod_l, norm_w.reshape(1, d), w_in)


def _gelu_tanh(x):
    return 0.5 * x * (1.0 + jnp.tanh(0.7978845608028654 * (x + 0.044715 * (x * x * x))))


def _seq_kernel(seq_len, conf_ref, lx_ref, lg_ref, h0_ref, ccw_ref, ccb_ref, lnw_ref, lnb_ref,
                lcw_ref, lcb_ref, wg_ref, bg_ref, lam_ref,
                co_ref, lo_ref, hl_ref, pad_ref, rot_ref, af_ref, uf_ref, ab_ref, ub_ref):
    L = seq_len
    W = co_ref.shape[-1]
    pad = L // 2
    zeros_halo = jnp.zeros((CONV_HALO, W), F32)

    pad_ref[0:CONV_HALO, :] = zeros_halo
    pad_ref[CONV_HALO + L:2 * CONV_HALO + L, :] = zeros_halo
    for c in range(L // ROW_CHUNK):
        r0 = c * ROW_CHUNK
        u = conf_ref[r0:r0 + ROW_CHUNK, :]
        pad_ref[CONV_HALO + r0:CONV_HALO + r0 + ROW_CHUNK, :] = u[:, :W] * jax.nn.sigmoid(u[:, W:])
    shifted_rows = L + 2 * CONV_HALO - SUBLANES
    for s in range(1, SUBLANES):
        for r0 in range(0, shifted_rows, ROW_CHUNK):
            rows = min(ROW_CHUNK, shifted_rows - r0)
            rot_ref[s - 1, r0:r0 + rows, :] = pad_ref[r0 + s:r0 + s + rows, :]
    left = CONF_KERNEL // 2
    for c in range(L // ROW_CHUNK):
        r0 = c * ROW_CHUNK
        acc = jnp.zeros((ROW_CHUNK, W), F32) + ccb_ref[...]
        for k in range(CONF_KERNEL):
            whole, s = divmod(CONV_HALO - left + k, SUBLANES)
            start = whole * SUBLANES + r0
            src = pad_ref if s == 0 else rot_ref.at[s - 1]
            acc = acc + ccw_ref[k:k + 1, :] * src[start:start + ROW_CHUNK, :]
        mu = jnp.mean(acc, axis=-1, keepdims=True)
        cen = acc - mu
        var = jnp.mean(cen * cen, axis=-1, keepdims=True)
        y = cen * lax.rsqrt(var + EPS) * lnw_ref[...] + lnb_ref[...]
        co_ref[r0:r0 + ROW_CHUNK, :] = y * jax.nn.sigmoid(y)

    for c in range(L // ROW_CHUNK):
        r0 = c * ROW_CHUNK
        pad_ref[CONV_HALO + r0:CONV_HALO + r0 + ROW_CHUNK, :] = lx_ref[r0:r0 + ROW_CHUNK, :]
    ones_pad = jnp.ones((pad, W), F32)
    zeros_pad = jnp.zeros((pad, W), F32)
    af_ref[0:pad, :] = ones_pad
    uf_ref[0:pad, :] = zeros_pad
    ab_ref[L:L + pad, :] = ones_pad
    ub_ref[L:L + pad, :] = zeros_pad
    sp = jax.nn.softplus(-lam_ref[...])
    h0 = h0_ref[...]
    left = LRU_CONV // 2
    n_chunks = L // ROW_CHUNK
    for c in range(n_chunks):
        r0 = c * ROW_CHUNK
        xc = jnp.zeros((ROW_CHUNK, W), F32) + lcb_ref[...]
        for k in range(LRU_CONV):
            start = CONV_HALO - left + k + r0
            xc = xc + lcw_ref[k:k + 1, :] * pad_ref[start:start + ROW_CHUNK, :]
        g = _dot(xc.astype(BF16), wg_ref[...]) + bg_ref[...]
        r = jax.nn.sigmoid(g[:, :2 * W])
        i = jax.nn.sigmoid(g[:, 2 * W:])
        a = jnp.exp((-LRU_C) * r * sp)
        xc2 = jnp.concatenate([xc, xc], axis=-1)
        u = jnp.sqrt(1.0 - a * a) * i * xc2
        a_f, a_b, u_f, u_b = a[:, :W], a[:, W:], u[:, :W], u[:, W:]
        row = lax.broadcasted_iota(jnp.int32, (ROW_CHUNK, W), 0)
        if c == 0:
            u_f = jnp.where(row == 0, u_f + a_f * h0[0:1], u_f)
        if c == n_chunks - 1:
            u_b = jnp.where(row == ROW_CHUNK - 1, u_b + a_b * h0[1:2], u_b)
        af_ref[pad + r0:pad + r0 + ROW_CHUNK, :] = a_f
        uf_ref[pad + r0:pad + r0 + ROW_CHUNK, :] = u_f
        ab_ref[r0:r0 + ROW_CHUNK, :] = a_b
        ub_ref[r0:r0 + ROW_CHUNK, :] = u_b

    n_sc = L // SCAN_CHUNK
    s = 1
    while s < L:
        for c in reversed(range(n_sc)):
            r0 = c * SCAN_CHUNK
            if r0 + SCAN_CHUNK <= s:
                continue
            cur = slice(pad + r0, pad + r0 + SCAN_CHUNK)
            sh = slice(pad + r0 - s, pad + r0 - s + SCAN_CHUNK)
            a_cur = af_ref[cur, :]
            uf_ref[cur, :] = uf_ref[cur, :] + a_cur * uf_ref[sh, :]
            af_ref[cur, :] = a_cur * af_ref[sh, :]
        for c in range(n_sc):
            r0 = c * SCAN_CHUNK
            if r0 >= L - s:
                continue
            cur = slice(r0, r0 + SCAN_CHUNK)
            sh = slice(r0 + s, r0 + s + SCAN_CHUNK)
            a_cur = ab_ref[cur, :]
            ub_ref[cur, :] = ub_ref[cur, :] + a_cur * ub_ref[sh, :]
            ab_ref[cur, :] = a_cur * ab_ref[sh, :]
        s *= 2

    for c in range(n_chunks):
        r0 = c * ROW_CHUNK
        h = uf_ref[pad + r0:pad + r0 + ROW_CHUNK, :] + ub_ref[r0:r0 + ROW_CHUNK, :]
        lo_ref[r0:r0 + ROW_CHUNK, :] = h * _gelu_tanh(lg_ref[r0:r0 + ROW_CHUNK, :])
    hl_ref[0:1, :] = uf_ref[pad + L - 1:pad + L, :]
    hl_ref[1:2, :] = ub_ref[0:1, :]


def _seq_mixers(seq_len, n_seq, row0, conf_u, lru_x, lru_g, h0, p):
    w = lru_x.shape[-1]
    b0 = row0 // seq_len
    pad = seq_len // 2
    in_spec = lambda width: pl.BlockSpec((seq_len, width), lambda i: (b0 + i, 0))
    out_spec = pl.BlockSpec((seq_len, w), lambda i: (i, 0))
    state_spec = pl.BlockSpec((None, 2, w), lambda i: (i, 0, 0))
    full = lambda a: pl.BlockSpec(a.shape, lambda i: (0,) * a.ndim)
    weights = (p['conf_conv_w'], p['conf_conv_b'], p['conf_ln_w'], p['conf_ln_b'],
               p['lru_conv_w'], p['lru_conv_b'], p['lru_wg'], p['lru_bg'], p['lru_lam'])
    padded = seq_len + 2 * CONV_HALO
    return pl.pallas_call(
        functools.partial(_seq_kernel, seq_len),
        out_shape=[jax.ShapeDtypeStruct((n_seq * seq_len, w), F32)] * 2
        + [jax.ShapeDtypeStruct((n_seq, 2, w), F32)],
        grid=(n_seq,),
        in_specs=[in_spec(2 * w), in_spec(w), in_spec(w), state_spec] + [full(a) for a in weights],
        out_specs=[out_spec, out_spec, state_spec],
        scratch_shapes=[pltpu.VMEM((padded, w), F32), pltpu.VMEM((SUBLANES - 1, padded, w), F32)]
        + [pltpu.VMEM((seq_len + pad, w), F32)] * 4,
        compiler_params=_params(("arbitrary",), VMEM_LIMIT),
        name=f"seq_mixers_{seq_len}",
    )(conf_u, lru_x, lru_g, h0, *weights)


def _head_rms(x):
    return lax.rsqrt(jnp.mean(x * x, axis=-1, keepdims=True) + EPS)


def _swap_halves(x):
    width = x.shape[-1]
    lane = lax.broadcasted_iota(jnp.int32, x.shape, x.ndim - 1)
    up = pltpu.roll(x, width - HEAD_DIM // 4, x.ndim - 1)
    down = pltpu.roll(x, HEAD_DIM // 4, x.ndim - 1)
    return jnp.where((lane % (HEAD_DIM // 2)) < HEAD_DIM // 4, up, down)


def _attend(q_heads, k_bf16, v_bf16):
    scale = HEAD_DIM ** -0.5
    s = lax.dot_general(q_heads.astype(BF16), k_bf16, (((1,), (1,)), ((), ())),
                        preferred_element_type=F32) * scale
    p = jnp.exp(s - jnp.max(s, axis=-1, keepdims=True))
    denom = jnp.sum(p, axis=-1, keepdims=True)
    return _dot(p.astype(BF16), v_bf16) / denom


def _attn_ctx_kernel(q_ref, k_ref, v_ref, qw_ref, kw_ref, o_ref, kn_ref):
    L = q_ref.shape[0]
    q, k, v = q_ref[...], k_ref[...], v_ref[...]
    qw, kw = qw_ref[...], kw_ref[...]
    k_out, o_out = [], []
    for g in range(N_KV_HEADS):
        kh = k[:, g * HEAD_DIM:(g + 1) * HEAD_DIM]
        kh = kh * _head_rms(kh) * kw
        k_out.append(kh)
        qs = []
        for j in range(HEADS_PER_KV):
            h = g * HEADS_PER_KV + j
            qh = q[:, h * HEAD_DIM:(h + 1) * HEAD_DIM]
            qs.append(qh * _head_rms(qh) * qw)
        o = _attend(jnp.concatenate(qs, axis=0), kh.astype(BF16),
                    v[:, g * HEAD_DIM:(g + 1) * HEAD_DIM].astype(BF16))
        o_out += [o[j * L:(j + 1) * L] for j in range(HEADS_PER_KV)]
    kn_ref[...] = jnp.concatenate(k_out, axis=-1)
    o_ref[...] = jnp.concatenate(o_out, axis=-1)


def _attention_ctx(seq_len, n_seq, q, k, v, qw, kw):
    kvw = k.shape[-1]
    row_spec = lambda width: pl.BlockSpec((seq_len, width), lambda i: (i, 0))
    full = lambda a: pl.BlockSpec(a.shape, lambda i: (0,) * a.ndim)
    return pl.pallas_call(
        _attn_ctx_kernel,
        out_shape=[jax.ShapeDtypeStruct((n_seq * seq_len, q.shape[-1]), F32),
                   jax.ShapeDtypeStruct((n_seq * seq_len, kvw), F32)],
        grid=(n_seq,),
        in_specs=[row_spec(q.shape[-1]), row_spec(kvw), row_spec(kvw), full(qw), full(kw)],
        out_specs=[row_spec(q.shape[-1]), row_spec(kvw)],
        compiler_params=_params(("arbitrary",), VMEM_LIMIT),
        name="attention_ctx",
    )(q, k, v, qw, kw)


def _attn_lat_kernel(q_ref, k_ref, v_ref, ck_ref, cv_ref, qw_ref, kw_ref, cq_ref, sq_ref, ck_t_ref,
                     sk_t_ref, o_ref, kall_ref, vall_ref):
    L = k_ref.shape[0]
    tq = q_ref.shape[0]

    @pl.when(pl.program_id(1) == 0)
    def _():
        k = k_ref[...]
        t = k * kw_ref[...]
        rot = t * ck_t_ref[...] + _swap_halves(t) * sk_t_ref[...]
        parts = []
        for g in range(N_KV_HEADS):
            sl = slice(g * HEAD_DIM, (g + 1) * HEAD_DIM)
            parts.append(rot[:, sl] * _head_rms(k[:, sl]))
        kall_ref[0:L, :] = jnp.concatenate(parts, axis=-1).astype(BF16)
        kall_ref[L:, :] = ck_ref[...].astype(BF16)
        vall_ref[0:L, :] = v_ref[...].astype(BF16)
        vall_ref[L:, :] = cv_ref[...].astype(BF16)

    q = q_ref[...]
    t = q * qw_ref[...]
    rot = t * cq_ref[...] + _swap_halves(t) * sq_ref[...]
    o_out = []
    for g in range(N_KV_HEADS):
        qs = []
        for j in range(HEADS_PER_KV):
            sl = slice((g * HEADS_PER_KV + j) * HEAD_DIM, (g * HEADS_PER_KV + j + 1) * HEAD_DIM)
            qs.append(rot[:, sl] * _head_rms(q[:, sl]))
        sl = slice(g * HEAD_DIM, (g + 1) * HEAD_DIM)
        o = _attend(jnp.concatenate(qs, axis=0), kall_ref[:, sl], vall_ref[:, sl])
        o_out += [o[j * tq:(j + 1) * tq] for j in range(HEADS_PER_KV)]
    o_ref[...] = jnp.concatenate(o_out, axis=-1)


def _attention_lat(seq_len, n_seq, row0, layer, q, k, v, cache_k, cache_v, qw8, kw2, rope):
    tq = TOKEN_TILE
    nq = seq_len // tq
    qwid, kvw = q.shape[-1], k.shape[-1]
    past = cache_k.shape[2]
    b0q = row0 // tq
    b0s = row0 // seq_len
    cq, sq, ck, sk = rope
    full = lambda a: pl.BlockSpec(a.shape, lambda b, j: (0,) * a.ndim)
    seq_spec = pl.BlockSpec((seq_len, kvw), lambda b, j: (b0s + b, 0))
    cache_spec = pl.BlockSpec((None, None, past, kvw), lambda b, j: (b, layer, 0, 0))
    q_spec = pl.BlockSpec((tq, qwid), lambda b, j: (b0q + b * nq + j, 0))
    rope_q_spec = pl.BlockSpec((tq, qwid), lambda b, j: (j, 0))
    return pl.pallas_call(
        _attn_lat_kernel,
        out_shape=jax.ShapeDtypeStruct((n_seq * seq_len, qwid), F32),
        grid=(n_seq, nq),
        in_specs=[q_spec, seq_spec, seq_spec, cache_spec, cache_spec, full(qw8), full(kw2),
                  rope_q_spec, rope_q_spec, full(ck), full(sk)],
        out_specs=pl.BlockSpec((tq, qwid), lambda b, j: (b * nq + j, 0)),
        scratch_shapes=[pltpu.VMEM((seq_len + past, kvw), BF16)] * 2,
        compiler_params=_params(("arbitrary", "arbitrary"), VMEM_LIMIT),
        name="attention_lat",
    )(q, k, v, cache_k, cache_v, qw8, kw2, cq, sq, ck, sk)


def _rope_tables(seq_len):
    t = jnp.arange(seq_len)
    row = (t // GRID_W).astype(F32)
    col = (t % GRID_W).astype(F32)
    half = HEAD_DIM // 2
    freqs = ROPE_THETA ** (-jnp.arange(0, half, 2, dtype=F32) / half)
    ang_r, ang_c = row[:, None] * freqs, col[:, None] * freqs
    cos = jnp.concatenate([jnp.cos(ang_r)] * 2 + [jnp.cos(ang_c)] * 2, axis=-1)
    sin = jnp.concatenate([-jnp.sin(ang_r), jnp.sin(ang_r), -jnp.sin(ang_c), jnp.sin(ang_c)], axis=-1)
    return (jnp.tile(cos, (1, N_HEADS)), jnp.tile(sin, (1, N_HEADS)),
            jnp.tile(cos, (1, N_KV_HEADS)), jnp.tile(sin, (1, N_KV_HEADS)))


def _slab_copy(src_ref, src_tok, dst_ref, dst_tok, sem):
    src = src_ref.at[pl.ds(pl.multiple_of(src_tok * SUBLANES, SUBLANES), SUBLANES)]
    dst = dst_ref.at[pl.ds(pl.multiple_of(dst_tok * SUBLANES, SUBLANES), SUBLANES)]
    return pltpu.make_async_copy(src, dst, sem)


def _load_slabs(ref, n_tok, lead=()):
    return jnp.concatenate(
        [ref[(*lead, pl.ds(j, n_tok, stride=SUBLANES), slice(None))] for j in range(SUBLANES)], axis=-1)


def _store_slabs(ref, val):
    for j in range(SUBLANES):
        ref[pl.ds(j, val.shape[0], stride=SUBLANES), :] = val[:, j * LANES:(j + 1) * LANES]


def _wait_slabs(hbm_ref, n_tok, sem):
    span = hbm_ref.at[pl.ds(0, n_tok * SUBLANES)]
    pltpu.make_async_copy(span, span, sem).wait()


def _fetch_routes(topi_ref, rank_ref, ti_ref, rk_ref, sems, tile):
    n = ti_ref.shape[0]
    copies = [pltpu.make_async_copy(src.at[pl.ds(tile * n, n)], dst, sems.at[j])
              for j, (src, dst) in enumerate(((topi_ref, ti_ref), (rank_ref, rk_ref)))]
    for copy in copies:
        copy.start()
    for copy in copies:
        copy.wait()


def _choice_major(x):
    t = jnp.transpose(x)
    return jnp.concatenate([t[k:k + 1, :] for k in range(TOP_K)], axis=1)


def _post_kernel(ctx_tiles, xc_ref, xl_ref, co_c_ref, ao_c_ref, lo_c_ref, co_l_ref, ao_l_ref, lo_l_ref, mod_ref,
                 wout_f32_ref, nw_ref, rw_ref, rb_ref,
                 x1_ref, h2_ref, topi_ref, gates_ref, rank_ref, counts_ref, carry_ref, wout_ref):
    tm = xc_ref.shape[0]
    cw, aw = co_c_ref.shape[-1], ao_c_ref.shape[-1]
    is_ctx = pl.program_id(0) < ctx_tiles

    @pl.when(pl.program_id(0) == 0)
    def _():
        carry_ref[...] = jnp.zeros_like(carry_ref)
        wout_ref[...] = wout_f32_ref[...].astype(BF16)

    m = mod_ref[...]
    gate1, shift2, scale2 = m[2:3], m[3:4], m[4:5]
    pick = lambda c_ref, l_ref: jnp.where(is_ctx, c_ref[...], l_ref[...]).astype(BF16)
    x = jnp.where(is_ctx, xc_ref[...], xl_ref[...])
    mixed = (_dot(pick(co_c_ref, co_l_ref), wout_ref[0:cw, :])
             + _dot(pick(ao_c_ref, ao_l_ref), wout_ref[cw:cw + aw, :])
             + _dot(pick(lo_c_ref, lo_l_ref), wout_ref[cw + aw:, :]))
    x1 = x + gate1 * mixed
    x1_ref[...] = x1
    h2 = x1 * lax.rsqrt(jnp.mean(x1 * x1, axis=-1, keepdims=True) + EPS) * nw_ref[...]
    h2 = h2 * (1.0 + scale2) + shift2
    _store_slabs(h2_ref, h2)

    lane = lax.broadcasted_iota(jnp.int32, (tm, LANES), 1)
    lane_f = lane.astype(F32)
    logits = jnp.where(lane < N_EXPERTS, _dot3(h2, rw_ref[...]) + rb_ref[...], -jnp.inf)
    top_v, onehots = [], []
    topi = jnp.zeros((tm, LANES), F32)
    for k in range(TOP_K):
        mx = jnp.max(logits, axis=-1, keepdims=True)
        idx = jnp.min(jnp.where(logits == mx, lane_f, float(LANES)), axis=-1, keepdims=True)
        hit = lane_f == idx
        logits = jnp.where(hit, -jnp.inf, logits)
        top_v.append(mx)
        onehots.append(hit)
        topi = jnp.where(lane == k, idx, topi)
    topi_ref[...] = _choice_major(topi).astype(jnp.int32)
    exps = [jnp.exp(v - top_v[0]) for v in top_v]
    denom = exps[0] + exps[1] + exps[2] + exps[3]
    gates = jnp.zeros((tm, LANES), F32)
    for k in range(TOP_K):
        gates = jnp.where(lane == k, exps[k] / denom, gates)
    gates_ref[...] = gates

    chosen = jnp.zeros((tm, LANES), F32)
    for hit in onehots:
        chosen = jnp.where(hit, 1.0, chosen)
    r_i = lax.broadcasted_iota(jnp.int32, (tm, tm), 0)
    c_i = lax.broadcasted_iota(jnp.int32, (tm, tm), 1)
    lower = jnp.where(c_i < r_i, 1.0, 0.0).astype(BF16)
    before = _dot(lower, chosen.astype(BF16)) + carry_ref[...]
    rank = jnp.zeros((tm, LANES), F32)
    for k, hit in enumerate(onehots):
        rk = jnp.sum(jnp.where(hit, before, 0.0), axis=-1, keepdims=True)
        rank = jnp.where(lane == k, rk, rank)
    rank_ref[...] = _choice_major(rank).astype(jnp.int32)
    carry = carry_ref[...] + jnp.sum(chosen, axis=0, keepdims=True)
    carry_ref[...] = carry
    counts_ref[...] = carry


def _post_mixer(x_pair, ctx_outs, lat_outs, mod_l, mod_row, w_out, norm_w, router_w, router_b):
    n, d = x_pair[0].shape[0] + x_pair[1].shape[0], x_pair[0].shape[1]
    tm = TOKEN_TILE
    ctx_tiles, x_specs = _path_specs(x_pair, tm)
    row_spec = lambda width: pl.BlockSpec((tm, width), lambda i: (i, 0))
    ctx_spec = lambda a: pl.BlockSpec((tm, a.shape[-1]), lambda i: (jnp.minimum(i, ctx_tiles - 1), 0))
    lat_spec = lambda a: pl.BlockSpec((tm, a.shape[-1]), lambda i: (jnp.maximum(i - ctx_tiles, 0), 0))
    full = lambda a: pl.BlockSpec(a.shape, lambda i: (0,) * a.ndim)
    rw = jnp.pad(router_w, ((0, 0), (0, LANES - N_EXPERTS)))
    rb = jnp.pad(router_b, (0, LANES - N_EXPERTS)).reshape(1, LANES)
    nw = norm_w.reshape(1, d)
    lane_tile = pl.BlockSpec((tm, LANES), lambda i: (i, 0))
    flat_tile = pl.BlockSpec((None, 1, TOP_K * tm), lambda i: (i, 0, 0))
    flat_shape = jax.ShapeDtypeStruct((n // tm, 1, TOP_K * tm), jnp.int32)
    return pl.pallas_call(
        functools.partial(_post_kernel, ctx_tiles),
        out_shape=[jax.ShapeDtypeStruct((n, d), F32), jax.ShapeDtypeStruct((n * SUBLANES, LANES), F32),
                   flat_shape, jax.ShapeDtypeStruct((n, LANES), F32),
                   flat_shape, jax.ShapeDtypeStruct((1, LANES), F32)],
        grid=(n // tm,),
        in_specs=x_specs + [ctx_spec(a) for a in ctx_outs] + [lat_spec(a) for a in lat_outs]
        + [pl.BlockSpec((None, 6, d), lambda i: (mod_row(i), 0, 0)),
           full(w_out), full(nw), full(rw), full(rb)],
        out_specs=[row_spec(d), pl.BlockSpec((tm * SUBLANES, LANES), lambda i: (i, 0)),
                   flat_tile, lane_tile, flat_tile,
                   pl.BlockSpec((1, LANES), lambda i: (0, 0))],
        scratch_shapes=[pltpu.VMEM((1, LANES), F32), pltpu.VMEM(w_out.shape, BF16)],
        compiler_params=_params(("arbitrary",), VMEM_LIMIT),
        name="post_mixer",
    )(*x_pair, *ctx_outs, *lat_outs, mod_l, w_out, nw, rw, rb)


def _dispatch_kernel(tend_ref, offs_ref, h2_ref, topi_ref, rank_ref, xs_ref, ti_ref, rk_ref, zero_ref,
                     idx_sem, row_sem):
    tm = h2_ref.shape[0] // SUBLANES
    tile_rows = EXPERT_TILE * SUBLANES

    @pl.when(pl.program_id(0) == 0)
    def _():
        zero_ref[...] = jnp.zeros_like(zero_ref)

        def last_tile_copy(e):
            start = pl.multiple_of((tend_ref[e] - 1) * tile_rows, tile_rows)
            return pltpu.make_async_copy(zero_ref, xs_ref.at[pl.ds(start, tile_rows)], row_sem)

        def has_tiles(e):
            return tend_ref[e] > (tend_ref[e - 1] if e else 0)

        for e in range(N_EXPERTS):
            pl.when(has_tiles(e))(lambda e=e: last_tile_copy(e).start())
        for e in range(N_EXPERTS):
            pl.when(has_tiles(e))(lambda e=e: last_tile_copy(e).wait())

        def spare_tile_copy(t):
            return pltpu.make_async_copy(
                zero_ref, xs_ref.at[pl.ds(pl.multiple_of(t * tile_rows, tile_rows), tile_rows)], row_sem)

        n_used, n_tiles = tend_ref[N_EXPERTS - 1], xs_ref.shape[0] // tile_rows
        lax.fori_loop(n_used, n_tiles, lambda t, c: (spare_tile_copy(t).start(), c)[1], 0)
        lax.fori_loop(n_used, n_tiles, lambda t, c: (spare_tile_copy(t).wait(), c)[1], 0)

    _fetch_routes(topi_ref, rank_ref, ti_ref, rk_ref, idx_sem, pl.program_id(0))

    def issue(r, carry):
        for k in range(TOP_K):
            row = offs_ref[ti_ref[k * tm + r]] + rk_ref[k * tm + r]
            _slab_copy(h2_ref, r, xs_ref, row, row_sem).start(priority=k % 2)
        return carry

    lax.fori_loop(0, tm, issue, 0, unroll=ISSUE_UNROLL)
    _wait_slabs(xs_ref, tm * TOP_K, row_sem)


def _dispatch(h2_slabs, topi_flat, rank_flat, tile_end, offs, n_tiles):
    tm = TOKEN_TILE
    n = h2_slabs.shape[0] // SUBLANES
    hbm = pl.BlockSpec(memory_space=pl.ANY)
    grid_spec = pltpu.PrefetchScalarGridSpec(
        num_scalar_prefetch=2,
        grid=(n // tm,),
        in_specs=[pl.BlockSpec((tm * SUBLANES, LANES), lambda i, tend, offs: (i, 0)), hbm, hbm],
        out_specs=hbm,
        scratch_shapes=[pltpu.SMEM((tm * TOP_K,), jnp.int32), pltpu.SMEM((tm * TOP_K,), jnp.int32),
                        pltpu.VMEM((EXPERT_TILE * SUBLANES, LANES), F32),
                        pltpu.SemaphoreType.DMA((2,)), pltpu.SemaphoreType.DMA],
    )
    return pl.pallas_call(
        _dispatch_kernel,
        out_shape=jax.ShapeDtypeStruct((n_tiles * EXPERT_TILE * SUBLANES, LANES), F32),
        grid_spec=grid_spec,
        compiler_params=_params(("arbitrary",)),
        name="moe_dispatch",
    )(tile_end, offs, h2_slabs, topi_flat, rank_flat)


def _expert_kernel(layer, te_ref, nxt_ref, par_ref, nu_ref, xs_ref, wgu_hbm, bgu_ref, wd_hbm, bd_ref,
                   ys_ref, wgu_buf, wd_buf, wgu_bf, wd_bf, w_sem):
    i = pl.program_id(0)
    dff = wd_hbm.shape[-2]

    def weight_copies(e, slot):
        return (pltpu.make_async_copy(wgu_hbm.at[layer, e], wgu_buf.at[slot], w_sem.at[slot]),
                pltpu.make_async_copy(wd_hbm.at[layer, e], wd_buf.at[slot], w_sem.at[2 + slot]))

    @pl.when(i < nu_ref[0])
    def _():
        prev = te_ref[jnp.maximum(i - 1, 0)]

        @pl.when((i == 0) | (te_ref[i] != prev))
        def _():
            slot = par_ref[i]

            @pl.when(i == 0)
            def _():
                for copy in weight_copies(te_ref[0], slot):
                    copy.start()

            for copy in weight_copies(te_ref[i], slot):
                copy.wait()

            @pl.when(nxt_ref[i] >= 0)
            def _():
                for copy in weight_copies(nxt_ref[i], 1 - slot):
                    copy.start()

            wgu_bf[...] = wgu_buf[slot].astype(BF16)
            wd_bf[...] = wd_buf[slot].astype(BF16)

        x = _load_slabs(xs_ref, EXPERT_TILE)
        gu = _dot(x.astype(BF16), wgu_bf[...]) + bgu_ref[...]
        x_glu = jnp.minimum(gu[:, :dff], SWIGLU_LIMIT)
        x_lin = jnp.clip(gu[:, dff:], -SWIGLU_LIMIT, SWIGLU_LIMIT)
        act = x_glu * jax.nn.sigmoid(SWIGLU_ALPHA * x_glu) * (x_lin + 1.0)
        _store_slabs(ys_ref, _dot(act.astype(BF16), wd_bf[...]) + bd_ref[...])

    @pl.when(i >= nu_ref[0])
    def _():
        ys_ref[...] = jnp.zeros_like(ys_ref)


def _experts(layer, tile_expert, next_expert, parity, n_used, xs, w_gu, b_gu, w_down, b_down):
    tm = EXPERT_TILE
    d, dff2 = w_gu.shape[-2:]
    dff = w_down.shape[-2]
    slab_tile = (tm * SUBLANES, LANES)
    hbm = pl.BlockSpec(memory_space=pl.ANY)
    grid_spec = pltpu.PrefetchScalarGridSpec(
        num_scalar_prefetch=4,
        grid=(xs.shape[0] // slab_tile[0],),
        in_specs=[
            pl.BlockSpec(slab_tile, lambda i, te, nx, pa, nu: (jnp.minimum(i, nu[0] - 1), 0)),
            hbm,
            pl.BlockSpec((None, None, 1, dff2), lambda i, te, nx, pa, nu: (layer, te[i], 0, 0)),
            hbm,
            pl.BlockSpec((None, None, 1, d), lambda i, te, nx, pa, nu: (layer, te[i], 0, 0)),
        ],
        out_specs=pl.BlockSpec(slab_tile, lambda i, te, nx, pa, nu: (i, 0)),
        scratch_shapes=[pltpu.VMEM((2, d, dff2), F32), pltpu.VMEM((2, dff, d), F32),
                        pltpu.VMEM((d, dff2), BF16), pltpu.VMEM((dff, d), BF16),
                        pltpu.SemaphoreType.DMA((4,))],
    )
    depth, n_e = w_gu.shape[:2]
    return pl.pallas_call(
        functools.partial(_expert_kernel, layer),
        out_shape=jax.ShapeDtypeStruct(xs.shape, F32),
        grid_spec=grid_spec,
        compiler_params=_params(("arbitrary",), VMEM_LIMIT),
        name="moe_experts",
    )(tile_expert, next_expert, parity, n_used, xs, w_gu, b_gu.reshape(depth, n_e, 1, dff2), w_down,
      b_down.reshape(depth, n_e, 1, d))


def _combine_kernel(final, tile0, offs_ref, x1_ref, gates_ref, mod_ref, fw_ref, topi_ref, rank_ref, ys_ref,
                    out_ref, ti_ref, rk_ref, rows_ref, idx_sem, row_sem):
    tm = x1_ref.shape[0]
    _fetch_routes(topi_ref, rank_ref, ti_ref, rk_ref, idx_sem, pl.program_id(0) + tile0)

    def issue(r, carry):
        for k in range(TOP_K):
            row = offs_ref[ti_ref[k * tm + r]] + rk_ref[k * tm + r]
            _slab_copy(ys_ref, row, rows_ref.at[k], r, row_sem).start(priority=k % 2)
        return carry

    lax.fori_loop(0, tm, issue, 0, unroll=ISSUE_UNROLL)
    _wait_slabs(ys_ref, tm * TOP_K, row_sem)

    gates = gates_ref[...]
    moe = gates[:, 0:1] * _load_slabs(rows_ref, tm, (0,))
    for k in range(1, TOP_K):
        moe = moe + gates[:, k:k + 1] * _load_slabs(rows_ref, tm, (k,))
    x2 = x1_ref[...] + mod_ref[5:6, :] * moe
    if final:
        x2 = x2 * lax.rsqrt(jnp.mean(x2 * x2, axis=-1, keepdims=True) + EPS) * fw_ref[...]
    out_ref[...] = x2


def _combine(final, row0, n_rows, x1, gates, mod_l, mod_row, final_w, topi_flat, rank_flat, offs, ys):
    d = x1.shape[1]
    tm = TOKEN_TILE
    tile0 = row0 // tm
    hbm = pl.BlockSpec(memory_space=pl.ANY)
    grid_spec = pltpu.PrefetchScalarGridSpec(
        num_scalar_prefetch=1,
        grid=(n_rows // tm,),
        in_specs=[pl.BlockSpec((tm, d), lambda i, offs: (tile0 + i, 0)),
                  pl.BlockSpec((tm, LANES), lambda i, offs: (tile0 + i, 0)),
                  pl.BlockSpec((None, 6, d), lambda i, offs: (mod_row(tile0 + i), 0, 0)),
                  pl.BlockSpec((1, d), lambda i, offs: (0, 0)),
                  hbm, hbm, hbm],
        out_specs=pl.BlockSpec((tm, d), lambda i, offs: (i, 0)),
        scratch_shapes=[pltpu.SMEM((tm * TOP_K,), jnp.int32), pltpu.SMEM((tm * TOP_K,), jnp.int32),
                        pltpu.VMEM((TOP_K, tm * SUBLANES, LANES), F32),
                        pltpu.SemaphoreType.DMA((2,)), pltpu.SemaphoreType.DMA],
    )
    return pl.pallas_call(
        functools.partial(_combine_kernel, final, tile0),
        out_shape=jax.ShapeDtypeStruct((n_rows, d), F32),
        grid_spec=grid_spec,
        compiler_params=_params(("arbitrary",), VMEM_LIMIT),
        name="moe_combine",
    )(offs, x1, gates, mod_l, final_w.reshape(1, d), topi_flat, rank_flat, ys)


def _routing_tables(counts, n_tiles):
    tm = EXPERT_TILE
    c = counts[0, :N_EXPERTS].astype(jnp.int32)
    tiles = (c + tm - 1) // tm
    tile_end = jnp.cumsum(tiles)
    offs = (tile_end - tiles) * tm
    n_used = tile_end[-1]
    t = jnp.minimum(jnp.arange(n_tiles, dtype=jnp.int32), n_used - 1)
    te = jnp.minimum(jnp.sum((tile_end[None, :] <= t[:, None]).astype(jnp.int32), axis=1), N_EXPERTS - 1)
    e = jnp.arange(N_EXPERTS, dtype=jnp.int32)
    later = (e[None, :] > e[:, None]) & (tiles[None, :] > 0)
    nxt = jnp.min(jnp.where(later, e[None, :], N_EXPERTS), axis=1)
    nxt = jnp.where(nxt == N_EXPERTS, -1, nxt)
    parity = (jnp.cumsum((tiles > 0).astype(jnp.int32)) - 1) % 2
    return (offs, tile_end.astype(jnp.int32), te, nxt[te].astype(jnp.int32),
            parity[te].astype(jnp.int32), n_used.reshape(1).astype(jnp.int32))


def _block_diag(w):
    dirs, heads, blk, _ = w.shape
    eye = jnp.eye(heads, dtype=w.dtype)
    full = jnp.einsum('dhij,hg->hidgj', w, eye)
    return full.reshape(heads * blk, dirs * heads * blk)


def kernel(x_prompt, x_sample, cache_k, cache_v, state_lru, c, c_ctx, w_mod, b_mod, norm_mix_w, w_in, conf_conv_w, conf_conv_b, conf_ln_w, conf_ln_b, q_norm_w, k_norm_w, lru_conv_w, lru_conv_b, lru_wa, lru_ba, lru_wx, lru_bx, lru_lambda, w_out, norm_ffn_w, router_w, router_b, w_gu, b_gu, w_down, b_down, final_norm_w):
    batch, seq, d = x_prompt.shape
    dec_batch, dec_seq, _ = x_sample.shape
    depth = w_mod.shape[0]
    n_ctx, n_lat = batch * seq, dec_batch * dec_seq
    n = n_ctx + n_lat
    conf_w = conf_conv_w.shape[-1]
    lru_w = lru_conv_w.shape[-1]
    kv_w = N_KV_HEADS * HEAD_DIM
    attn_w = N_HEADS * HEAD_DIM
    widths = (2 * conf_w, attn_w, kv_w, kv_w, lru_w, lru_w)
    past = cache_k.shape[2]

    ctx_tiles = n_ctx // TOKEN_TILE
    lat_tiles_per_seq = dec_seq // TOKEN_TILE
    mod_row = lambda i: jnp.where(i < ctx_tiles, 0, 1 + (i - ctx_tiles) // lat_tiles_per_seq)

    n_cond = 8
    cvec = jnp.zeros((n_cond, d), F32).at[0].set(c_ctx).at[1:1 + dec_batch].set(c)
    mod = _modulation(cvec, w_mod, b_mod).reshape(depth, n_cond, 6, d)

    x = (x_prompt.reshape(n_ctx, d), x_sample.reshape(n_lat, d))
    rope = _rope_tables(dec_seq)
    cache_k4 = cache_k.reshape(dec_batch, depth, past, kv_w)
    cache_v4 = cache_v.reshape(dec_batch, depth, past, kv_w)
    h0_ctx = jnp.zeros((batch, 2, lru_w), F32)
    assert d == SUBLANES * LANES, "row tables are moved as one (8, 128) tile per token"
    n_sorted_tiles = n * TOP_K // EXPERT_TILE + N_EXPERTS

    new_k, new_v, new_h = [], [], []
    for l in range(depth):
        p = {
            'conf_conv_w': conf_conv_w[l], 'conf_conv_b': conf_conv_b[l].reshape(1, conf_w),
            'conf_ln_w': conf_ln_w[l].reshape(1, conf_w), 'conf_ln_b': conf_ln_b[l].reshape(1, conf_w),
            'lru_conv_w': lru_conv_w[l], 'lru_conv_b': lru_conv_b[l].reshape(1, lru_w),
            'lru_wg': jnp.concatenate([_block_diag(lru_wa[l]), _block_diag(lru_wx[l])], axis=-1).astype(BF16),
            'lru_bg': jnp.concatenate([lru_ba[l].reshape(-1), lru_bx[l].reshape(-1)]).reshape(1, 4 * lru_w),
            'lru_lam': lru_lambda[l].reshape(1, 2 * lru_w),
        }
        conf_u, q, k, v, lru_x, lru_g = _pre_mixer(x, mod[l], mod_row, norm_mix_w[l], w_in[l], widths)

        conf_c, lru_c, h_last = _seq_mixers(seq, batch, 0, conf_u, lru_x, lru_g, h0_ctx, p)
        conf_l, lru_l, _ = _seq_mixers(dec_seq, dec_batch, n_ctx, conf_u, lru_x, lru_g, state_lru[:, l], p)

        qw = q_norm_w[l].reshape(1, HEAD_DIM)
        kw = k_norm_w[l].reshape(1, HEAD_DIM)
        attn_c, k_ctx = _attention_ctx(seq, batch, q, k, v, qw, kw)
        attn_l = _attention_lat(dec_seq, dec_batch, n_ctx, l, q, k, v, cache_k4, cache_v4,
                                jnp.tile(qw, (1, N_HEADS)), jnp.tile(kw, (1, N_KV_HEADS)), rope)
        new_k.append(k_ctx.reshape(batch, seq, N_KV_HEADS, HEAD_DIM))
        new_v.append(v[:n_ctx].reshape(batch, seq, N_KV_HEADS, HEAD_DIM))
        new_h.append(h_last)

        x1, h2, topi, gates, rank, counts = _post_mixer(
            x, (conf_c, attn_c, lru_c), (conf_l, attn_l, lru_l), mod[l], mod_row, w_out[l],
            norm_ffn_w[l], router_w[l], router_b[l])
        topi, rank = topi.reshape(-1), rank.reshape(-1)
        offs, tile_end, tile_expert, next_expert, parity, n_used = _routing_tables(counts, n_sorted_tiles)
        xs = _dispatch(h2, topi, rank, tile_end, offs, n_sorted_tiles)
        ys = _experts(l, tile_expert, next_expert, parity, n_used, xs, w_gu, b_gu, w_down, b_down)
        x = tuple(_combine(l == depth - 1, row0, rows, x1, gates, mod[l], mod_row, final_norm_w, topi, rank,
                           offs, ys)
                  for row0, rows in ((0, n_ctx), (n_ctx, n_lat)))

    y_prompt = x[0].reshape(batch, seq, d)
    y_sample = x[1].reshape(dec_batch, dec_seq, d)
    return (y_prompt, y_sample, jnp.stack(new_k, axis=1), jnp.stack(new_v, axis=1), jnp.stack(new_h, axis=1))
```

```python
import functools

import jax
import jax.numpy as jnp
import numpy as np
from jax import lax
from jax.experimental import pallas as pl
from jax.experimental.pallas import tpu as pltpu

F32 = jnp.float32
BF16 = jnp.bfloat16

HEAD_DIM = 64
N_HEADS = 8
N_KV_HEADS = 2
HEADS_PER_KV = N_HEADS // N_KV_HEADS
CONF_KERNEL = 31
LRU_CONV = 4
LRU_C = 8.0
N_EXPERTS = 32
TOP_K = 4
SWIGLU_ALPHA = 1.702
SWIGLU_LIMIT = 7.0
ROPE_THETA = 10000.0
GRID_W = 64
EPS = 1e-6

LANES = 128
SUBLANES = 8
ISSUE_UNROLL = 4
TOKEN_TILE = 256
EXPERT_TILE = 256
CONV_HALO = 16
ROW_CHUNK = 64
SCAN_CHUNK = 32
VMEM_LIMIT = 56 * 1024 * 1024


def _params(sem, vmem=None):
    return pltpu.CompilerParams(dimension_semantics=sem, vmem_limit_bytes=vmem)


def _split_bf16(x):
    hi = x.astype(BF16)
    lo = (x - hi.astype(F32)).astype(BF16)
    return hi, lo


def _dot(a, b):
    return jnp.dot(a, b, preferred_element_type=F32)


def _dot3(a, b):
    a_hi, a_lo = _split_bf16(a)
    b_hi, b_lo = _split_bf16(b)
    return _dot(a_hi, b_hi) + (_dot(a_hi, b_lo) + _dot(a_lo, b_hi))


def _mod_kernel(c_ref, w_ref, b_ref, o_ref):
    c = c_ref[...]
    s = c * jax.nn.sigmoid(c)
    o_ref[...] = _dot3(s, w_ref[...]) + b_ref[...]


def _modulation(cvec, w_mod, b_mod):
    depth, d, d6 = w_mod.shape
    tn = 768
    return pl.pallas_call(
        _mod_kernel,
        out_shape=jax.ShapeDtypeStruct((depth, cvec.shape[0], d6), F32),
        grid=(depth, d6 // tn),
        in_specs=[
            pl.BlockSpec(cvec.shape, lambda l, j: (0, 0)),
            pl.BlockSpec((None, d, tn), lambda l, j: (l, 0, j)),
            pl.BlockSpec((None, 1, tn), lambda l, j: (l, 0, j)),
        ],
        out_specs=pl.BlockSpec((None, cvec.shape[0], tn), lambda l, j: (l, 0, j)),
        compiler_params=_params(("arbitrary", "arbitrary")),
        name="modulation",
    )(cvec, w_mod, b_mod.reshape(depth, 1, d6))


def _pre_kernel(ctx_tiles, xc_ref, xl_ref, mod_ref, nw_ref, win_f32_ref, conf_ref, q_ref, k_ref, v_ref, lx_ref,
                lg_ref, win_ref):
    @pl.when(pl.program_id(0) == 0)
    def _():
        win_ref[...] = win_f32_ref[...].astype(BF16)

    x = jnp.where(pl.program_id(0) < ctx_tiles, xc_ref[...], xl_ref[...])
    m = mod_ref[...]
    shift, scale = m[0:1], m[1:2]
    h = x * lax.rsqrt(jnp.mean(x * x, axis=-1, keepdims=True) + EPS) * nw_ref[...]
    h = h * (1.0 + scale) + shift
    proj = _dot(h.astype(BF16), win_ref[...])
    col = 0
    for ref in (conf_ref, q_ref, k_ref, v_ref, lx_ref, lg_ref):
        w = ref.shape[-1]
        ref[...] = proj[:, col:col + w]
        col += w


def _path_specs(x_pair, tm):
    ctx_tiles = x_pair[0].shape[0] // tm
    return ctx_tiles, [
        pl.BlockSpec((tm, x_pair[0].shape[-1]), lambda i: (jnp.minimum(i, ctx_tiles - 1), 0)),
        pl.BlockSpec((tm, x_pair[1].shape[-1]), lambda i: (jnp.maximum(i - ctx_tiles, 0), 0))]


def _pre_mixer(x_pair, mod_l, mod_row, norm_w, w_in, widths):
    n, d = x_pair[0].shape[0] + x_pair[1].shape[0], x_pair[0].shape[1]
    ctx_tiles, x_specs = _path_specs(x_pair, TOKEN_TILE)
    return pl.pallas_call(
        functools.partial(_pre_kernel, ctx_tiles),
        out_shape=[jax.ShapeDtypeStruct((n, w), F32) for w in widths],
        grid=(n // TOKEN_TILE,),
        in_specs=x_specs + [
            pl.BlockSpec((None, 6, d), lambda i: (mod_row(i), 0, 0)),
            pl.BlockSpec((1, d), lambda i: (0, 0)),
            pl.BlockSpec(w_in.shape, lambda i: (0, 0)),
        ],
        out_specs=[pl.BlockSpec((TOKEN_TILE, w), lambda i: (i, 0)) for w in widths],
        scratch_shapes=[pltpu.VMEM(w_in.shape, BF16)],
        compiler_params=_params(("arbitrary",), VMEM_LIMIT),
        name="pre_mixer",
    )(*x_pair, mod_l, norm_w.reshape(1, d), w_in)


def _gelu_tanh(x):
    return 0.5 * x * (1.0 + jnp.tanh(0.7978845608028654 * (x + 0.044715 * (x * x * x))))


def _seq_kernel(seq_len, conf_ref, lx_ref, lg_ref, h0_ref, ccw_ref, ccb_ref, lnw_ref, lnb_ref,
                lcw_ref, lcb_ref, wg_ref, bg_ref, lam_ref,
                co_ref, lo_ref, hl_ref, pad_ref, rot_ref, af_ref, uf_ref, ab_ref, ub_ref):
    L = seq_len
    W = co_ref.shape[-1]
    pad = L // 2
    zeros_halo = jnp.zeros((CONV_HALO, W), F32)

    pad_ref[0:CONV_HALO, :] = zeros_halo
    pad_ref[CONV_HALO + L:2 * CONV_HALO + L, :] = zeros_halo
    for c in range(L // ROW_CHUNK):
        r0 = c * ROW_CHUNK
        u = conf_ref[r0:r0 + ROW_CHUNK, :]
        pad_ref[CONV_HALO + r0:CONV_HALO + r0 + ROW_CHUNK, :] = u[:, :W] * jax.nn.sigmoid(u[:, W:])
    shifted_rows = L + 2 * CONV_HALO - SUBLANES
    for s in range(1, SUBLANES):
        for r0 in range(0, shifted_rows, ROW_CHUNK):
            rows = min(ROW_CHUNK, shifted_rows - r0)
            rot_ref[s - 1, r0:r0 + rows, :] = pad_ref[r0 + s:r0 + s + rows, :]
    left = CONF_KERNEL // 2
    for c in range(L // ROW_CHUNK):
        r0 = c * ROW_CHUNK
        acc = jnp.zeros((ROW_CHUNK, W), F32) + ccb_ref[...]
        for k in range(CONF_KERNEL):
            whole, s = divmod(CONV_HALO - left + k, SUBLANES)
            start = whole * SUBLANES + r0
            src = pad_ref if s == 0 else rot_ref.at[s - 1]
            acc = acc + ccw_ref[k:k + 1, :] * src[start:start + ROW_CHUNK, :]
        mu = jnp.mean(acc, axis=-1, keepdims=True)
        cen = acc - mu
        var = jnp.mean(cen * cen, axis=-1, keepdims=True)
        y = cen * lax.rsqrt(var + EPS) * lnw_ref[...] + lnb_ref[...]
        co_ref[r0:r0 + ROW_CHUNK, :] = y * jax.nn.sigmoid(y)

    for c in range(L // ROW_CHUNK):
        r0 = c * ROW_CHUNK
        pad_ref[CONV_HALO + r0:CONV_HALO + r0 + ROW_CHUNK, :] = lx_ref[r0:r0 + ROW_CHUNK, :]
    ones_pad = jnp.ones((pad, W), F32)
    zeros_pad = jnp.zeros((pad, W), F32)
    af_ref[0:pad, :] = ones_pad
    uf_ref[0:pad, :] = zeros_pad
    ab_ref[L:L + pad, :] = ones_pad
    ub_ref[L:L + pad, :] = zeros_pad
    sp = jax.nn.softplus(-lam_ref[...])
    h0 = h0_ref[...]
    left = LRU_CONV // 2
    n_chunks = L // ROW_CHUNK
    for c in range(n_chunks):
        r0 = c * ROW_CHUNK
        xc = jnp.zeros((ROW_CHUNK, W), F32) + lcb_ref[...]
        for k in range(LRU_CONV):
            start = CONV_HALO - left + k + r0
            xc = xc + lcw_ref[k:k + 1, :] * pad_ref[start:start + ROW_CHUNK, :]
        g = _dot(xc.astype(BF16), wg_ref[...]) + bg_ref[...]
        r = jax.nn.sigmoid(g[:, :2 * W])
        i = jax.nn.sigmoid(g[:, 2 * W:])
        a = jnp.exp((-LRU_C) * r * sp)
        xc2 = jnp.concatenate([xc, xc], axis=-1)
        u = jnp.sqrt(1.0 - a * a) * i * xc2
        a_f, a_b, u_f, u_b = a[:, :W], a[:, W:], u[:, :W], u[:, W:]
        row = lax.broadcasted_iota(jnp.int32, (ROW_CHUNK, W), 0)
        if c == 0:
            u_f = jnp.where(row == 0, u_f + a_f * h0[0:1], u_f)
        if c == n_chunks - 1:
            u_b = jnp.where(row == ROW_CHUNK - 1, u_b + a_b * h0[1:2], u_b)
        af_ref[pad + r0:pad + r0 + ROW_CHUNK, :] = a_f
        uf_ref[pad + r0:pad + r0 + ROW_CHUNK, :] = u_f
        ab_ref[r0:r0 + ROW_CHUNK, :] = a_b
        ub_ref[r0:r0 + ROW_CHUNK, :] = u_b

    n_sc = L // SCAN_CHUNK
    s = 1
    while s < L:
        for c in reversed(range(n_sc)):
            r0 = c * SCAN_CHUNK
            if r0 + SCAN_CHUNK <= s:
                continue
            cur = slice(pad + r0, pad + r0 + SCAN_CHUNK)
            sh = slice(pad + r0 - s, pad + r0 - s + SCAN_CHUNK)
            a_cur = af_ref[cur, :]
            uf_ref[cur, :] = uf_ref[cur, :] + a_cur * uf_ref[sh, :]
            af_ref[cur, :] = a_cur * af_ref[sh, :]
        for c in range(n_sc):
            r0 = c * SCAN_CHUNK
            if r0 >= L - s:
                continue
            cur = slice(r0, r0 + SCAN_CHUNK)
            sh = slice(r0 + s, r0 + s + SCAN_CHUNK)
            a_cur = ab_ref[cur, :]
            ub_ref[cur, :] = ub_ref[cur, :] + a_cur * ub_ref[sh, :]
            ab_ref[cur, :] = a_cur * ab_ref[sh, :]
        s *= 2

    for c in range(n_chunks):
        r0 = c * ROW_CHUNK
        h = uf_ref[pad + r0:pad + r0 + ROW_CHUNK, :] + ub_ref[r0:r0 + ROW_CHUNK, :]
        lo_ref[r0:r0 + ROW_CHUNK, :] = h * _gelu_tanh(lg_ref[r0:r0 + ROW_CHUNK, :])
    hl_ref[0:1, :] = uf_ref[pad + L - 1:pad + L, :]
    hl_ref[1:2, :] = ub_ref[0:1, :]


def _seq_mixers(seq_len, n_seq, row0, conf_u, lru_x, lru_g, h0, p):
    w = lru_x.shape[-1]
    b0 = row0 // seq_len
    pad = seq_len // 2
    in_spec = lambda width: pl.BlockSpec((seq_len, width), lambda i: (b0 + i, 0))
    out_spec = pl.BlockSpec((seq_len, w), lambda i: (i, 0))
    state_spec = pl.BlockSpec((None, 2, w), lambda i: (i, 0, 0))
    full = lambda a: pl.BlockSpec(a.shape, lambda i: (0,) * a.ndim)
    weights = (p['conf_conv_w'], p['conf_conv_b'], p['conf_ln_w'], p['conf_ln_b'],
               p['lru_conv_w'], p['lru_conv_b'], p['lru_wg'], p['lru_bg'], p['lru_lam'])
    padded = seq_len + 2 * CONV_HALO
    return pl.pallas_call(
        functools.partial(_seq_kernel, seq_len),
        out_shape=[jax.ShapeDtypeStruct((n_seq * seq_len, w), F32)] * 2
        + [jax.ShapeDtypeStruct((n_seq, 2, w), F32)],
        grid=(n_seq,),
        in_specs=[in_spec(2 * w), in_spec(w), in_spec(w), state_spec] + [full(a) for a in weights],
        out_specs=[out_spec, out_spec, state_spec],
        scratch_shapes=[pltpu.VMEM((padded, w), F32), pltpu.VMEM((SUBLANES - 1, padded, w), F32)]
        + [pltpu.VMEM((seq_len + pad, w), F32)] * 4,
        compiler_params=_params(("arbitrary",), VMEM_LIMIT),
        name=f"seq_mixers_{seq_len}",
    )(conf_u, lru_x, lru_g, h0, *weights)


def _head_rms(x):
    return lax.rsqrt(jnp.mean(x * x, axis=-1, keepdims=True) + EPS)


def _swap_halves(x):
    width = x.shape[-1]
    lane = lax.broadcasted_iota(jnp.int32, x.shape, x.ndim - 1)
    up = pltpu.roll(x, width - HEAD_DIM // 4, x.ndim - 1)
    down = pltpu.roll(x, HEAD_DIM // 4, x.ndim - 1)
    return jnp.where((lane % (HEAD_DIM // 2)) < HEAD_DIM // 4, up, down)


def _attend(q_heads, k_bf16, v_bf16):
    scale = HEAD_DIM ** -0.5
    s = lax.dot_general(q_heads.astype(BF16), k_bf16, (((1,), (1,)), ((), ())),
                        preferred_element_type=F32) * scale
    p = jnp.exp(s - jnp.max(s, axis=-1, keepdims=True))
    denom = jnp.sum(p, axis=-1, keepdims=True)
    return _dot(p.astype(BF16), v_bf16) / denom


def _attn_ctx_kernel(q_ref, k_ref, v_ref, qw_ref, kw_ref, o_ref, kn_ref):
    L = q_ref.shape[0]
    q, k, v = q_ref[...], k_ref[...], v_ref[...]
    qw, kw = qw_ref[...], kw_ref[...]
    k_out, o_out = [], []
    for g in range(N_KV_HEADS):
        kh = k[:, g * HEAD_DIM:(g + 1) * HEAD_DIM]
        kh = kh * _head_rms(kh) * kw
        k_out.append(kh)
        qs = []
        for j in range(HEADS_PER_KV):
            h = g * HEADS_PER_KV + j
            qh = q[:, h * HEAD_DIM:(h + 1) * HEAD_DIM]
            qs.append(qh * _head_rms(qh) * qw)
        o = _attend(jnp.concatenate(qs, axis=0), kh.astype(BF16),
                    v[:, g * HEAD_DIM:(g + 1) * HEAD_DIM].astype(BF16))
        o_out += [o[j * L:(j + 1) * L] for j in range(HEADS_PER_KV)]
    kn_ref[...] = jnp.concatenate(k_out, axis=-1)
    o_ref[...] = jnp.concatenate(o_out, axis=-1)


def _attention_ctx(seq_len, n_seq, q, k, v, qw, kw):
    kvw = k.shape[-1]
    row_spec = lambda width: pl.BlockSpec((seq_len, width), lambda i: (i, 0))
    full = lambda a: pl.BlockSpec(a.shape, lambda i: (0,) * a.ndim)
    return pl.pallas_call(
        _attn_ctx_kernel,
        out_shape=[jax.ShapeDtypeStruct((n_seq * seq_len, q.shape[-1]), F32),
                   jax.ShapeDtypeStruct((n_seq * seq_len, kvw), F32)],
        grid=(n_seq,),
        in_specs=[row_spec(q.shape[-1]), row_spec(kvw), row_spec(kvw), full(qw), full(kw)],
        out_specs=[row_spec(q.shape[-1]), row_spec(kvw)],
        compiler_params=_params(("arbitrary",), VMEM_LIMIT),
        name="attention_ctx",
    )(q, k, v, qw, kw)


def _attn_lat_kernel(q_ref, k_ref, v_ref, ck_ref, cv_ref, qw_ref, kw_ref, cq_ref, sq_ref, ck_t_ref,
                     sk_t_ref, o_ref, kall_ref, vall_ref):
    L = k_ref.shape[0]
    tq = q_ref.shape[0]

    @pl.when(pl.program_id(1) == 0)
    def _():
        k = k_ref[...]
        t = k * kw_ref[...]
        rot = t * ck_t_ref[...] + _swap_halves(t) * sk_t_ref[...]
        parts = []
        for g in range(N_KV_HEADS):
            sl = slice(g * HEAD_DIM, (g + 1) * HEAD_DIM)
            parts.append(rot[:, sl] * _head_rms(k[:, sl]))
        kall_ref[0:L, :] = jnp.concatenate(parts, axis=-1).astype(BF16)
        kall_ref[L:, :] = ck_ref[...].astype(BF16)
        vall_ref[0:L, :] = v_ref[...].astype(BF16)
        vall_ref[L:, :] = cv_ref[...].astype(BF16)

    q = q_ref[...]
    t = q * qw_ref[...]
    rot = t * cq_ref[...] + _swap_halves(t) * sq_ref[...]
    o_out = []
    for g in range(N_KV_HEADS):
        qs = []
        for j in range(HEADS_PER_KV):
            sl = slice((g * HEADS_PER_KV + j) * HEAD_DIM, (g * HEADS_PER_KV + j + 1) * HEAD_DIM)
            qs.append(rot[:, sl] * _head_rms(q[:, sl]))
        sl = slice(g * HEAD_DIM, (g + 1) * HEAD_DIM)
        o = _attend(jnp.concatenate(qs, axis=0), kall_ref[:, sl], vall_ref[:, sl])
        o_out += [o[j * tq:(j + 1) * tq] for j in range(HEADS_PER_KV)]
    o_ref[...] = jnp.concatenate(o_out, axis=-1)


def _attention_lat(seq_len, n_seq, row0, layer, q, k, v, cache_k, cache_v, qw8, kw2, rope):
    tq = TOKEN_TILE
    nq = seq_len // tq
    qwid, kvw = q.shape[-1], k.shape[-1]
    past = cache_k.shape[2]
    b0q = row0 // tq
    b0s = row0 // seq_len
    cq, sq, ck, sk = rope
    full = lambda a: pl.BlockSpec(a.shape, lambda b, j: (0,) * a.ndim)
    seq_spec = pl.BlockSpec((seq_len, kvw), lambda b, j: (b0s + b, 0))
    cache_spec = pl.BlockSpec((None, None, past, kvw), lambda b, j: (b, layer, 0, 0))
    q_spec = pl.BlockSpec((tq, qwid), lambda b, j: (b0q + b * nq + j, 0))
    rope_q_spec = pl.BlockSpec((tq, qwid), lambda b, j: (j, 0))
    return pl.pallas_call(
        _attn_lat_kernel,
        out_shape=jax.ShapeDtypeStruct((n_seq * seq_len, qwid), F32),
        grid=(n_seq, nq),
        in_specs=[q_spec, seq_spec, seq_spec, cache_spec, cache_spec, full(qw8), full(kw2),
                  rope_q_spec, rope_q_spec, full(ck), full(sk)],
        out_specs=pl.BlockSpec((tq, qwid), lambda b, j: (b * nq + j, 0)),
        scratch_shapes=[pltpu.VMEM((seq_len + past, kvw), BF16)] * 2,
        compiler_params=_params(("arbitrary", "arbitrary"), VMEM_LIMIT),
        name="attention_lat",
    )(q, k, v, cache_k, cache_v, qw8, kw2, cq, sq, ck, sk)


def _rope_tables(seq_len):
    t = np.arange(seq_len)
    row = (t // GRID_W).astype(np.float32)
    col = (t % GRID_W).astype(np.float32)
    half = HEAD_DIM // 2
    freqs = (np.float32(ROPE_THETA) ** (-np.arange(0, half, 2, dtype=np.float32) / np.float32(half)))
    ang_r, ang_c = row[:, None] * freqs, col[:, None] * freqs
    cos = np.concatenate([np.cos(ang_r)] * 2 + [np.cos(ang_c)] * 2, axis=-1)
    sin = np.concatenate([-np.sin(ang_r), np.sin(ang_r), -np.sin(ang_c), np.sin(ang_c)], axis=-1)
    return tuple(jnp.asarray(a, F32) for a in
                 (np.tile(cos, (1, N_HEADS)), np.tile(sin, (1, N_HEADS)),
                  np.tile(cos, (1, N_KV_HEADS)), np.tile(sin, (1, N_KV_HEADS))))


def _slab_copy(src_ref, src_tok, dst_ref, dst_tok, sem):
    src = src_ref.at[pl.ds(pl.multiple_of(src_tok * SUBLANES, SUBLANES), SUBLANES)]
    dst = dst_ref.at[pl.ds(pl.multiple_of(dst_tok * SUBLANES, SUBLANES), SUBLANES)]
    return pltpu.make_async_copy(src, dst, sem)


def _load_slabs(ref, n_tok, lead=()):
    return jnp.concatenate(
        [ref[(*lead, pl.ds(j, n_tok, stride=SUBLANES), slice(None))] for j in range(SUBLANES)], axis=-1)


def _store_slabs(ref, val):
    for j in range(SUBLANES):
        ref[pl.ds(j, val.shape[0], stride=SUBLANES), :] = val[:, j * LANES:(j + 1) * LANES]


def _wait_slabs(hbm_ref, n_tok, sem):
    span = hbm_ref.at[pl.ds(0, n_tok * SUBLANES)]
    pltpu.make_async_copy(span, span, sem).wait()


def _fetch_positions(pos_ref, idx_ref, sem, tile):
    n = idx_ref.shape[0]
    copy = pltpu.make_async_copy(pos_ref.at[pl.ds(tile * n, n)], idx_ref, sem)
    copy.start()
    copy.wait()


def _choice_major(x):
    t = jnp.transpose(x)
    return jnp.concatenate([t[k:k + 1, :] for k in range(TOP_K)], axis=1)


def _post_kernel(ctx_tiles, xc_ref, xl_ref, co_c_ref, ao_c_ref, lo_c_ref, co_l_ref, ao_l_ref, lo_l_ref, mod_ref,
                 wout_f32_ref, nw_ref, rw_ref, rb_ref,
                 x1_ref, h2_ref, topi_ref, gates_ref, rank_ref, counts_ref, carry_ref, wout_ref):
    tm = xc_ref.shape[0]
    cw, aw = co_c_ref.shape[-1], ao_c_ref.shape[-1]
    is_ctx = pl.program_id(0) < ctx_tiles

    @pl.when(pl.program_id(0) == 0)
    def _():
        carry_ref[...] = jnp.zeros_like(carry_ref)
        wout_ref[...] = wout_f32_ref[...].astype(BF16)

    m = mod_ref[...]
    gate1, shift2, scale2 = m[2:3], m[3:4], m[4:5]
    pick = lambda c_ref, l_ref: jnp.where(is_ctx, c_ref[...], l_ref[...]).astype(BF16)
    x = jnp.where(is_ctx, xc_ref[...], xl_ref[...])
    mixed = (_dot(pick(co_c_ref, co_l_ref), wout_ref[0:cw, :])
             + _dot(pick(ao_c_ref, ao_l_ref), wout_ref[cw:cw + aw, :])
             + _dot(pick(lo_c_ref, lo_l_ref), wout_ref[cw + aw:, :]))
    x1 = x + gate1 * mixed
    x1_ref[...] = x1
    h2 = x1 * lax.rsqrt(jnp.mean(x1 * x1, axis=-1, keepdims=True) + EPS) * nw_ref[...]
    h2 = h2 * (1.0 + scale2) + shift2
    _store_slabs(h2_ref, h2)

    lane = lax.broadcasted_iota(jnp.int32, (tm, LANES), 1)
    lane_f = lane.astype(F32)
    logits = jnp.where(lane < N_EXPERTS, _dot3(h2, rw_ref[...]) + rb_ref[...], -jnp.inf)
    top_v, onehots = [], []
    topi = jnp.zeros((tm, LANES), F32)
    for k in range(TOP_K):
        mx = jnp.max(logits, axis=-1, keepdims=True)
        idx = jnp.min(jnp.where(logits == mx, lane_f, float(LANES)), axis=-1, keepdims=True)
        hit = lane_f == idx
        logits = jnp.where(hit, -jnp.inf, logits)
        top_v.append(mx)
        onehots.append(hit)
        topi = jnp.where(lane == k, idx, topi)
    topi_ref[...] = _choice_major(topi).astype(jnp.int32)
    exps = [jnp.exp(v - top_v[0]) for v in top_v]
    denom = exps[0] + exps[1] + exps[2] + exps[3]
    gates = jnp.zeros((tm, LANES), F32)
    for k in range(TOP_K):
        gates = jnp.where(lane == k, exps[k] / denom, gates)
    gates_ref[...] = gates

    chosen = jnp.zeros((tm, LANES), F32)
    for hit in onehots:
        chosen = jnp.where(hit, 1.0, chosen)
    r_i = lax.broadcasted_iota(jnp.int32, (tm, tm), 0)
    c_i = lax.broadcasted_iota(jnp.int32, (tm, tm), 1)
    lower = jnp.where(c_i < r_i, 1.0, 0.0).astype(BF16)
    before = _dot(lower, chosen.astype(BF16)) + carry_ref[...]
    rank = jnp.zeros((tm, LANES), F32)
    for k, hit in enumerate(onehots):
        rk = jnp.sum(jnp.where(hit, before, 0.0), axis=-1, keepdims=True)
        rank = jnp.where(lane == k, rk, rank)
    rank_ref[...] = _choice_major(rank).astype(jnp.int32)
    carry = carry_ref[...] + jnp.sum(chosen, axis=0, keepdims=True)
    carry_ref[...] = carry
    counts_ref[...] = carry


def _post_mixer(x_pair, ctx_outs, lat_outs, mod_l, mod_row, w_out, norm_w, router_w, router_b):
    n, d = x_pair[0].shape[0] + x_pair[1].shape[0], x_pair[0].shape[1]
    tm = TOKEN_TILE
    ctx_tiles, x_specs = _path_specs(x_pair, tm)
    row_spec = lambda width: pl.BlockSpec((tm, width), lambda i: (i, 0))
    ctx_spec = lambda a: pl.BlockSpec((tm, a.shape[-1]), lambda i: (jnp.minimum(i, ctx_tiles - 1), 0))
    lat_spec = lambda a: pl.BlockSpec((tm, a.shape[-1]), lambda i: (jnp.maximum(i - ctx_tiles, 0), 0))
    full = lambda a: pl.BlockSpec(a.shape, lambda i: (0,) * a.ndim)
    rw = jnp.pad(router_w, ((0, 0), (0, LANES - N_EXPERTS)))
    rb = jnp.pad(router_b, (0, LANES - N_EXPERTS)).reshape(1, LANES)
    nw = norm_w.reshape(1, d)
    lane_tile = pl.BlockSpec((tm, LANES), lambda i: (i, 0))
    flat_tile = pl.BlockSpec((None, 1, TOP_K * tm), lambda i: (i, 0, 0))
    flat_shape = jax.ShapeDtypeStruct((n // tm, 1, TOP_K * tm), jnp.int32)
    return pl.pallas_call(
        functools.partial(_post_kernel, ctx_tiles),
        out_shape=[jax.ShapeDtypeStruct((n, d), F32), jax.ShapeDtypeStruct((n * SUBLANES, LANES), F32),
                   flat_shape, jax.ShapeDtypeStruct((n, LANES), F32),
                   flat_shape, jax.ShapeDtypeStruct((1, LANES), F32)],
        grid=(n // tm,),
        in_specs=x_specs + [ctx_spec(a) for a in ctx_outs] + [lat_spec(a) for a in lat_outs]
        + [pl.BlockSpec((None, 6, d), lambda i: (mod_row(i), 0, 0)),
           full(w_out), full(nw), full(rw), full(rb)],
        out_specs=[row_spec(d), pl.BlockSpec((tm * SUBLANES, LANES), lambda i: (i, 0)),
                   flat_tile, lane_tile, flat_tile,
                   pl.BlockSpec((1, LANES), lambda i: (0, 0))],
        scratch_shapes=[pltpu.VMEM((1, LANES), F32), pltpu.VMEM(w_out.shape, BF16)],
        compiler_params=_params(("arbitrary",), VMEM_LIMIT),
        name="post_mixer",
    )(*x_pair, *ctx_outs, *lat_outs, mod_l, w_out, nw, rw, rb)


def _pos_kernel(offs_ref, topi_ref, rank_ref, pos_ref):
    topi = topi_ref[...]
    pos = rank_ref[...]
    for e in range(N_EXPERTS):
        pos = pos + jnp.where(topi == e, offs_ref[e], 0)
    pos_ref[...] = pos


def _positions(topi_flat, rank_flat, offs):
    tiles, _, per_tile = topi_flat.shape
    full = pl.BlockSpec((tiles, per_tile), lambda i, offs: (0, 0))
    pos = pl.pallas_call(
        _pos_kernel,
        out_shape=jax.ShapeDtypeStruct((tiles, per_tile), jnp.int32),
        grid_spec=pltpu.PrefetchScalarGridSpec(num_scalar_prefetch=1, grid=(1,), in_specs=[full, full],
                                               out_specs=full),
        compiler_params=_params(("arbitrary",)),
        name="moe_positions",
    )(offs, topi_flat.reshape(tiles, per_tile), rank_flat.reshape(tiles, per_tile))
    return pos.reshape(-1)


def _dispatch_kernel(tend_ref, h2_ref, pos_ref, xs_ref, idx_ref, zero_ref, idx_sem, row_sem):
    tm = h2_ref.shape[0] // SUBLANES
    tile_rows = EXPERT_TILE * SUBLANES

    @pl.when(pl.program_id(0) == 0)
    def _():
        zero_ref[...] = jnp.zeros_like(zero_ref)

        def last_tile_copy(e):
            start = pl.multiple_of((tend_ref[e] - 1) * tile_rows, tile_rows)
            return pltpu.make_async_copy(zero_ref, xs_ref.at[pl.ds(start, tile_rows)], row_sem)

        def has_tiles(e):
            return tend_ref[e] > (tend_ref[e - 1] if e else 0)

        for e in range(N_EXPERTS):
            pl.when(has_tiles(e))(lambda e=e: last_tile_copy(e).start())
        for e in range(N_EXPERTS):
            pl.when(has_tiles(e))(lambda e=e: last_tile_copy(e).wait())

        def spare_tile_copy(t):
            return pltpu.make_async_copy(
                zero_ref, xs_ref.at[pl.ds(pl.multiple_of(t * tile_rows, tile_rows), tile_rows)], row_sem)

        n_used, n_tiles = tend_ref[N_EXPERTS - 1], xs_ref.shape[0] // tile_rows
        lax.fori_loop(n_used, n_tiles, lambda t, c: (spare_tile_copy(t).start(), c)[1], 0)
        lax.fori_loop(n_used, n_tiles, lambda t, c: (spare_tile_copy(t).wait(), c)[1], 0)

    _fetch_positions(pos_ref, idx_ref, idx_sem, pl.program_id(0))

    def issue(r, carry):
        for k in range(TOP_K):
            _slab_copy(h2_ref, r, xs_ref, idx_ref[k * tm + r], row_sem).start(priority=k % 2)
        return carry

    lax.fori_loop(0, tm, issue, 0, unroll=ISSUE_UNROLL)
    _wait_slabs(xs_ref, tm * TOP_K, row_sem)


def _dispatch(h2_slabs, pos_flat, tile_end, n_tiles):
    tm = TOKEN_TILE
    n = h2_slabs.shape[0] // SUBLANES
    hbm = pl.BlockSpec(memory_space=pl.ANY)
    grid_spec = pltpu.PrefetchScalarGridSpec(
        num_scalar_prefetch=1,
        grid=(n // tm,),
        in_specs=[pl.BlockSpec((tm * SUBLANES, LANES), lambda i, tend: (i, 0)), hbm],
        out_specs=hbm,
        scratch_shapes=[pltpu.SMEM((tm * TOP_K,), jnp.int32),
                        pltpu.VMEM((EXPERT_TILE * SUBLANES, LANES), F32),
                        pltpu.SemaphoreType.DMA, pltpu.SemaphoreType.DMA],
    )
    return pl.pallas_call(
        _dispatch_kernel,
        out_shape=jax.ShapeDtypeStruct((n_tiles * EXPERT_TILE * SUBLANES, LANES), F32),
        grid_spec=grid_spec,
        compiler_params=_params(("arbitrary",)),
        name="moe_dispatch",
    )(tile_end, h2_slabs, pos_flat)


def _expert_kernel(layer, te_ref, nxt_ref, par_ref, nu_ref, xs_ref, wgu_hbm, bgu_ref, wd_hbm, bd_ref,
                   ys_ref, wgu_buf, wd_buf, wgu_bf, wd_bf, w_sem):
    i = pl.program_id(0)
    dff = wd_hbm.shape[-2]

    def weight_copies(e, slot):
        return (pltpu.make_async_copy(wgu_hbm.at[layer, e], wgu_buf.at[slot], w_sem.at[slot]),
                pltpu.make_async_copy(wd_hbm.at[layer, e], wd_buf.at[slot], w_sem.at[2 + slot]))

    @pl.when(i < nu_ref[0])
    def _():
        prev = te_ref[jnp.maximum(i - 1, 0)]

        @pl.when((i == 0) | (te_ref[i] != prev))
        def _():
            slot = par_ref[i]

            @pl.when(i == 0)
            def _():
                for copy in weight_copies(te_ref[0], slot):
                    copy.start()

            for copy in weight_copies(te_ref[i], slot):
                copy.wait()

            @pl.when(nxt_ref[i] >= 0)
            def _():
                for copy in weight_copies(nxt_ref[i], 1 - slot):
                    copy.start()

            wgu_bf[...] = wgu_buf[slot].astype(BF16)
            wd_bf[...] = wd_buf[slot].astype(BF16)

        x = _load_slabs(xs_ref, EXPERT_TILE)
        gu = _dot(x.astype(BF16), wgu_bf[...]) + bgu_ref[...]
        x_glu = jnp.minimum(gu[:, :dff], SWIGLU_LIMIT)
        x_lin = jnp.clip(gu[:, dff:], -SWIGLU_LIMIT, SWIGLU_LIMIT)
        act = x_glu * jax.nn.sigmoid(SWIGLU_ALPHA * x_glu) * (x_lin + 1.0)
        _store_slabs(ys_ref, _dot(act.astype(BF16), wd_bf[...]) + bd_ref[...])

    @pl.when(i >= nu_ref[0])
    def _():
        ys_ref[...] = jnp.zeros_like(ys_ref)


def _experts(layer, tile_expert, next_expert, parity, n_used, xs, w_gu, b_gu, w_down, b_down):
    tm = EXPERT_TILE
    d, dff2 = w_gu.shape[-2:]
    dff = w_down.shape[-2]
    slab_tile = (tm * SUBLANES, LANES)
    hbm = pl.BlockSpec(memory_space=pl.ANY)
    grid_spec = pltpu.PrefetchScalarGridSpec(
        num_scalar_prefetch=4,
        grid=(xs.shape[0] // slab_tile[0],),
        in_specs=[
            pl.BlockSpec(slab_tile, lambda i, te, nx, pa, nu: (jnp.minimum(i, nu[0] - 1), 0)),
            hbm,
            pl.BlockSpec((None, None, 1, dff2), lambda i, te, nx, pa, nu: (layer, te[i], 0, 0)),
            hbm,
            pl.BlockSpec((None, None, 1, d), lambda i, te, nx, pa, nu: (layer, te[i], 0, 0)),
        ],
        out_specs=pl.BlockSpec(slab_tile, lambda i, te, nx, pa, nu: (i, 0)),
        scratch_shapes=[pltpu.VMEM((2, d, dff2), F32), pltpu.VMEM((2, dff, d), F32),
                        pltpu.VMEM((d, dff2), BF16), pltpu.VMEM((dff, d), BF16),
                        pltpu.SemaphoreType.DMA((4,))],
    )
    depth, n_e = w_gu.shape[:2]
    return pl.pallas_call(
        functools.partial(_expert_kernel, layer),
        out_shape=jax.ShapeDtypeStruct(xs.shape, F32),
        grid_spec=grid_spec,
        compiler_params=_params(("arbitrary",), VMEM_LIMIT),
        name="moe_experts",
    )(tile_expert, next_expert, parity, n_used, xs, w_gu, b_gu.reshape(depth, n_e, 1, dff2), w_down,
      b_down.reshape(depth, n_e, 1, d))


def _combine_kernel(final, tile0, x1_ref, gates_ref, mod_ref, fw_ref, pos_ref, ys_ref, out_ref,
                    idx_ref, rows_ref, idx_sem, row_sem):
    tm = x1_ref.shape[0]
    _fetch_positions(pos_ref, idx_ref, idx_sem, pl.program_id(0) + tile0)

    def issue(r, carry):
        for k in range(TOP_K):
            _slab_copy(ys_ref, idx_ref[k * tm + r], rows_ref.at[k], r, row_sem).start(priority=k % 2)
        return carry

    lax.fori_loop(0, tm, issue, 0, unroll=ISSUE_UNROLL)
    _wait_slabs(ys_ref, tm * TOP_K, row_sem)

    gates = gates_ref[...]
    moe = gates[:, 0:1] * _load_slabs(rows_ref, tm, (0,))
    for k in range(1, TOP_K):
        moe = moe + gates[:, k:k + 1] * _load_slabs(rows_ref, tm, (k,))
    x2 = x1_ref[...] + mod_ref[5:6, :] * moe
    if final:
        x2 = x2 * lax.rsqrt(jnp.mean(x2 * x2, axis=-1, keepdims=True) + EPS) * fw_ref[...]
    out_ref[...] = x2


def _combine(final, row0, n_rows, x1, gates, mod_l, mod_row, final_w, pos_flat, ys):
    d = x1.shape[1]
    tm = TOKEN_TILE
    tile0 = row0 // tm
    hbm = pl.BlockSpec(memory_space=pl.ANY)
    return pl.pallas_call(
        functools.partial(_combine_kernel, final, tile0),
        out_shape=jax.ShapeDtypeStruct((n_rows, d), F32),
        grid=(n_rows // tm,),
        in_specs=[pl.BlockSpec((tm, d), lambda i: (tile0 + i, 0)),
                  pl.BlockSpec((tm, LANES), lambda i: (tile0 + i, 0)),
                  pl.BlockSpec((None, 6, d), lambda i: (mod_row(tile0 + i), 0, 0)),
                  pl.BlockSpec((1, d), lambda i: (0, 0)),
                  hbm, hbm],
        out_specs=pl.BlockSpec((tm, d), lambda i: (i, 0)),
        scratch_shapes=[pltpu.SMEM((tm * TOP_K,), jnp.int32),
                        pltpu.VMEM((TOP_K, tm * SUBLANES, LANES), F32),
                        pltpu.SemaphoreType.DMA, pltpu.SemaphoreType.DMA],
        compiler_params=_params(("arbitrary",), VMEM_LIMIT),
        name="moe_combine",
    )(x1, gates, mod_l, final_w.reshape(1, d), pos_flat, ys)


def _routing_tables(counts, n_tiles):
    tm = EXPERT_TILE
    c = counts[0, :N_EXPERTS].astype(jnp.int32)
    tiles = (c + tm - 1) // tm
    tile_end = jnp.cumsum(tiles)
    offs = (tile_end - tiles) * tm
    n_used = tile_end[-1]
    t = jnp.minimum(jnp.arange(n_tiles, dtype=jnp.int32), n_used - 1)
    te = jnp.minimum(jnp.sum((tile_end[None, :] <= t[:, None]).astype(jnp.int32), axis=1), N_EXPERTS - 1)
    e = jnp.arange(N_EXPERTS, dtype=jnp.int32)
    later = (e[None, :] > e[:, None]) & (tiles[None, :] > 0)
    nxt = jnp.min(jnp.where(later, e[None, :], N_EXPERTS), axis=1)
    nxt = jnp.where(nxt == N_EXPERTS, -1, nxt)
    parity = (jnp.cumsum((tiles > 0).astype(jnp.int32)) - 1) % 2
    return (offs, tile_end.astype(jnp.int32), te, nxt[te].astype(jnp.int32),
            parity[te].astype(jnp.int32), n_used.reshape(1).astype(jnp.int32))


def _block_diag(w):
    dirs, heads, blk, _ = w.shape
    eye = jnp.eye(heads, dtype=w.dtype)
    full = jnp.einsum('dhij,hg->hidgj', w, eye)
    return full.reshape(heads * blk, dirs * heads * blk)


def kernel(x_prompt, x_sample, cache_k, cache_v, state_lru, c, c_ctx, w_mod, b_mod, norm_mix_w, w_in, conf_conv_w, conf_conv_b, conf_ln_w, conf_ln_b, q_norm_w, k_norm_w, lru_conv_w, lru_conv_b, lru_wa, lru_ba, lru_wx, lru_bx, lru_lambda, w_out, norm_ffn_w, router_w, router_b, w_gu, b_gu, w_down, b_down, final_norm_w):
    batch, seq, d = x_prompt.shape
    dec_batch, dec_seq, _ = x_sample.shape
    depth = w_mod.shape[0]
    n_ctx, n_lat = batch * seq, dec_batch * dec_seq
    n = n_ctx + n_lat
    conf_w = conf_conv_w.shape[-1]
    lru_w = lru_conv_w.shape[-1]
    kv_w = N_KV_HEADS * HEAD_DIM
    attn_w = N_HEADS * HEAD_DIM
    widths = (2 * conf_w, attn_w, kv_w, kv_w, lru_w, lru_w)
    past = cache_k.shape[2]

    ctx_tiles = n_ctx // TOKEN_TILE
    lat_tiles_per_seq = dec_seq // TOKEN_TILE
    mod_row = lambda i: jnp.where(i < ctx_tiles, 0, 1 + (i - ctx_tiles) // lat_tiles_per_seq)

    n_cond = 8
    cvec = jnp.zeros((n_cond, d), F32).at[0].set(c_ctx).at[1:1 + dec_batch].set(c)
    mod = _modulation(cvec, w_mod, b_mod).reshape(depth, n_cond, 6, d)

    x = (x_prompt.reshape(n_ctx, d), x_sample.reshape(n_lat, d))
    rope = _rope_tables(dec_seq)
    cache_k4 = cache_k.reshape(dec_batch, depth, past, kv_w)
    cache_v4 = cache_v.reshape(dec_batch, depth, past, kv_w)
    h0_ctx = jnp.zeros((batch, 2, lru_w), F32)
    assert d == SUBLANES * LANES, "row tables are moved as one (8, 128) tile per token"
    n_sorted_tiles = n * TOP_K // EXPERT_TILE + N_EXPERTS

    new_k, new_v, new_h = [], [], []
    for l in range(depth):
        p = {
            'conf_conv_w': conf_conv_w[l], 'conf_conv_b': conf_conv_b[l].reshape(1, conf_w),
            'conf_ln_w': conf_ln_w[l].reshape(1, conf_w), 'conf_ln_b': conf_ln_b[l].reshape(1, conf_w),
            'lru_conv_w': lru_conv_w[l], 'lru_conv_b': lru_conv_b[l].reshape(1, lru_w),
            'lru_wg': jnp.concatenate([_block_diag(lru_wa[l]), _block_diag(lru_wx[l])], axis=-1).astype(BF16),
            'lru_bg': jnp.concatenate([lru_ba[l].reshape(-1), lru_bx[l].reshape(-1)]).reshape(1, 4 * lru_w),
            'lru_lam': lru_lambda[l].reshape(1, 2 * lru_w),
        }
        conf_u, q, k, v, lru_x, lru_g = _pre_mixer(x, mod[l], mod_row, norm_mix_w[l], w_in[l], widths)

        conf_c, lru_c, h_last = _seq_mixers(seq, batch, 0, conf_u, lru_x, lru_g, h0_ctx, p)
        conf_l, lru_l, _ = _seq_mixers(dec_seq, dec_batch, n_ctx, conf_u, lru_x, lru_g, state_lru[:, l], p)

        qw = q_norm_w[l].reshape(1, HEAD_DIM)
        kw = k_norm_w[l].reshape(1, HEAD_DIM)
        attn_c, k_ctx = _attention_ctx(seq, batch, q, k, v, qw, kw)
        attn_l = _attention_lat(dec_seq, dec_batch, n_ctx, l, q, k, v, cache_k4, cache_v4,
                                jnp.tile(qw, (1, N_HEADS)), jnp.tile(kw, (1, N_KV_HEADS)), rope)
        new_k.append(k_ctx.reshape(batch, seq, N_KV_HEADS, HEAD_DIM))
        new_v.append(v[:n_ctx].reshape(batch, seq, N_KV_HEADS, HEAD_DIM))
        new_h.append(h_last)

        x1, h2, topi, gates, rank, counts = _post_mixer(
            x, (conf_c, attn_c, lru_c), (conf_l, attn_l, lru_l), mod[l], mod_row, w_out[l],
            norm_ffn_w[l], router_w[l], router_b[l])
        offs, tile_end, tile_expert, next_expert, parity, n_used = _routing_tables(counts, n_sorted_tiles)
        pos = _positions(topi, rank, offs)
        xs = _dispatch(h2, pos, tile_end, n_sorted_tiles)
        ys = _experts(l, tile_expert, next_expert, parity, n_used, xs, w_gu, b_gu, w_down, b_down)
        x = tuple(_combine(l == depth - 1, row0, rows, x1, gates, mod[l], mod_row, final_norm_w, pos, ys)
                  for row0, rows in ((0, n_ctx), (n_ctx, n_lat)))

    y_prompt = x[0].reshape(batch, seq, d)
    y_sample = x[1].reshape(dec_batch, dec_seq, d)
    return (y_prompt, y_sample, jnp.stack(new_k, axis=1), jnp.stack(new_v, axis=1), jnp.stack(new_h, axis=1))
```

```python
import functools

import jax
import jax.numpy as jnp
import numpy as np
from jax import lax
from jax.experimental import pallas as pl
from jax.experimental.pallas import tpu as pltpu

F32 = jnp.float32
BF16 = jnp.bfloat16

HEAD_DIM = 64
N_HEADS = 8
N_KV_HEADS = 2
HEADS_PER_KV = N_HEADS // N_KV_HEADS
CONF_KERNEL = 31
LRU_CONV = 4
LRU_C = 8.0
N_EXPERTS = 32
TOP_K = 4
SWIGLU_ALPHA = 1.702
SWIGLU_LIMIT = 7.0
ROPE_THETA = 10000.0
GRID_W = 64
EPS = 1e-6

LANES = 128
SUBLANES = 8
ISSUE_UNROLL = 4
TOKEN_TILE = 256
EXPERT_TILE = 256
TILES_PER_STEP = 2
CONV_HALO = 16
ROW_CHUNK = 64
SCAN_CHUNK = 32
VMEM_LIMIT = 56 * 1024 * 1024


def _params(sem, vmem=None):
    return pltpu.CompilerParams(dimension_semantics=sem, vmem_limit_bytes=vmem)


def _split_bf16(x):
    hi = x.astype(BF16)
    lo = (x - hi.astype(F32)).astype(BF16)
    return hi, lo


def _dot(a, b):
    return jnp.dot(a, b, preferred_element_type=F32)


def _dot3(a, b):
    a_hi, a_lo = _split_bf16(a)
    b_hi, b_lo = _split_bf16(b)
    return _dot(a_hi, b_hi) + (_dot(a_hi, b_lo) + _dot(a_lo, b_hi))


def _mod_kernel(c_ref, w_ref, b_ref, o_ref):
    c = c_ref[...]
    s = c * jax.nn.sigmoid(c)
    o_ref[...] = _dot3(s, w_ref[...]) + b_ref[...]


def _modulation(cvec, w_mod, b_mod):
    depth, d, d6 = w_mod.shape
    tn = 768
    return pl.pallas_call(
        _mod_kernel,
        out_shape=jax.ShapeDtypeStruct((depth, cvec.shape[0], d6), F32),
        grid=(depth, d6 // tn),
        in_specs=[
            pl.BlockSpec(cvec.shape, lambda l, j: (0, 0)),
            pl.BlockSpec((None, d, tn), lambda l, j: (l, 0, j)),
            pl.BlockSpec((None, 1, tn), lambda l, j: (l, 0, j)),
        ],
        out_specs=pl.BlockSpec((None, cvec.shape[0], tn), lambda l, j: (l, 0, j)),
        compiler_params=_params(("arbitrary", "arbitrary")),
        name="modulation",
    )(cvec, w_mod, b_mod.reshape(depth, 1, d6))


def _pre_kernel(ctx_tiles, xc_ref, xl_ref, mod_ref, nw_ref, win_f32_ref, conf_ref, q_ref, k_ref, v_ref, lx_ref,
                lg_ref, win_ref):
    @pl.when(pl.program_id(0) == 0)
    def _():
        win_ref[...] = win_f32_ref[...].astype(BF16)

    x = jnp.where(pl.program_id(0) < ctx_tiles, xc_ref[...], xl_ref[...])
    m = mod_ref[...]
    shift, scale = m[0:1], m[1:2]
    h = x * lax.rsqrt(jnp.mean(x * x, axis=-1, keepdims=True) + EPS) * nw_ref[...]
    h = h * (1.0 + scale) + shift
    proj = _dot(h.astype(BF16), win_ref[...])
    col = 0
    for ref in (conf_ref, q_ref, k_ref, v_ref, lx_ref, lg_ref):
        w = ref.shape[-1]
        ref[...] = proj[:, col:col + w]
        col += w


def _path_specs(x_pair, tm):
    ctx_tiles = x_pair[0].shape[0] // tm
    return ctx_tiles, [
        pl.BlockSpec((tm, x_pair[0].shape[-1]), lambda i: (jnp.minimum(i, ctx_tiles - 1), 0)),
        pl.BlockSpec((tm, x_pair[1].shape[-1]), lambda i: (jnp.maximum(i - ctx_tiles, 0), 0))]


def _pre_mixer(x_pair, mod_l, mod_row, norm_w, w_in, widths):
    n, d = x_pair[0].shape[0] + x_pair[1].shape[0], x_pair[0].shape[1]
    ctx_tiles, x_specs = _path_specs(x_pair, TOKEN_TILE)
    return pl.pallas_call(
        functools.partial(_pre_kernel, ctx_tiles),
        out_shape=[jax.ShapeDtypeStruct((n, w), F32) for w in widths],
        grid=(n // TOKEN_TILE,),
        in_specs=x_specs + [
            pl.BlockSpec((None, 6, d), lambda i: (mod_row(i), 0, 0)),
            pl.BlockSpec((1, d), lambda i: (0, 0)),
            pl.BlockSpec(w_in.shape, lambda i: (0, 0)),
        ],
        out_specs=[pl.BlockSpec((TOKEN_TILE, w), lambda i: (i, 0)) for w in widths],
        scratch_shapes=[pltpu.VMEM(w_in.shape, BF16)],
        compiler_params=_params(("arbitrary",), VMEM_LIMIT),
        name="pre_mixer",
    )(*x_pair, mod_l, norm_w.reshape(1, d), w_in)


def _gelu_tanh(x):
    return 0.5 * x * (1.0 + jnp.tanh(0.7978845608028654 * (x + 0.044715 * (x * x * x))))


def _seq_kernel(seq_len, conf_ref, lx_ref, lg_ref, h0_ref, ccw_ref, ccb_ref, lnw_ref, lnb_ref,
                lcw_ref, lcb_ref, wa_ref, wx_ref, bg_ref, lam_ref,
                co_ref, lo_ref, hl_ref, pad_ref, rot_ref, af_ref, uf_ref, ab_ref, ub_ref, wg_ref):
    L = seq_len
    W = co_ref.shape[-1]
    pad = L // 2
    zeros_halo = jnp.zeros((CONV_HALO, W), F32)

    @pl.when(pl.program_id(0) == 0)
    def _():
        dirs, heads, blk, _ = wa_ref.shape
        zero_blk = jnp.zeros((blk, blk), F32)
        for h in range(heads):
            cols = [w_ref[d, h] if g == h else zero_blk
                    for w_ref in (wa_ref, wx_ref) for d in range(dirs) for g in range(heads)]
            wg_ref[h * blk:(h + 1) * blk, :] = jnp.concatenate(cols, axis=-1).astype(BF16)

    pad_ref[0:CONV_HALO, :] = zeros_halo
    pad_ref[CONV_HALO + L:2 * CONV_HALO + L, :] = zeros_halo
    for c in range(L // ROW_CHUNK):
        r0 = c * ROW_CHUNK
        u = conf_ref[r0:r0 + ROW_CHUNK, :]
        pad_ref[CONV_HALO + r0:CONV_HALO + r0 + ROW_CHUNK, :] = u[:, :W] * jax.nn.sigmoid(u[:, W:])
    shifted_rows = L + 2 * CONV_HALO - SUBLANES
    for s in range(1, SUBLANES):
        for r0 in range(0, shifted_rows, ROW_CHUNK):
            rows = min(ROW_CHUNK, shifted_rows - r0)
            rot_ref[s - 1, r0:r0 + rows, :] = pad_ref[r0 + s:r0 + s + rows, :]
    left = CONF_KERNEL // 2
    for c in range(L // ROW_CHUNK):
        r0 = c * ROW_CHUNK
        acc = jnp.zeros((ROW_CHUNK, W), F32) + ccb_ref[...]
        for k in range(CONF_KERNEL):
            whole, s = divmod(CONV_HALO - left + k, SUBLANES)
            start = whole * SUBLANES + r0
            src = pad_ref if s == 0 else rot_ref.at[s - 1]
            acc = acc + ccw_ref[k:k + 1, :] * src[start:start + ROW_CHUNK, :]
        mu = jnp.mean(acc, axis=-1, keepdims=True)
        cen = acc - mu
        var = jnp.mean(cen * cen, axis=-1, keepdims=True)
        y = cen * lax.rsqrt(var + EPS) * lnw_ref[...] + lnb_ref[...]
        co_ref[r0:r0 + ROW_CHUNK, :] = y * jax.nn.sigmoid(y)

    for c in range(L // ROW_CHUNK):
        r0 = c * ROW_CHUNK
        pad_ref[CONV_HALO + r0:CONV_HALO + r0 + ROW_CHUNK, :] = lx_ref[r0:r0 + ROW_CHUNK, :]
    ones_pad = jnp.ones((pad, W), F32)
    zeros_pad = jnp.zeros((pad, W), F32)
    af_ref[0:pad, :] = ones_pad
    uf_ref[0:pad, :] = zeros_pad
    ab_ref[L:L + pad, :] = ones_pad
    ub_ref[L:L + pad, :] = zeros_pad
    sp = jax.nn.softplus(-lam_ref[...])
    h0 = h0_ref[...]
    left = LRU_CONV // 2
    n_chunks = L // ROW_CHUNK
    for c in range(n_chunks):
        r0 = c * ROW_CHUNK
        xc = jnp.zeros((ROW_CHUNK, W), F32) + lcb_ref[...]
        for k in range(LRU_CONV):
            start = CONV_HALO - left + k + r0
            xc = xc + lcw_ref[k:k + 1, :] * pad_ref[start:start + ROW_CHUNK, :]
        g = _dot(xc.astype(BF16), wg_ref[...]) + bg_ref[...]
        r = jax.nn.sigmoid(g[:, :2 * W])
        i = jax.nn.sigmoid(g[:, 2 * W:])
        a = jnp.exp((-LRU_C) * r * sp)
        xc2 = jnp.concatenate([xc, xc], axis=-1)
        u = jnp.sqrt(1.0 - a * a) * i * xc2
        a_f, a_b, u_f, u_b = a[:, :W], a[:, W:], u[:, :W], u[:, W:]
        row = lax.broadcasted_iota(jnp.int32, (ROW_CHUNK, W), 0)
        if c == 0:
            u_f = jnp.where(row == 0, u_f + a_f * h0[0:1], u_f)
        if c == n_chunks - 1:
            u_b = jnp.where(row == ROW_CHUNK - 1, u_b + a_b * h0[1:2], u_b)
        af_ref[pad + r0:pad + r0 + ROW_CHUNK, :] = a_f
        uf_ref[pad + r0:pad + r0 + ROW_CHUNK, :] = u_f
        ab_ref[r0:r0 + ROW_CHUNK, :] = a_b
        ub_ref[r0:r0 + ROW_CHUNK, :] = u_b

    n_sc = L // SCAN_CHUNK
    s = 1
    while s < L:
        for c in reversed(range(n_sc)):
            r0 = c * SCAN_CHUNK
            if r0 + SCAN_CHUNK <= s:
                continue
            cur = slice(pad + r0, pad + r0 + SCAN_CHUNK)
            sh = slice(pad + r0 - s, pad + r0 - s + SCAN_CHUNK)
            a_cur = af_ref[cur, :]
            uf_ref[cur, :] = uf_ref[cur, :] + a_cur * uf_ref[sh, :]
            af_ref[cur, :] = a_cur * af_ref[sh, :]
        for c in range(n_sc):
            r0 = c * SCAN_CHUNK
            if r0 >= L - s:
                continue
            cur = slice(r0, r0 + SCAN_CHUNK)
            sh = slice(r0 + s, r0 + s + SCAN_CHUNK)
            a_cur = ab_ref[cur, :]
            ub_ref[cur, :] = ub_ref[cur, :] + a_cur * ub_ref[sh, :]
            ab_ref[cur, :] = a_cur * ab_ref[sh, :]
        s *= 2

    for c in range(n_chunks):
        r0 = c * ROW_CHUNK
        h = uf_ref[pad + r0:pad + r0 + ROW_CHUNK, :] + ub_ref[r0:r0 + ROW_CHUNK, :]
        lo_ref[r0:r0 + ROW_CHUNK, :] = h * _gelu_tanh(lg_ref[r0:r0 + ROW_CHUNK, :])
    hl_ref[0:1, :] = uf_ref[pad + L - 1:pad + L, :]
    hl_ref[1:2, :] = ub_ref[0:1, :]


def _seq_mixers(seq_len, n_seq, row0, conf_u, lru_x, lru_g, h0, p):
    w = lru_x.shape[-1]
    b0 = row0 // seq_len
    pad = seq_len // 2
    in_spec = lambda width: pl.BlockSpec((seq_len, width), lambda i: (b0 + i, 0))
    out_spec = pl.BlockSpec((seq_len, w), lambda i: (i, 0))
    state_spec = pl.BlockSpec((None, 2, w), lambda i: (i, 0, 0))
    full = lambda a: pl.BlockSpec(a.shape, lambda i: (0,) * a.ndim)
    weights = (p['conf_conv_w'], p['conf_conv_b'], p['conf_ln_w'], p['conf_ln_b'],
               p['lru_conv_w'], p['lru_conv_b'], p['lru_wa'], p['lru_wx'], p['lru_bg'], p['lru_lam'])
    padded = seq_len + 2 * CONV_HALO
    return pl.pallas_call(
        functools.partial(_seq_kernel, seq_len),
        out_shape=[jax.ShapeDtypeStruct((n_seq * seq_len, w), F32)] * 2
        + [jax.ShapeDtypeStruct((n_seq, 2, w), F32)],
        grid=(n_seq,),
        in_specs=[in_spec(2 * w), in_spec(w), in_spec(w), state_spec] + [full(a) for a in weights],
        out_specs=[out_spec, out_spec, state_spec],
        scratch_shapes=[pltpu.VMEM((padded, w), F32), pltpu.VMEM((SUBLANES - 1, padded, w), F32)]
        + [pltpu.VMEM((seq_len + pad, w), F32)] * 4 + [pltpu.VMEM((w, 4 * w), BF16)],
        compiler_params=_params(("arbitrary",), VMEM_LIMIT),
        name=f"seq_mixers_{seq_len}",
    )(conf_u, lru_x, lru_g, h0, *weights)


def _head_rms(x):
    return lax.rsqrt(jnp.mean(x * x, axis=-1, keepdims=True) + EPS)


def _swap_halves(x):
    width = x.shape[-1]
    lane = lax.broadcasted_iota(jnp.int32, x.shape, x.ndim - 1)
    up = pltpu.roll(x, width - HEAD_DIM // 4, x.ndim - 1)
    down = pltpu.roll(x, HEAD_DIM // 4, x.ndim - 1)
    return jnp.where((lane % (HEAD_DIM // 2)) < HEAD_DIM // 4, up, down)


def _attend(q_heads, k_bf16, v_bf16):
    scale = HEAD_DIM ** -0.5
    s = lax.dot_general(q_heads.astype(BF16), k_bf16, (((1,), (1,)), ((), ())),
                        preferred_element_type=F32) * scale
    p = jnp.exp(s - jnp.max(s, axis=-1, keepdims=True))
    denom = jnp.sum(p, axis=-1, keepdims=True)
    return _dot(p.astype(BF16), v_bf16) / denom


def _attn_ctx_kernel(q_ref, k_ref, v_ref, qw_ref, kw_ref, o_ref, kn_ref):
    L = q_ref.shape[0]
    q, k, v = q_ref[...], k_ref[...], v_ref[...]
    qw, kw = qw_ref[...], kw_ref[...]
    k_out, o_out = [], []
    for g in range(N_KV_HEADS):
        kh = k[:, g * HEAD_DIM:(g + 1) * HEAD_DIM]
        kh = kh * _head_rms(kh) * kw
        k_out.append(kh)
        qs = []
        for j in range(HEADS_PER_KV):
            h = g * HEADS_PER_KV + j
            qh = q[:, h * HEAD_DIM:(h + 1) * HEAD_DIM]
            qs.append(qh * _head_rms(qh) * qw)
        o = _attend(jnp.concatenate(qs, axis=0), kh.astype(BF16),
                    v[:, g * HEAD_DIM:(g + 1) * HEAD_DIM].astype(BF16))
        o_out += [o[j * L:(j + 1) * L] for j in range(HEADS_PER_KV)]
    kn_ref[...] = jnp.concatenate(k_out, axis=-1)
    o_ref[...] = jnp.concatenate(o_out, axis=-1)


def _attention_ctx(seq_len, n_seq, q, k, v, qw, kw):
    kvw = k.shape[-1]
    row_spec = lambda width: pl.BlockSpec((seq_len, width), lambda i: (i, 0))
    full = lambda a: pl.BlockSpec(a.shape, lambda i: (0,) * a.ndim)
    return pl.pallas_call(
        _attn_ctx_kernel,
        out_shape=[jax.ShapeDtypeStruct((n_seq * seq_len, q.shape[-1]), F32),
                   jax.ShapeDtypeStruct((n_seq * seq_len, kvw), F32)],
        grid=(n_seq,),
        in_specs=[row_spec(q.shape[-1]), row_spec(kvw), row_spec(kvw), full(qw), full(kw)],
        out_specs=[row_spec(q.shape[-1]), row_spec(kvw)],
        compiler_params=_params(("arbitrary",), VMEM_LIMIT),
        name="attention_ctx",
    )(q, k, v, qw, kw)


def _attn_lat_kernel(q_ref, k_ref, v_ref, ck_ref, cv_ref, qw_ref, kw_ref, cq_ref, sq_ref, ck_t_ref,
                     sk_t_ref, o_ref, kall_ref, vall_ref):
    L = k_ref.shape[0]
    tq = q_ref.shape[0]

    @pl.when(pl.program_id(1) == 0)
    def _():
        k = k_ref[...]
        t = k * kw_ref[...]
        rot = t * ck_t_ref[...] + _swap_halves(t) * sk_t_ref[...]
        parts = []
        for g in range(N_KV_HEADS):
            sl = slice(g * HEAD_DIM, (g + 1) * HEAD_DIM)
            parts.append(rot[:, sl] * _head_rms(k[:, sl]))
        kall_ref[0:L, :] = jnp.concatenate(parts, axis=-1).astype(BF16)
        kall_ref[L:, :] = ck_ref[...].astype(BF16)
        vall_ref[0:L, :] = v_ref[...].astype(BF16)
        vall_ref[L:, :] = cv_ref[...].astype(BF16)

    q = q_ref[...]
    t = q * qw_ref[...]
    rot = t * cq_ref[...] + _swap_halves(t) * sq_ref[...]
    o_out = []
    for g in range(N_KV_HEADS):
        qs = []
        for j in range(HEADS_PER_KV):
            sl = slice((g * HEADS_PER_KV + j) * HEAD_DIM, (g * HEADS_PER_KV + j + 1) * HEAD_DIM)
            qs.append(rot[:, sl] * _head_rms(q[:, sl]))
        sl = slice(g * HEAD_DIM, (g + 1) * HEAD_DIM)
        o = _attend(jnp.concatenate(qs, axis=0), kall_ref[:, sl], vall_ref[:, sl])
        o_out += [o[j * tq:(j + 1) * tq] for j in range(HEADS_PER_KV)]
    o_ref[...] = jnp.concatenate(o_out, axis=-1)


def _attention_lat(seq_len, n_seq, row0, layer, q, k, v, cache_k, cache_v, qw8, kw2, rope):
    tq = TOKEN_TILE
    nq = seq_len // tq
    qwid, kvw = q.shape[-1], k.shape[-1]
    past = cache_k.shape[2]
    b0q = row0 // tq
    b0s = row0 // seq_len
    cq, sq, ck, sk = rope
    full = lambda a: pl.BlockSpec(a.shape, lambda b, j: (0,) * a.ndim)
    seq_spec = pl.BlockSpec((seq_len, kvw), lambda b, j: (b0s + b, 0))
    cache_spec = pl.BlockSpec((None, None, past, kvw), lambda b, j: (b, layer, 0, 0))
    q_spec = pl.BlockSpec((tq, qwid), lambda b, j: (b0q + b * nq + j, 0))
    rope_q_spec = pl.BlockSpec((tq, qwid), lambda b, j: (j, 0))
    return pl.pallas_call(
        _attn_lat_kernel,
        out_shape=jax.ShapeDtypeStruct((n_seq * seq_len, qwid), F32),
        grid=(n_seq, nq),
        in_specs=[q_spec, seq_spec, seq_spec, cache_spec, cache_spec, full(qw8), full(kw2),
                  rope_q_spec, rope_q_spec, full(ck), full(sk)],
        out_specs=pl.BlockSpec((tq, qwid), lambda b, j: (b * nq + j, 0)),
        scratch_shapes=[pltpu.VMEM((seq_len + past, kvw), BF16)] * 2,
        compiler_params=_params(("arbitrary", "arbitrary"), VMEM_LIMIT),
        name="attention_lat",
    )(q, k, v, cache_k, cache_v, qw8, kw2, cq, sq, ck, sk)


def _rope_tables(seq_len):
    t = np.arange(seq_len)
    row = (t // GRID_W).astype(np.float32)
    col = (t % GRID_W).astype(np.float32)
    half = HEAD_DIM // 2
    freqs = (np.float32(ROPE_THETA) ** (-np.arange(0, half, 2, dtype=np.float32) / np.float32(half)))
    ang_r, ang_c = row[:, None] * freqs, col[:, None] * freqs
    cos = np.concatenate([np.cos(ang_r)] * 2 + [np.cos(ang_c)] * 2, axis=-1)
    sin = np.concatenate([-np.sin(ang_r), np.sin(ang_r), -np.sin(ang_c), np.sin(ang_c)], axis=-1)
    return tuple(jnp.asarray(a, F32) for a in
                 (np.tile(cos, (1, N_HEADS)), np.tile(sin, (1, N_HEADS)),
                  np.tile(cos, (1, N_KV_HEADS)), np.tile(sin, (1, N_KV_HEADS))))


def _slab_copy(src_ref, src_tok, dst_ref, dst_tok, sem):
    src = src_ref.at[pl.ds(pl.multiple_of(src_tok * SUBLANES, SUBLANES), SUBLANES)]
    dst = dst_ref.at[pl.ds(pl.multiple_of(dst_tok * SUBLANES, SUBLANES), SUBLANES)]
    return pltpu.make_async_copy(src, dst, sem)


def _load_slabs(ref, n_tok, lead=()):
    return jnp.concatenate(
        [ref[(*lead, pl.ds(j, n_tok, stride=SUBLANES), slice(None))] for j in range(SUBLANES)], axis=-1)


def _store_slabs(ref, val):
    for j in range(SUBLANES):
        ref[pl.ds(j, val.shape[0], stride=SUBLANES), :] = val[:, j * LANES:(j + 1) * LANES]


def _wait_slabs(hbm_ref, n_tok, sem):
    span = hbm_ref.at[pl.ds(0, n_tok * SUBLANES)]
    pltpu.make_async_copy(span, span, sem).wait()


def _fetch_positions(pos_ref, idx_ref, sem, tile):
    n = idx_ref.shape[0]
    copy = pltpu.make_async_copy(pos_ref.at[pl.ds(tile * n, n)], idx_ref, sem)
    copy.start()
    copy.wait()


def _choice_major(x):
    t = jnp.transpose(x)
    return jnp.concatenate([t[k:k + 1, :] for k in range(TOP_K)], axis=1)


def _post_kernel(ctx_tiles, xc_ref, xl_ref, co_c_ref, ao_c_ref, lo_c_ref, co_l_ref, ao_l_ref, lo_l_ref, mod_ref,
                 wout_f32_ref, nw_ref, rw_ref, rb_ref,
                 x1_ref, h2_ref, topi_ref, gates_ref, rank_ref, counts_ref, carry_ref, wout_ref):
    tm = xc_ref.shape[0]
    cw, aw = co_c_ref.shape[-1], ao_c_ref.shape[-1]
    is_ctx = pl.program_id(0) < ctx_tiles

    @pl.when(pl.program_id(0) == 0)
    def _():
        carry_ref[...] = jnp.zeros_like(carry_ref)
        wout_ref[...] = wout_f32_ref[...].astype(BF16)

    m = mod_ref[...]
    gate1, shift2, scale2 = m[2:3], m[3:4], m[4:5]
    pick = lambda c_ref, l_ref: jnp.where(is_ctx, c_ref[...], l_ref[...]).astype(BF16)
    x = jnp.where(is_ctx, xc_ref[...], xl_ref[...])
    mixed = (_dot(pick(co_c_ref, co_l_ref), wout_ref[0:cw, :])
             + _dot(pick(ao_c_ref, ao_l_ref), wout_ref[cw:cw + aw, :])
             + _dot(pick(lo_c_ref, lo_l_ref), wout_ref[cw + aw:, :]))
    x1 = x + gate1 * mixed
    x1_ref[...] = x1
    h2 = x1 * lax.rsqrt(jnp.mean(x1 * x1, axis=-1, keepdims=True) + EPS) * nw_ref[...]
    h2 = h2 * (1.0 + scale2) + shift2
    _store_slabs(h2_ref, h2)

    lane = lax.broadcasted_iota(jnp.int32, (tm, LANES), 1)
    lane_f = lane.astype(F32)
    logits = jnp.where(lane < N_EXPERTS, _dot3(h2, rw_ref[...]) + rb_ref[...], -jnp.inf)
    top_v, onehots = [], []
    topi = jnp.zeros((tm, LANES), F32)
    for k in range(TOP_K):
        mx = jnp.max(logits, axis=-1, keepdims=True)
        idx = jnp.min(jnp.where(logits == mx, lane_f, float(LANES)), axis=-1, keepdims=True)
        hit = lane_f == idx
        logits = jnp.where(hit, -jnp.inf, logits)
        top_v.append(mx)
        onehots.append(hit)
        topi = jnp.where(lane == k, idx, topi)
    topi_ref[...] = _choice_major(topi).astype(jnp.int32)
    exps = [jnp.exp(v - top_v[0]) for v in top_v]
    denom = exps[0] + exps[1] + exps[2] + exps[3]
    gates = jnp.zeros((tm, LANES), F32)
    for k in range(TOP_K):
        gates = jnp.where(lane == k, exps[k] / denom, gates)
    gates_ref[...] = gates

    chosen = jnp.zeros((tm, LANES), F32)
    for hit in onehots:
        chosen = jnp.where(hit, 1.0, chosen)
    r_i = lax.broadcasted_iota(jnp.int32, (tm, tm), 0)
    c_i = lax.broadcasted_iota(jnp.int32, (tm, tm), 1)
    lower = jnp.where(c_i < r_i, 1.0, 0.0).astype(BF16)
    before = _dot(lower, chosen.astype(BF16)) + carry_ref[...]
    rank = jnp.zeros((tm, LANES), F32)
    for k, hit in enumerate(onehots):
        rk = jnp.sum(jnp.where(hit, before, 0.0), axis=-1, keepdims=True)
        rank = jnp.where(lane == k, rk, rank)
    rank_ref[...] = _choice_major(rank).astype(jnp.int32)
    carry = carry_ref[...] + jnp.sum(chosen, axis=0, keepdims=True)
    carry_ref[...] = carry
    counts_ref[...] = carry


def _post_mixer(x_pair, ctx_outs, lat_outs, mod_l, mod_row, w_out, norm_w, router_w, router_b):
    n, d = x_pair[0].shape[0] + x_pair[1].shape[0], x_pair[0].shape[1]
    tm = TOKEN_TILE
    ctx_tiles, x_specs = _path_specs(x_pair, tm)
    row_spec = lambda width: pl.BlockSpec((tm, width), lambda i: (i, 0))
    ctx_spec = lambda a: pl.BlockSpec((tm, a.shape[-1]), lambda i: (jnp.minimum(i, ctx_tiles - 1), 0))
    lat_spec = lambda a: pl.BlockSpec((tm, a.shape[-1]), lambda i: (jnp.maximum(i - ctx_tiles, 0), 0))
    full = lambda a: pl.BlockSpec(a.shape, lambda i: (0,) * a.ndim)
    rw = jnp.pad(router_w, ((0, 0), (0, LANES - N_EXPERTS)))
    rb = jnp.pad(router_b, (0, LANES - N_EXPERTS)).reshape(1, LANES)
    nw = norm_w.reshape(1, d)
    lane_tile = pl.BlockSpec((tm, LANES), lambda i: (i, 0))
    flat_tile = pl.BlockSpec((None, 1, TOP_K * tm), lambda i: (i, 0, 0))
    flat_shape = jax.ShapeDtypeStruct((n // tm, 1, TOP_K * tm), jnp.int32)
    return pl.pallas_call(
        functools.partial(_post_kernel, ctx_tiles),
        out_shape=[jax.ShapeDtypeStruct((n, d), F32), jax.ShapeDtypeStruct((n * SUBLANES, LANES), F32),
                   flat_shape, jax.ShapeDtypeStruct((n, LANES), F32),
                   flat_shape, jax.ShapeDtypeStruct((1, LANES), F32)],
        grid=(n // tm,),
        in_specs=x_specs + [ctx_spec(a) for a in ctx_outs] + [lat_spec(a) for a in lat_outs]
        + [pl.BlockSpec((None, 6, d), lambda i: (mod_row(i), 0, 0)),
           full(w_out), full(nw), full(rw), full(rb)],
        out_specs=[row_spec(d), pl.BlockSpec((tm * SUBLANES, LANES), lambda i: (i, 0)),
                   flat_tile, lane_tile, flat_tile,
                   pl.BlockSpec((1, LANES), lambda i: (0, 0))],
        scratch_shapes=[pltpu.VMEM((1, LANES), F32), pltpu.VMEM(w_out.shape, BF16)],
        compiler_params=_params(("arbitrary",), VMEM_LIMIT),
        name="post_mixer",
    )(*x_pair, *ctx_outs, *lat_outs, mod_l, w_out, nw, rw, rb)


def _pos_kernel(offs_ref, topi_ref, rank_ref, pos_ref):
    topi = topi_ref[...]
    pos = rank_ref[...]
    for e in range(N_EXPERTS):
        pos = pos + jnp.where(topi == e, offs_ref[e], 0)
    pos_ref[...] = pos


def _positions(topi_flat, rank_flat, offs):
    tiles, _, per_tile = topi_flat.shape
    full = pl.BlockSpec((tiles, per_tile), lambda i, offs: (0, 0))
    pos = pl.pallas_call(
        _pos_kernel,
        out_shape=jax.ShapeDtypeStruct((tiles, per_tile), jnp.int32),
        grid_spec=pltpu.PrefetchScalarGridSpec(num_scalar_prefetch=1, grid=(1,), in_specs=[full, full],
                                               out_specs=full),
        compiler_params=_params(("arbitrary",)),
        name="moe_positions",
    )(offs, topi_flat.reshape(tiles, per_tile), rank_flat.reshape(tiles, per_tile))
    return pos.reshape(-1)


def _dispatch_kernel(tend_ref, h2_ref, pos_ref, xs_ref, idx_ref, zero_ref, idx_sem, row_sem):
    tm = h2_ref.shape[0] // SUBLANES
    tile_rows = EXPERT_TILE * SUBLANES

    @pl.when(pl.program_id(0) == 0)
    def _():
        zero_ref[...] = jnp.zeros_like(zero_ref)

        def last_tile_copy(e):
            start = pl.multiple_of((tend_ref[e] - 1) * tile_rows, tile_rows)
            return pltpu.make_async_copy(zero_ref, xs_ref.at[pl.ds(start, tile_rows)], row_sem)

        def has_tiles(e):
            return tend_ref[e] > (tend_ref[e - 1] if e else 0)

        for e in range(N_EXPERTS):
            pl.when(has_tiles(e))(lambda e=e: last_tile_copy(e).start())
        for e in range(N_EXPERTS):
            pl.when(has_tiles(e))(lambda e=e: last_tile_copy(e).wait())

        def spare_tile_copy(t):
            return pltpu.make_async_copy(
                zero_ref, xs_ref.at[pl.ds(pl.multiple_of(t * tile_rows, tile_rows), tile_rows)], row_sem)

        n_used, n_tiles = tend_ref[N_EXPERTS - 1], xs_ref.shape[0] // tile_rows
        lax.fori_loop(n_used, n_tiles, lambda t, c: (spare_tile_copy(t).start(), c)[1], 0)
        lax.fori_loop(n_used, n_tiles, lambda t, c: (spare_tile_copy(t).wait(), c)[1], 0)

    _fetch_positions(pos_ref, idx_ref, idx_sem, pl.program_id(0))

    def issue(r, carry):
        for k in range(TOP_K):
            _slab_copy(h2_ref, r, xs_ref, idx_ref[k * tm + r], row_sem).start(priority=k % 2)
        return carry

    lax.fori_loop(0, tm, issue, 0, unroll=ISSUE_UNROLL)
    _wait_slabs(xs_ref, tm * TOP_K, row_sem)


def _dispatch(h2_slabs, pos_flat, tile_end, n_tiles):
    tm = TOKEN_TILE
    n = h2_slabs.shape[0] // SUBLANES
    hbm = pl.BlockSpec(memory_space=pl.ANY)
    grid_spec = pltpu.PrefetchScalarGridSpec(
        num_scalar_prefetch=1,
        grid=(n // tm,),
        in_specs=[pl.BlockSpec((tm * SUBLANES, LANES), lambda i, tend: (i, 0)), hbm],
        out_specs=hbm,
        scratch_shapes=[pltpu.SMEM((tm * TOP_K,), jnp.int32),
                        pltpu.VMEM((EXPERT_TILE * SUBLANES, LANES), F32),
                        pltpu.SemaphoreType.DMA, pltpu.SemaphoreType.DMA],
    )
    return pl.pallas_call(
        _dispatch_kernel,
        out_shape=jax.ShapeDtypeStruct((n_tiles * EXPERT_TILE * SUBLANES, LANES), F32),
        grid_spec=grid_spec,
        compiler_params=_params(("arbitrary",)),
        name="moe_dispatch",
    )(tile_end, h2_slabs, pos_flat)


def _expert_kernel(layer, te_ref, nxt_ref, par_ref, nu_ref, xs_ref, wgu_hbm, wd_hbm, *refs):
    bias_refs = refs[:2 * TILES_PER_STEP]
    ys_ref, wgu_buf, wd_buf, wgu_bf, wd_bf, w_sem = refs[2 * TILES_PER_STEP:]
    dff = wd_hbm.shape[-2]
    tile_rows = EXPERT_TILE * SUBLANES

    def weight_copies(e, slot):
        return (pltpu.make_async_copy(wgu_hbm.at[layer, e], wgu_buf.at[slot], w_sem.at[slot]),
                pltpu.make_async_copy(wd_hbm.at[layer, e], wd_buf.at[slot], w_sem.at[2 + slot]))

    for half in range(TILES_PER_STEP):
        t = pl.program_id(0) * TILES_PER_STEP + half
        xs_tile = xs_ref.at[pl.ds(half * tile_rows, tile_rows)]
        ys_tile = ys_ref.at[pl.ds(half * tile_rows, tile_rows)]
        bgu_ref, bd_ref = bias_refs[2 * half:2 * half + 2]

        @pl.when(t < nu_ref[0])
        def _(t=t, xs_tile=xs_tile, ys_tile=ys_tile, bgu_ref=bgu_ref, bd_ref=bd_ref):
            prev = te_ref[jnp.maximum(t - 1, 0)]

            @pl.when((t == 0) | (te_ref[t] != prev))
            def _():
                slot = par_ref[t]

                @pl.when(t == 0)
                def _():
                    for copy in weight_copies(te_ref[0], slot):
                        copy.start()

                for copy in weight_copies(te_ref[t], slot):
                    copy.wait()

                @pl.when(nxt_ref[t] >= 0)
                def _():
                    for copy in weight_copies(nxt_ref[t], 1 - slot):
                        copy.start()

                wgu_bf[...] = wgu_buf[slot].astype(BF16)
                wd_bf[...] = wd_buf[slot].astype(BF16)

            x = _load_slabs(xs_tile, EXPERT_TILE)
            gu = _dot(x.astype(BF16), wgu_bf[...]) + bgu_ref[...]
            x_glu = jnp.minimum(gu[:, :dff], SWIGLU_LIMIT)
            x_lin = jnp.clip(gu[:, dff:], -SWIGLU_LIMIT, SWIGLU_LIMIT)
            act = x_glu * jax.nn.sigmoid(SWIGLU_ALPHA * x_glu) * (x_lin + 1.0)
            _store_slabs(ys_tile, _dot(act.astype(BF16), wd_bf[...]) + bd_ref[...])

        @pl.when(t >= nu_ref[0])
        def _(ys_tile=ys_tile):
            ys_tile[...] = jnp.zeros(ys_tile.shape, F32)


def _experts(layer, tile_expert, next_expert, parity, n_used, xs, w_gu, b_gu, w_down, b_down):
    tm = EXPERT_TILE
    d, dff2 = w_gu.shape[-2:]
    dff = w_down.shape[-2]
    step_rows = TILES_PER_STEP * tm * SUBLANES
    n_tiles = tile_expert.shape[0]
    assert n_tiles % TILES_PER_STEP == 0 and xs.shape[0] == n_tiles * tm * SUBLANES
    hbm = pl.BlockSpec(memory_space=pl.ANY)

    def bias_specs(half):
        tile = lambda i: jnp.minimum(i * TILES_PER_STEP + half, n_tiles - 1)
        return [pl.BlockSpec((None, None, 1, dff2), lambda i, te, nx, pa, nu: (layer, te[tile(i)], 0, 0)),
                pl.BlockSpec((None, None, 1, d), lambda i, te, nx, pa, nu: (layer, te[tile(i)], 0, 0))]

    last_step = lambda nu: (nu[0] - 1) // TILES_PER_STEP
    grid_spec = pltpu.PrefetchScalarGridSpec(
        num_scalar_prefetch=4,
        grid=(n_tiles // TILES_PER_STEP,),
        in_specs=[pl.BlockSpec((step_rows, LANES), lambda i, te, nx, pa, nu: (jnp.minimum(i, last_step(nu)), 0)),
                  hbm, hbm] + [s for half in range(TILES_PER_STEP) for s in bias_specs(half)],
        out_specs=pl.BlockSpec((step_rows, LANES), lambda i, te, nx, pa, nu: (i, 0)),
        scratch_shapes=[pltpu.VMEM((2, d, dff2), F32), pltpu.VMEM((2, dff, d), F32),
                        pltpu.VMEM((d, dff2), BF16), pltpu.VMEM((dff, d), BF16),
                        pltpu.SemaphoreType.DMA((4,))],
    )
    depth, n_e = w_gu.shape[:2]
    biases = (b_gu.reshape(depth, n_e, 1, dff2), b_down.reshape(depth, n_e, 1, d)) * TILES_PER_STEP
    return pl.pallas_call(
        functools.partial(_expert_kernel, layer),
        out_shape=jax.ShapeDtypeStruct(xs.shape, F32),
        grid_spec=grid_spec,
        compiler_params=_params(("arbitrary",), VMEM_LIMIT),
        name="moe_experts",
    )(tile_expert, next_expert, parity, n_used, xs, w_gu, w_down, *biases)


def _combine_kernel(final, tile0, x1_ref, gates_ref, mod_ref, fw_ref, pos_ref, ys_ref, out_ref,
                    idx_ref, rows_ref, idx_sem, row_sem):
    tm = x1_ref.shape[0]
    _fetch_positions(pos_ref, idx_ref, idx_sem, pl.program_id(0) + tile0)

    def issue(r, carry):
        for k in range(TOP_K):
            _slab_copy(ys_ref, idx_ref[k * tm + r], rows_ref.at[k], r, row_sem).start(priority=k % 2)
        return carry

    lax.fori_loop(0, tm, issue, 0, unroll=ISSUE_UNROLL)
    _wait_slabs(ys_ref, tm * TOP_K, row_sem)

    gates = gates_ref[...]
    moe = gates[:, 0:1] * _load_slabs(rows_ref, tm, (0,))
    for k in range(1, TOP_K):
        moe = moe + gates[:, k:k + 1] * _load_slabs(rows_ref, tm, (k,))
    x2 = x1_ref[...] + mod_ref[5:6, :] * moe
    if final:
        x2 = x2 * lax.rsqrt(jnp.mean(x2 * x2, axis=-1, keepdims=True) + EPS) * fw_ref[...]
    out_ref[...] = x2


def _combine(final, row0, n_rows, x1, gates, mod_l, mod_row, final_w, pos_flat, ys):
    d = x1.shape[1]
    tm = TOKEN_TILE
    tile0 = row0 // tm
    hbm = pl.BlockSpec(memory_space=pl.ANY)
    return pl.pallas_call(
        functools.partial(_combine_kernel, final, tile0),
        out_shape=jax.ShapeDtypeStruct((n_rows, d), F32),
        grid=(n_rows // tm,),
        in_specs=[pl.BlockSpec((tm, d), lambda i: (tile0 + i, 0)),
                  pl.BlockSpec((tm, LANES), lambda i: (tile0 + i, 0)),
                  pl.BlockSpec((None, 6, d), lambda i: (mod_row(tile0 + i), 0, 0)),
                  pl.BlockSpec((1, d), lambda i: (0, 0)),
                  hbm, hbm],
        out_specs=pl.BlockSpec((tm, d), lambda i: (i, 0)),
        scratch_shapes=[pltpu.SMEM((tm * TOP_K,), jnp.int32),
                        pltpu.VMEM((TOP_K, tm * SUBLANES, LANES), F32),
                        pltpu.SemaphoreType.DMA, pltpu.SemaphoreType.DMA],
        compiler_params=_params(("arbitrary",), VMEM_LIMIT),
        name="moe_combine",
    )(x1, gates, mod_l, final_w.reshape(1, d), pos_flat, ys)


def _routing_tables(counts, n_tiles):
    tm = EXPERT_TILE
    c = counts[0, :N_EXPERTS].astype(jnp.int32)
    tiles = (c + tm - 1) // tm
    tile_end = jnp.cumsum(tiles)
    offs = (tile_end - tiles) * tm
    n_used = tile_end[-1]
    t = jnp.minimum(jnp.arange(n_tiles, dtype=jnp.int32), n_used - 1)
    te = jnp.minimum(jnp.sum((tile_end[None, :] <= t[:, None]).astype(jnp.int32), axis=1), N_EXPERTS - 1)
    e = jnp.arange(N_EXPERTS, dtype=jnp.int32)
    later = (e[None, :] > e[:, None]) & (tiles[None, :] > 0)
    nxt = jnp.min(jnp.where(later, e[None, :], N_EXPERTS), axis=1)
    nxt = jnp.where(nxt == N_EXPERTS, -1, nxt)
    parity = (jnp.cumsum((tiles > 0).astype(jnp.int32)) - 1) % 2
    return (offs, tile_end.astype(jnp.int32), te, nxt[te].astype(jnp.int32),
            parity[te].astype(jnp.int32), n_used.reshape(1).astype(jnp.int32))


def kernel(x_prompt, x_sample, cache_k, cache_v, state_lru, c, c_ctx, w_mod, b_mod, norm_mix_w, w_in, conf_conv_w, conf_conv_b, conf_ln_w, conf_ln_b, q_norm_w, k_norm_w, lru_conv_w, lru_conv_b, lru_wa, lru_ba, lru_wx, lru_bx, lru_lambda, w_out, norm_ffn_w, router_w, router_b, w_gu, b_gu, w_down, b_down, final_norm_w):
    batch, seq, d = x_prompt.shape
    dec_batch, dec_seq, _ = x_sample.shape
    depth = w_mod.shape[0]
    n_ctx, n_lat = batch * seq, dec_batch * dec_seq
    n = n_ctx + n_lat
    conf_w = conf_conv_w.shape[-1]
    lru_w = lru_conv_w.shape[-1]
    kv_w = N_KV_HEADS * HEAD_DIM
    attn_w = N_HEADS * HEAD_DIM
    widths = (2 * conf_w, attn_w, kv_w, kv_w, lru_w, lru_w)
    past = cache_k.shape[2]

    ctx_tiles = n_ctx // TOKEN_TILE
    lat_tiles_per_seq = dec_seq // TOKEN_TILE
    mod_row = lambda i: jnp.where(i < ctx_tiles, 0, 1 + (i - ctx_tiles) // lat_tiles_per_seq)

    n_cond = 8
    cvec = jnp.zeros((n_cond, d), F32).at[0].set(c_ctx).at[1:1 + dec_batch].set(c)
    mod = _modulation(cvec, w_mod, b_mod).reshape(depth, n_cond, 6, d)

    x = (x_prompt.reshape(n_ctx, d), x_sample.reshape(n_lat, d))
    rope = _rope_tables(dec_seq)
    cache_k4 = cache_k.reshape(dec_batch, depth, past, kv_w)
    cache_v4 = cache_v.reshape(dec_batch, depth, past, kv_w)
    h0_ctx = jnp.zeros((batch, 2, lru_w), F32)
    assert d == SUBLANES * LANES, "row tables are moved as one (8, 128) tile per token"
    n_sorted_tiles = n * TOP_K // EXPERT_TILE + N_EXPERTS

    new_k, new_v, new_h = [], [], []
    for l in range(depth):
        p = {
            'conf_conv_w': conf_conv_w[l], 'conf_conv_b': conf_conv_b[l].reshape(1, conf_w),
            'conf_ln_w': conf_ln_w[l].reshape(1, conf_w), 'conf_ln_b': conf_ln_b[l].reshape(1, conf_w),
            'lru_conv_w': lru_conv_w[l], 'lru_conv_b': lru_conv_b[l].reshape(1, lru_w),
            'lru_wa': lru_wa[l], 'lru_wx': lru_wx[l],
            'lru_bg': jnp.concatenate([lru_ba[l].reshape(-1), lru_bx[l].reshape(-1)]).reshape(1, 4 * lru_w),
            'lru_lam': lru_lambda[l].reshape(1, 2 * lru_w),
        }
        conf_u, q, k, v, lru_x, lru_g = _pre_mixer(x, mod[l], mod_row, norm_mix_w[l], w_in[l], widths)

        conf_c, lru_c, h_last = _seq_mixers(seq, batch, 0, conf_u, lru_x, lru_g, h0_ctx, p)
        conf_l, lru_l, _ = _seq_mixers(dec_seq, dec_batch, n_ctx, conf_u, lru_x, lru_g, state_lru[:, l], p)

        qw = q_norm_w[l].reshape(1, HEAD_DIM)
        kw = k_norm_w[l].reshape(1, HEAD_DIM)
        attn_c, k_ctx = _attention_ctx(seq, batch, q, k, v, qw, kw)
        attn_l = _attention_lat(dec_seq, dec_batch, n_ctx, l, q, k, v, cache_k4, cache_v4,
                                jnp.tile(qw, (1, N_HEADS)), jnp.tile(kw, (1, N_KV_HEADS)), rope)
        new_k.append(k_ctx.reshape(batch, seq, N_KV_HEADS, HEAD_DIM))
        new_v.append(v[:n_ctx].reshape(batch, seq, N_KV_HEADS, HEAD_DIM))
        new_h.append(h_last)

        x1, h2, topi, gates, rank, counts = _post_mixer(
            x, (conf_c, attn_c, lru_c), (conf_l, attn_l, lru_l), mod[l], mod_row, w_out[l],
            norm_ffn_w[l], router_w[l], router_b[l])
        offs, tile_end, tile_expert, next_expert, parity, n_used = _routing_tables(counts, n_sorted_tiles)
        pos = _positions(topi, rank, offs)
        xs = _dispatch(h2, pos, tile_end, n_sorted_tiles)
        ys = _experts(l, tile_expert, next_expert, parity, n_used, xs, w_gu, b_gu, w_down, b_down)
        x = tuple(_combine(l == depth - 1, row0, rows, x1, gates, mod[l], mod_row, final_norm_w, pos, ys)
                  for row0, rows in ((0, n_ctx), (n_ctx, n_lat)))

    y_prompt = x[0].reshape(batch, seq, d)
    y_sample = x[1].reshape(dec_batch, dec_seq, d)
    return (y_prompt, y_sample, jnp.stack(new_k, axis=1), jnp.stack(new_v, axis=1), jnp.stack(new_h, axis=1))
```

```python
import functools

import jax
import jax.numpy as jnp
import numpy as np
from jax import lax
from jax.experimental import pallas as pl
from jax.experimental.pallas import tpu as pltpu

F32 = jnp.float32
BF16 = jnp.bfloat16

HEAD_DIM = 64
N_HEADS = 8
N_KV_HEADS = 2
HEADS_PER_KV = N_HEADS // N_KV_HEADS
CONF_KERNEL = 31
LRU_CONV = 4
LRU_C = 8.0
N_EXPERTS = 32
TOP_K = 4
SWIGLU_ALPHA = 1.702
SWIGLU_LIMIT = 7.0
ROPE_THETA = 10000.0
GRID_W = 64
EPS = 1e-6

LANES = 128
SUBLANES = 8
ISSUE_UNROLL = 4
TOKEN_TILE = 256
EXPERT_TILE = 256
TILES_PER_STEP = 2
CONV_HALO = 16
ROW_CHUNK = 64
SCAN_CHUNK = 32
VMEM_LIMIT = 56 * 1024 * 1024


def _params(sem, vmem=None):
    return pltpu.CompilerParams(dimension_semantics=sem, vmem_limit_bytes=vmem)


def _split_bf16(x):
    hi = x.astype(BF16)
    lo = (x - hi.astype(F32)).astype(BF16)
    return hi, lo


def _dot(a, b):
    return jnp.dot(a, b, preferred_element_type=F32)


def _dot3(a, b):
    a_hi, a_lo = _split_bf16(a)
    b_hi, b_lo = _split_bf16(b)
    return _dot(a_hi, b_hi) + (_dot(a_hi, b_lo) + _dot(a_lo, b_hi))


def _mod_kernel(c_ref, w_ref, b_ref, o_ref):
    c = c_ref[...]
    s = c * jax.nn.sigmoid(c)
    o_ref[...] = _dot3(s, w_ref[...]) + b_ref[...]


def _modulation(cvec, w_mod, b_mod):
    depth, d, d6 = w_mod.shape
    tn = 768
    return pl.pallas_call(
        _mod_kernel,
        out_shape=jax.ShapeDtypeStruct((depth, cvec.shape[0], d6), F32),
        grid=(depth, d6 // tn),
        in_specs=[
            pl.BlockSpec(cvec.shape, lambda l, j: (0, 0)),
            pl.BlockSpec((None, d, tn), lambda l, j: (l, 0, j)),
            pl.BlockSpec((None, 1, tn), lambda l, j: (l, 0, j)),
        ],
        out_specs=pl.BlockSpec((None, cvec.shape[0], tn), lambda l, j: (l, 0, j)),
        compiler_params=_params(("arbitrary", "arbitrary")),
        name="modulation",
    )(cvec, w_mod, b_mod.reshape(depth, 1, d6))


def _pre_kernel(ctx_tiles, xc_ref, xl_ref, mod_ref, nw_ref, win_f32_ref, conf_ref, q_ref, k_ref, v_ref, lx_ref,
                lg_ref, win_ref):
    @pl.when(pl.program_id(0) == 0)
    def _():
        win_ref[...] = win_f32_ref[...].astype(BF16)

    x = jnp.where(pl.program_id(0) < ctx_tiles, xc_ref[...], xl_ref[...])
    m = mod_ref[...]
    shift, scale = m[0:1], m[1:2]
    h = x * lax.rsqrt(jnp.mean(x * x, axis=-1, keepdims=True) + EPS) * nw_ref[...]
    h = h * (1.0 + scale) + shift
    proj = _dot(h.astype(BF16), win_ref[...])
    col = 0
    for ref in (conf_ref, q_ref, k_ref, v_ref, lx_ref, lg_ref):
        w = ref.shape[-1]
        ref[...] = proj[:, col:col + w]
        col += w


def _path_specs(x_pair, tm):
    ctx_tiles = x_pair[0].shape[0] // tm
    return ctx_tiles, [
        pl.BlockSpec((tm, x_pair[0].shape[-1]), lambda i: (jnp.minimum(i, ctx_tiles - 1), 0)),
        pl.BlockSpec((tm, x_pair[1].shape[-1]), lambda i: (jnp.maximum(i - ctx_tiles, 0), 0))]


def _layer_spec(a, layer):
    zeros = (0,) * (a.ndim - 1)
    return pl.BlockSpec((None,) + a.shape[1:], lambda *_: (layer,) + zeros)


def _mod_spec(mod, layer, mod_row, tile0=0):
    return pl.BlockSpec((None, None) + mod.shape[2:], lambda i, *_: (layer, mod_row(tile0 + i), 0, 0))


def _pre_mixer(layer, x_pair, mod, mod_row, norm_w, w_in, widths):
    n, d = x_pair[0].shape[0] + x_pair[1].shape[0], x_pair[0].shape[1]
    ctx_tiles, x_specs = _path_specs(x_pair, TOKEN_TILE)
    return pl.pallas_call(
        functools.partial(_pre_kernel, ctx_tiles),
        out_shape=[jax.ShapeDtypeStruct((n, w), F32) for w in widths],
        grid=(n // TOKEN_TILE,),
        in_specs=x_specs + [_mod_spec(mod, layer, mod_row), _layer_spec(norm_w, layer),
                            _layer_spec(w_in, layer)],
        out_specs=[pl.BlockSpec((TOKEN_TILE, w), lambda i: (i, 0)) for w in widths],
        scratch_shapes=[pltpu.VMEM(w_in.shape[1:], BF16)],
        compiler_params=_params(("arbitrary",), VMEM_LIMIT),
        name="pre_mixer",
    )(*x_pair, mod, norm_w, w_in)


def _gelu_tanh(x):
    return 0.5 * x * (1.0 + jnp.tanh(0.7978845608028654 * (x + 0.044715 * (x * x * x))))


def _seq_kernel(seq_len, conf_ref, lx_ref, lg_ref, h0_ref, ccw_ref, ccb_ref, lnw_ref, lnb_ref,
                lcw_ref, lcb_ref, wa_ref, wx_ref, bg_ref, lam_ref,
                co_ref, lo_ref, hl_ref, pad_ref, rot_ref, af_ref, uf_ref, ab_ref, ub_ref, wg_ref):
    L = seq_len
    W = co_ref.shape[-1]
    pad = L // 2
    zeros_halo = jnp.zeros((CONV_HALO, W), F32)

    @pl.when(pl.program_id(0) == 0)
    def _():
        dirs, heads, blk, _ = wa_ref.shape
        zero_blk = jnp.zeros((blk, blk), F32)
        for h in range(heads):
            cols = [w_ref[d, h] if g == h else zero_blk
                    for w_ref in (wa_ref, wx_ref) for d in range(dirs) for g in range(heads)]
            wg_ref[h * blk:(h + 1) * blk, :] = jnp.concatenate(cols, axis=-1).astype(BF16)

    pad_ref[0:CONV_HALO, :] = zeros_halo
    pad_ref[CONV_HALO + L:2 * CONV_HALO + L, :] = zeros_halo
    for c in range(L // ROW_CHUNK):
        r0 = c * ROW_CHUNK
        u = conf_ref[r0:r0 + ROW_CHUNK, :]
        pad_ref[CONV_HALO + r0:CONV_HALO + r0 + ROW_CHUNK, :] = u[:, :W] * jax.nn.sigmoid(u[:, W:])
    shifted_rows = L + 2 * CONV_HALO - SUBLANES
    for s in range(1, SUBLANES):
        for r0 in range(0, shifted_rows, ROW_CHUNK):
            rows = min(ROW_CHUNK, shifted_rows - r0)
            rot_ref[s - 1, r0:r0 + rows, :] = pad_ref[r0 + s:r0 + s + rows, :]
    left = CONF_KERNEL // 2
    for c in range(L // ROW_CHUNK):
        r0 = c * ROW_CHUNK
        acc = jnp.zeros((ROW_CHUNK, W), F32) + ccb_ref[...]
        for k in range(CONF_KERNEL):
            whole, s = divmod(CONV_HALO - left + k, SUBLANES)
            start = whole * SUBLANES + r0
            src = pad_ref if s == 0 else rot_ref.at[s - 1]
            acc = acc + ccw_ref[k:k + 1, :] * src[start:start + ROW_CHUNK, :]
        mu = jnp.mean(acc, axis=-1, keepdims=True)
        cen = acc - mu
        var = jnp.mean(cen * cen, axis=-1, keepdims=True)
        y = cen * lax.rsqrt(var + EPS) * lnw_ref[...] + lnb_ref[...]
        co_ref[r0:r0 + ROW_CHUNK, :] = y * jax.nn.sigmoid(y)

    for c in range(L // ROW_CHUNK):
        r0 = c * ROW_CHUNK
        pad_ref[CONV_HALO + r0:CONV_HALO + r0 + ROW_CHUNK, :] = lx_ref[r0:r0 + ROW_CHUNK, :]
    ones_pad = jnp.ones((pad, W), F32)
    zeros_pad = jnp.zeros((pad, W), F32)
    af_ref[0:pad, :] = ones_pad
    uf_ref[0:pad, :] = zeros_pad
    ab_ref[L:L + pad, :] = ones_pad
    ub_ref[L:L + pad, :] = zeros_pad
    sp = jax.nn.softplus(-lam_ref[...])
    h0 = h0_ref[...]
    left = LRU_CONV // 2
    n_chunks = L // ROW_CHUNK
    for c in range(n_chunks):
        r0 = c * ROW_CHUNK
        xc = jnp.zeros((ROW_CHUNK, W), F32) + lcb_ref[...]
        for k in range(LRU_CONV):
            start = CONV_HALO - left + k + r0
            xc = xc + lcw_ref[k:k + 1, :] * pad_ref[start:start + ROW_CHUNK, :]
        g = _dot(xc.astype(BF16), wg_ref[...]) + bg_ref[...]
        r = jax.nn.sigmoid(g[:, :2 * W])
        i = jax.nn.sigmoid(g[:, 2 * W:])
        a = jnp.exp((-LRU_C) * r * sp)
        xc2 = jnp.concatenate([xc, xc], axis=-1)
        u = jnp.sqrt(1.0 - a * a) * i * xc2
        a_f, a_b, u_f, u_b = a[:, :W], a[:, W:], u[:, :W], u[:, W:]
        row = lax.broadcasted_iota(jnp.int32, (ROW_CHUNK, W), 0)
        if c == 0:
            u_f = jnp.where(row == 0, u_f + a_f * h0[0:1], u_f)
        if c == n_chunks - 1:
            u_b = jnp.where(row == ROW_CHUNK - 1, u_b + a_b * h0[1:2], u_b)
        af_ref[pad + r0:pad + r0 + ROW_CHUNK, :] = a_f
        uf_ref[pad + r0:pad + r0 + ROW_CHUNK, :] = u_f
        ab_ref[r0:r0 + ROW_CHUNK, :] = a_b
        ub_ref[r0:r0 + ROW_CHUNK, :] = u_b

    n_sc = L // SCAN_CHUNK
    s = 1
    while s < L:
        for c in reversed(range(n_sc)):
            r0 = c * SCAN_CHUNK
            if r0 + SCAN_CHUNK <= s:
                continue
            cur = slice(pad + r0, pad + r0 + SCAN_CHUNK)
            sh = slice(pad + r0 - s, pad + r0 - s + SCAN_CHUNK)
            a_cur = af_ref[cur, :]
            uf_ref[cur, :] = uf_ref[cur, :] + a_cur * uf_ref[sh, :]
            af_ref[cur, :] = a_cur * af_ref[sh, :]
        for c in range(n_sc):
            r0 = c * SCAN_CHUNK
            if r0 >= L - s:
                continue
            cur = slice(r0, r0 + SCAN_CHUNK)
            sh = slice(r0 + s, r0 + s + SCAN_CHUNK)
            a_cur = ab_ref[cur, :]
            ub_ref[cur, :] = ub_ref[cur, :] + a_cur * ub_ref[sh, :]
            ab_ref[cur, :] = a_cur * ab_ref[sh, :]
        s *= 2

    for c in range(n_chunks):
        r0 = c * ROW_CHUNK
        h = uf_ref[pad + r0:pad + r0 + ROW_CHUNK, :] + ub_ref[r0:r0 + ROW_CHUNK, :]
        lo_ref[r0:r0 + ROW_CHUNK, :] = h * _gelu_tanh(lg_ref[r0:r0 + ROW_CHUNK, :])
    hl_ref[0:1, :] = uf_ref[pad + L - 1:pad + L, :]
    hl_ref[1:2, :] = ub_ref[0:1, :]


def _seq_mixers(layer, seq_len, n_seq, row0, conf_u, lru_x, lru_g, h0, h0_layer, p):
    w = lru_x.shape[-1]
    b0 = row0 // seq_len
    pad = seq_len // 2
    in_spec = lambda width: pl.BlockSpec((seq_len, width), lambda i: (b0 + i, 0))
    out_spec = pl.BlockSpec((seq_len, w), lambda i: (i, 0))
    h0_spec = pl.BlockSpec((None, None, 2, w), lambda i: (i, h0_layer, 0, 0))
    weights = (p['conf_conv_w'], p['conf_conv_b'], p['conf_ln_w'], p['conf_ln_b'],
               p['lru_conv_w'], p['lru_conv_b'], p['lru_wa'], p['lru_wx'], p['lru_bg'], p['lru_lam'])
    padded = seq_len + 2 * CONV_HALO
    return pl.pallas_call(
        functools.partial(_seq_kernel, seq_len),
        out_shape=[jax.ShapeDtypeStruct((n_seq * seq_len, w), F32)] * 2
        + [jax.ShapeDtypeStruct((n_seq, 2, w), F32)],
        grid=(n_seq,),
        in_specs=[in_spec(2 * w), in_spec(w), in_spec(w), h0_spec] + [_layer_spec(a, layer) for a in weights],
        out_specs=[out_spec, out_spec, pl.BlockSpec((None, 2, w), lambda i: (i, 0, 0))],
        scratch_shapes=[pltpu.VMEM((padded, w), F32), pltpu.VMEM((SUBLANES - 1, padded, w), F32)]
        + [pltpu.VMEM((seq_len + pad, w), F32)] * 4 + [pltpu.VMEM((w, 4 * w), BF16)],
        compiler_params=_params(("arbitrary",), VMEM_LIMIT),
        name=f"seq_mixers_{seq_len}",
    )(conf_u, lru_x, lru_g, h0, *weights)


def _head_rms(x):
    return lax.rsqrt(jnp.mean(x * x, axis=-1, keepdims=True) + EPS)


def _swap_halves(x):
    width = x.shape[-1]
    lane = lax.broadcasted_iota(jnp.int32, x.shape, x.ndim - 1)
    up = pltpu.roll(x, width - HEAD_DIM // 4, x.ndim - 1)
    down = pltpu.roll(x, HEAD_DIM // 4, x.ndim - 1)
    return jnp.where((lane % (HEAD_DIM // 2)) < HEAD_DIM // 4, up, down)


def _attend(q_heads, k_bf16, v_bf16):
    scale = HEAD_DIM ** -0.5
    s = lax.dot_general(q_heads.astype(BF16), k_bf16, (((1,), (1,)), ((), ())),
                        preferred_element_type=F32) * scale
    p = jnp.exp(s - jnp.max(s, axis=-1, keepdims=True))
    denom = jnp.sum(p, axis=-1, keepdims=True)
    return _dot(p.astype(BF16), v_bf16) / denom


def _attn_ctx_kernel(q_ref, k_ref, v_ref, qw_ref, kw_ref, o_ref, kn_ref):
    L = q_ref.shape[0]
    q, k, v = q_ref[...], k_ref[...], v_ref[...]
    qw, kw = qw_ref[...], kw_ref[...]
    k_out, o_out = [], []
    for g in range(N_KV_HEADS):
        kh = k[:, g * HEAD_DIM:(g + 1) * HEAD_DIM]
        kh = kh * _head_rms(kh) * kw
        k_out.append(kh)
        qs = []
        for j in range(HEADS_PER_KV):
            h = g * HEADS_PER_KV + j
            qh = q[:, h * HEAD_DIM:(h + 1) * HEAD_DIM]
            qs.append(qh * _head_rms(qh) * qw)
        o = _attend(jnp.concatenate(qs, axis=0), kh.astype(BF16),
                    v[:, g * HEAD_DIM:(g + 1) * HEAD_DIM].astype(BF16))
        o_out += [o[j * L:(j + 1) * L] for j in range(HEADS_PER_KV)]
    kn_ref[...] = jnp.concatenate(k_out, axis=-1)
    o_ref[...] = jnp.concatenate(o_out, axis=-1)


def _attention_ctx(layer, seq_len, n_seq, q, k, v, qw, kw):
    kvw = k.shape[-1]
    row_spec = lambda width: pl.BlockSpec((seq_len, width), lambda i: (i, 0))
    return pl.pallas_call(
        _attn_ctx_kernel,
        out_shape=[jax.ShapeDtypeStruct((n_seq * seq_len, q.shape[-1]), F32),
                   jax.ShapeDtypeStruct((n_seq * seq_len, kvw), F32)],
        grid=(n_seq,),
        in_specs=[row_spec(q.shape[-1]), row_spec(kvw), row_spec(kvw), _layer_spec(qw, layer),
                  _layer_spec(kw, layer)],
        out_specs=[row_spec(q.shape[-1]), row_spec(kvw)],
        compiler_params=_params(("arbitrary",), VMEM_LIMIT),
        name="attention_ctx",
    )(q, k, v, qw, kw)


def _attn_lat_kernel(q_ref, k_ref, v_ref, ck_ref, cv_ref, qw_ref, kw_ref, cq_ref, sq_ref, ck_t_ref,
                     sk_t_ref, o_ref, kall_ref, vall_ref):
    L = k_ref.shape[0]
    tq = q_ref.shape[0]

    @pl.when(pl.program_id(1) == 0)
    def _():
        k = k_ref[...]
        t = k * kw_ref[...]
        rot = t * ck_t_ref[...] + _swap_halves(t) * sk_t_ref[...]
        parts = []
        for g in range(N_KV_HEADS):
            sl = slice(g * HEAD_DIM, (g + 1) * HEAD_DIM)
            parts.append(rot[:, sl] * _head_rms(k[:, sl]))
        kall_ref[0:L, :] = jnp.concatenate(parts, axis=-1).astype(BF16)
        kall_ref[L:, :] = ck_ref[...].astype(BF16)
        vall_ref[0:L, :] = v_ref[...].astype(BF16)
        vall_ref[L:, :] = cv_ref[...].astype(BF16)

    q = q_ref[...]
    t = q * qw_ref[...]
    rot = t * cq_ref[...] + _swap_halves(t) * sq_ref[...]
    o_out = []
    for g in range(N_KV_HEADS):
        qs = []
        for j in range(HEADS_PER_KV):
            sl = slice((g * HEADS_PER_KV + j) * HEAD_DIM, (g * HEADS_PER_KV + j + 1) * HEAD_DIM)
            qs.append(rot[:, sl] * _head_rms(q[:, sl]))
        sl = slice(g * HEAD_DIM, (g + 1) * HEAD_DIM)
        o = _attend(jnp.concatenate(qs, axis=0), kall_ref[:, sl], vall_ref[:, sl])
        o_out += [o[j * tq:(j + 1) * tq] for j in range(HEADS_PER_KV)]
    o_ref[...] = jnp.concatenate(o_out, axis=-1)


def _attention_lat(seq_len, n_seq, row0, layer, q, k, v, cache_k, cache_v, qw8, kw2, rope):
    tq = TOKEN_TILE
    nq = seq_len // tq
    qwid, kvw = q.shape[-1], k.shape[-1]
    past = cache_k.shape[2]
    b0q = row0 // tq
    b0s = row0 // seq_len
    cq, sq, ck, sk = rope
    full = lambda a: pl.BlockSpec(a.shape, lambda b, j: (0,) * a.ndim)
    seq_spec = pl.BlockSpec((seq_len, kvw), lambda b, j: (b0s + b, 0))
    cache_spec = pl.BlockSpec((None, None, past, kvw), lambda b, j: (b, layer, 0, 0))
    q_spec = pl.BlockSpec((tq, qwid), lambda b, j: (b0q + b * nq + j, 0))
    rope_q_spec = pl.BlockSpec((tq, qwid), lambda b, j: (j, 0))
    return pl.pallas_call(
        _attn_lat_kernel,
        out_shape=jax.ShapeDtypeStruct((n_seq * seq_len, qwid), F32),
        grid=(n_seq, nq),
        in_specs=[q_spec, seq_spec, seq_spec, cache_spec, cache_spec, _layer_spec(qw8, layer),
                  _layer_spec(kw2, layer), rope_q_spec, rope_q_spec, full(ck), full(sk)],
        out_specs=pl.BlockSpec((tq, qwid), lambda b, j: (b * nq + j, 0)),
        scratch_shapes=[pltpu.VMEM((seq_len + past, kvw), BF16)] * 2,
        compiler_params=_params(("arbitrary", "arbitrary"), VMEM_LIMIT),
        name="attention_lat",
    )(q, k, v, cache_k, cache_v, qw8, kw2, cq, sq, ck, sk)


def _rope_tables(seq_len):
    t = np.arange(seq_len)
    row = (t // GRID_W).astype(np.float32)
    col = (t % GRID_W).astype(np.float32)
    half = HEAD_DIM // 2
    freqs = (np.float32(ROPE_THETA) ** (-np.arange(0, half, 2, dtype=np.float32) / np.float32(half)))
    ang_r, ang_c = row[:, None] * freqs, col[:, None] * freqs
    cos = np.concatenate([np.cos(ang_r)] * 2 + [np.cos(ang_c)] * 2, axis=-1)
    sin = np.concatenate([-np.sin(ang_r), np.sin(ang_r), -np.sin(ang_c), np.sin(ang_c)], axis=-1)
    return tuple(jnp.asarray(a, F32) for a in
                 (np.tile(cos, (1, N_HEADS)), np.tile(sin, (1, N_HEADS)),
                  np.tile(cos, (1, N_KV_HEADS)), np.tile(sin, (1, N_KV_HEADS))))


def _slab_copy(src_ref, src_tok, dst_ref, dst_tok, sem):
    src = src_ref.at[pl.ds(pl.multiple_of(src_tok * SUBLANES, SUBLANES), SUBLANES)]
    dst = dst_ref.at[pl.ds(pl.multiple_of(dst_tok * SUBLANES, SUBLANES), SUBLANES)]
    return pltpu.make_async_copy(src, dst, sem)


def _load_slabs(ref, n_tok, lead=()):
    return jnp.concatenate(
        [ref[(*lead, pl.ds(j, n_tok, stride=SUBLANES), slice(None))] for j in range(SUBLANES)], axis=-1)


def _store_slabs(ref, val):
    for j in range(SUBLANES):
        ref[pl.ds(j, val.shape[0], stride=SUBLANES), :] = val[:, j * LANES:(j + 1) * LANES]


def _wait_slabs(hbm_ref, n_tok, sem):
    span = hbm_ref.at[pl.ds(0, n_tok * SUBLANES)]
    pltpu.make_async_copy(span, span, sem).wait()


def _fetch_positions(pos_ref, idx_ref, sem, tile):
    n = idx_ref.shape[0]
    copy = pltpu.make_async_copy(pos_ref.at[pl.ds(tile * n, n)], idx_ref, sem)
    copy.start()
    copy.wait()


def _choice_major(x):
    t = jnp.transpose(x)
    return jnp.concatenate([t[k:k + 1, :] for k in range(TOP_K)], axis=1)


def _post_kernel(ctx_tiles, xc_ref, xl_ref, co_c_ref, ao_c_ref, lo_c_ref, co_l_ref, ao_l_ref, lo_l_ref, mod_ref,
                 wout_f32_ref, nw_ref, rw_ref, rb_ref,
                 x1_ref, h2_ref, topi_ref, gates_ref, rank_ref, counts_ref, carry_ref, wout_ref):
    tm = xc_ref.shape[0]
    cw, aw = co_c_ref.shape[-1], ao_c_ref.shape[-1]
    is_ctx = pl.program_id(0) < ctx_tiles

    @pl.when(pl.program_id(0) == 0)
    def _():
        carry_ref[...] = jnp.zeros_like(carry_ref)
        wout_ref[...] = wout_f32_ref[...].astype(BF16)

    m = mod_ref[...]
    gate1, shift2, scale2 = m[2:3], m[3:4], m[4:5]
    pick = lambda c_ref, l_ref: jnp.where(is_ctx, c_ref[...], l_ref[...]).astype(BF16)
    x = jnp.where(is_ctx, xc_ref[...], xl_ref[...])
    mixed = (_dot(pick(co_c_ref, co_l_ref), wout_ref[0:cw, :])
             + _dot(pick(ao_c_ref, ao_l_ref), wout_ref[cw:cw + aw, :])
             + _dot(pick(lo_c_ref, lo_l_ref), wout_ref[cw + aw:, :]))
    x1 = x + gate1 * mixed
    x1_ref[...] = x1
    h2 = x1 * lax.rsqrt(jnp.mean(x1 * x1, axis=-1, keepdims=True) + EPS) * nw_ref[...]
    h2 = h2 * (1.0 + scale2) + shift2
    _store_slabs(h2_ref, h2)

    lane = lax.broadcasted_iota(jnp.int32, (tm, LANES), 1)
    lane_f = lane.astype(F32)
    logits = jnp.where(lane < N_EXPERTS, _dot3(h2, rw_ref[...]) + rb_ref[...], -jnp.inf)
    top_v, onehots = [], []
    topi = jnp.zeros((tm, LANES), F32)
    for k in range(TOP_K):
        mx = jnp.max(logits, axis=-1, keepdims=True)
        idx = jnp.min(jnp.where(logits == mx, lane_f, float(LANES)), axis=-1, keepdims=True)
        hit = lane_f == idx
        logits = jnp.where(hit, -jnp.inf, logits)
        top_v.append(mx)
        onehots.append(hit)
        topi = jnp.where(lane == k, idx, topi)
    topi_ref[...] = _choice_major(topi).astype(jnp.int32)
    exps = [jnp.exp(v - top_v[0]) for v in top_v]
    denom = exps[0] + exps[1] + exps[2] + exps[3]
    gates = jnp.zeros((tm, LANES), F32)
    for k in range(TOP_K):
        gates = jnp.where(lane == k, exps[k] / denom, gates)
    gates_ref[...] = gates

    chosen = jnp.zeros((tm, LANES), F32)
    for hit in onehots:
        chosen = jnp.where(hit, 1.0, chosen)
    r_i = lax.broadcasted_iota(jnp.int32, (tm, tm), 0)
    c_i = lax.broadcasted_iota(jnp.int32, (tm, tm), 1)
    lower = jnp.where(c_i < r_i, 1.0, 0.0).astype(BF16)
    before = _dot(lower, chosen.astype(BF16)) + carry_ref[...]
    rank = jnp.zeros((tm, LANES), F32)
    for k, hit in enumerate(onehots):
        rk = jnp.sum(jnp.where(hit, before, 0.0), axis=-1, keepdims=True)
        rank = jnp.where(lane == k, rk, rank)
    rank_ref[...] = _choice_major(rank).astype(jnp.int32)
    carry = carry_ref[...] + jnp.sum(chosen, axis=0, keepdims=True)
    carry_ref[...] = carry
    counts_ref[...] = carry


def _post_mixer(layer, x_pair, ctx_outs, lat_outs, mod, mod_row, w_out, norm_w, router_w, router_b):
    n, d = x_pair[0].shape[0] + x_pair[1].shape[0], x_pair[0].shape[1]
    tm = TOKEN_TILE
    ctx_tiles, x_specs = _path_specs(x_pair, tm)
    row_spec = lambda width: pl.BlockSpec((tm, width), lambda i: (i, 0))
    ctx_spec = lambda a: pl.BlockSpec((tm, a.shape[-1]), lambda i: (jnp.minimum(i, ctx_tiles - 1), 0))
    lat_spec = lambda a: pl.BlockSpec((tm, a.shape[-1]), lambda i: (jnp.maximum(i - ctx_tiles, 0), 0))
    lane_tile = pl.BlockSpec((tm, LANES), lambda i: (i, 0))
    flat_tile = pl.BlockSpec((None, 1, TOP_K * tm), lambda i: (i, 0, 0))
    flat_shape = jax.ShapeDtypeStruct((n // tm, 1, TOP_K * tm), jnp.int32)
    return pl.pallas_call(
        functools.partial(_post_kernel, ctx_tiles),
        out_shape=[jax.ShapeDtypeStruct((n, d), F32), jax.ShapeDtypeStruct((n * SUBLANES, LANES), F32),
                   flat_shape, jax.ShapeDtypeStruct((n, LANES), F32),
                   flat_shape, jax.ShapeDtypeStruct((1, LANES), F32)],
        grid=(n // tm,),
        in_specs=x_specs + [ctx_spec(a) for a in ctx_outs] + [lat_spec(a) for a in lat_outs]
        + [_mod_spec(mod, layer, mod_row)]
        + [_layer_spec(a, layer) for a in (w_out, norm_w, router_w, router_b)],
        out_specs=[row_spec(d), pl.BlockSpec((tm * SUBLANES, LANES), lambda i: (i, 0)),
                   flat_tile, lane_tile, flat_tile,
                   pl.BlockSpec((1, LANES), lambda i: (0, 0))],
        scratch_shapes=[pltpu.VMEM((1, LANES), F32), pltpu.VMEM(w_out.shape[1:], BF16)],
        compiler_params=_params(("arbitrary",), VMEM_LIMIT),
        name="post_mixer",
    )(*x_pair, *ctx_outs, *lat_outs, mod, w_out, norm_w, router_w, router_b)


def _pos_kernel(offs_ref, topi_ref, rank_ref, pos_ref):
    topi = topi_ref[...]
    pos = rank_ref[...]
    for e in range(N_EXPERTS):
        pos = pos + jnp.where(topi == e, offs_ref[e], 0)
    pos_ref[...] = pos


def _positions(topi_flat, rank_flat, offs):
    tiles, _, per_tile = topi_flat.shape
    full = pl.BlockSpec((tiles, per_tile), lambda i, offs: (0, 0))
    pos = pl.pallas_call(
        _pos_kernel,
        out_shape=jax.ShapeDtypeStruct((tiles, per_tile), jnp.int32),
        grid_spec=pltpu.PrefetchScalarGridSpec(num_scalar_prefetch=1, grid=(1,), in_specs=[full, full],
                                               out_specs=full),
        compiler_params=_params(("arbitrary",)),
        name="moe_positions",
    )(offs, topi_flat.reshape(tiles, per_tile), rank_flat.reshape(tiles, per_tile))
    return pos.reshape(-1)


def _dispatch_kernel(tend_ref, h2_ref, pos_ref, xs_ref, idx_ref, zero_ref, idx_sem, row_sem):
    tm = h2_ref.shape[0] // SUBLANES
    tile_rows = EXPERT_TILE * SUBLANES

    @pl.when(pl.program_id(0) == 0)
    def _():
        zero_ref[...] = jnp.zeros_like(zero_ref)

        def last_tile_copy(e):
            start = pl.multiple_of((tend_ref[e] - 1) * tile_rows, tile_rows)
            return pltpu.make_async_copy(zero_ref, xs_ref.at[pl.ds(start, tile_rows)], row_sem)

        def has_tiles(e):
            return tend_ref[e] > (tend_ref[e - 1] if e else 0)

        for e in range(N_EXPERTS):
            pl.when(has_tiles(e))(lambda e=e: last_tile_copy(e).start())
        for e in range(N_EXPERTS):
            pl.when(has_tiles(e))(lambda e=e: last_tile_copy(e).wait())

        def spare_tile_copy(t):
            return pltpu.make_async_copy(
                zero_ref, xs_ref.at[pl.ds(pl.multiple_of(t * tile_rows, tile_rows), tile_rows)], row_sem)

        n_used, n_tiles = tend_ref[N_EXPERTS - 1], xs_ref.shape[0] // tile_rows
        lax.fori_loop(n_used, n_tiles, lambda t, c: (spare_tile_copy(t).start(), c)[1], 0)
        lax.fori_loop(n_used, n_tiles, lambda t, c: (spare_tile_copy(t).wait(), c)[1], 0)

    _fetch_positions(pos_ref, idx_ref, idx_sem, pl.program_id(0))

    def issue(r, carry):
        for k in range(TOP_K):
            _slab_copy(h2_ref, r, xs_ref, idx_ref[k * tm + r], row_sem).start(priority=k % 2)
        return carry

    lax.fori_loop(0, tm, issue, 0, unroll=ISSUE_UNROLL)
    _wait_slabs(xs_ref, tm * TOP_K, row_sem)


def _dispatch(h2_slabs, pos_flat, tile_end, n_tiles):
    tm = TOKEN_TILE
    n = h2_slabs.shape[0] // SUBLANES
    hbm = pl.BlockSpec(memory_space=pl.ANY)
    grid_spec = pltpu.PrefetchScalarGridSpec(
        num_scalar_prefetch=1,
        grid=(n // tm,),
        in_specs=[pl.BlockSpec((tm * SUBLANES, LANES), lambda i, tend: (i, 0)), hbm],
        out_specs=hbm,
        scratch_shapes=[pltpu.SMEM((tm * TOP_K,), jnp.int32),
                        pltpu.VMEM((EXPERT_TILE * SUBLANES, LANES), F32),
                        pltpu.SemaphoreType.DMA, pltpu.SemaphoreType.DMA],
    )
    return pl.pallas_call(
        _dispatch_kernel,
        out_shape=jax.ShapeDtypeStruct((n_tiles * EXPERT_TILE * SUBLANES, LANES), F32),
        grid_spec=grid_spec,
        compiler_params=_params(("arbitrary",)),
        name="moe_dispatch",
    )(tile_end, h2_slabs, pos_flat)


def _expert_kernel(layer, te_ref, nxt_ref, par_ref, nu_ref, xs_ref, wgu_hbm, wd_hbm, *refs):
    bias_refs = refs[:2 * TILES_PER_STEP]
    ys_ref, wgu_buf, wd_buf, wgu_bf, wd_bf, w_sem = refs[2 * TILES_PER_STEP:]
    dff = wd_hbm.shape[-2]
    tile_rows = EXPERT_TILE * SUBLANES

    def weight_copies(e, slot):
        return (pltpu.make_async_copy(wgu_hbm.at[layer, e], wgu_buf.at[slot], w_sem.at[slot]),
                pltpu.make_async_copy(wd_hbm.at[layer, e], wd_buf.at[slot], w_sem.at[2 + slot]))

    for half in range(TILES_PER_STEP):
        t = pl.program_id(0) * TILES_PER_STEP + half
        xs_tile = xs_ref.at[pl.ds(half * tile_rows, tile_rows)]
        ys_tile = ys_ref.at[pl.ds(half * tile_rows, tile_rows)]
        bgu_ref, bd_ref = bias_refs[2 * half:2 * half + 2]

        @pl.when(t < nu_ref[0])
        def _(t=t, xs_tile=xs_tile, ys_tile=ys_tile, bgu_ref=bgu_ref, bd_ref=bd_ref):
            prev = te_ref[jnp.maximum(t - 1, 0)]

            @pl.when((t == 0) | (te_ref[t] != prev))
            def _():
                slot = par_ref[t]

                @pl.when(t == 0)
                def _():
                    for copy in weight_copies(te_ref[0], slot):
                        copy.start()

                for copy in weight_copies(te_ref[t], slot):
                    copy.wait()

                @pl.when(nxt_ref[t] >= 0)
                def _():
                    for copy in weight_copies(nxt_ref[t], 1 - slot):
                        copy.start()

                wgu_bf[...] = wgu_buf[slot].astype(BF16)
                wd_bf[...] = wd_buf[slot].astype(BF16)

            bias_row = pl.ds(te_ref[t] % SUBLANES, 1)
            x = _load_slabs(xs_tile, EXPERT_TILE)
            gu = _dot(x.astype(BF16), wgu_bf[...]) + bgu_ref[bias_row, :]
            x_glu = jnp.minimum(gu[:, :dff], SWIGLU_LIMIT)
            x_lin = jnp.clip(gu[:, dff:], -SWIGLU_LIMIT, SWIGLU_LIMIT)
            act = x_glu * jax.nn.sigmoid(SWIGLU_ALPHA * x_glu) * (x_lin + 1.0)
            _store_slabs(ys_tile, _dot(act.astype(BF16), wd_bf[...]) + bd_ref[bias_row, :])

        @pl.when(t >= nu_ref[0])
        def _(ys_tile=ys_tile):
            ys_tile[...] = jnp.zeros(ys_tile.shape, F32)


def _experts(layer, tile_expert, next_expert, parity, n_used, xs, w_gu, b_gu, w_down, b_down):
    tm = EXPERT_TILE
    d, dff2 = w_gu.shape[-2:]
    dff = w_down.shape[-2]
    step_rows = TILES_PER_STEP * tm * SUBLANES
    n_tiles = tile_expert.shape[0]
    assert n_tiles % TILES_PER_STEP == 0 and xs.shape[0] == n_tiles * tm * SUBLANES
    hbm = pl.BlockSpec(memory_space=pl.ANY)

    def bias_specs(half):
        group = lambda i, te: te[i * TILES_PER_STEP + half] // SUBLANES
        return [pl.BlockSpec((None, SUBLANES, dff2), lambda i, te, nx, pa, nu: (layer, group(i, te), 0)),
                pl.BlockSpec((None, SUBLANES, d), lambda i, te, nx, pa, nu: (layer, group(i, te), 0))]

    last_step = lambda nu: (nu[0] - 1) // TILES_PER_STEP
    grid_spec = pltpu.PrefetchScalarGridSpec(
        num_scalar_prefetch=4,
        grid=(n_tiles // TILES_PER_STEP,),
        in_specs=[pl.BlockSpec((step_rows, LANES), lambda i, te, nx, pa, nu: (jnp.minimum(i, last_step(nu)), 0)),
                  hbm, hbm] + [s for half in range(TILES_PER_STEP) for s in bias_specs(half)],
        out_specs=pl.BlockSpec((step_rows, LANES), lambda i, te, nx, pa, nu: (i, 0)),
        scratch_shapes=[pltpu.VMEM((2, d, dff2), F32), pltpu.VMEM((2, dff, d), F32),
                        pltpu.VMEM((d, dff2), BF16), pltpu.VMEM((dff, d), BF16),
                        pltpu.SemaphoreType.DMA((4,))],
    )
    biases = (b_gu, b_down) * TILES_PER_STEP
    return pl.pallas_call(
        functools.partial(_expert_kernel, layer),
        out_shape=jax.ShapeDtypeStruct(xs.shape, F32),
        grid_spec=grid_spec,
        compiler_params=_params(("arbitrary",), VMEM_LIMIT),
        name="moe_experts",
    )(tile_expert, next_expert, parity, n_used, xs, w_gu, w_down, *biases)


def _combine_kernel(final, tile0, x1_ref, gates_ref, mod_ref, fw_ref, pos_ref, ys_ref, out_ref,
                    idx_ref, rows_ref, idx_sem, row_sem):
    tm = x1_ref.shape[0]
    _fetch_positions(pos_ref, idx_ref, idx_sem, pl.program_id(0) + tile0)

    def issue(r, carry):
        for k in range(TOP_K):
            _slab_copy(ys_ref, idx_ref[k * tm + r], rows_ref.at[k], r, row_sem).start(priority=k % 2)
        return carry

    lax.fori_loop(0, tm, issue, 0, unroll=ISSUE_UNROLL)
    _wait_slabs(ys_ref, tm * TOP_K, row_sem)

    gates = gates_ref[...]
    moe = gates[:, 0:1] * _load_slabs(rows_ref, tm, (0,))
    for k in range(1, TOP_K):
        moe = moe + gates[:, k:k + 1] * _load_slabs(rows_ref, tm, (k,))
    x2 = x1_ref[...] + mod_ref[5:6, :] * moe
    if final:
        x2 = x2 * lax.rsqrt(jnp.mean(x2 * x2, axis=-1, keepdims=True) + EPS) * fw_ref[...]
    out_ref[...] = x2


def _combine(layer, final, row0, n_rows, x1, gates, mod, mod_row, final_w, pos_flat, ys):
    d = x1.shape[1]
    tm = TOKEN_TILE
    tile0 = row0 // tm
    hbm = pl.BlockSpec(memory_space=pl.ANY)
    return pl.pallas_call(
        functools.partial(_combine_kernel, final, tile0),
        out_shape=jax.ShapeDtypeStruct((n_rows, d), F32),
        grid=(n_rows // tm,),
        in_specs=[pl.BlockSpec((tm, d), lambda i: (tile0 + i, 0)),
                  pl.BlockSpec((tm, LANES), lambda i: (tile0 + i, 0)),
                  _mod_spec(mod, layer, mod_row, tile0),
                  pl.BlockSpec((1, d), lambda i: (0, 0)),
                  hbm, hbm],
        out_specs=pl.BlockSpec((tm, d), lambda i: (i, 0)),
        scratch_shapes=[pltpu.SMEM((tm * TOP_K,), jnp.int32),
                        pltpu.VMEM((TOP_K, tm * SUBLANES, LANES), F32),
                        pltpu.SemaphoreType.DMA, pltpu.SemaphoreType.DMA],
        compiler_params=_params(("arbitrary",), VMEM_LIMIT),
        name="moe_combine",
    )(x1, gates, mod, final_w.reshape(1, d), pos_flat, ys)


def _routing_tables(counts, n_tiles):
    tm = EXPERT_TILE
    c = counts[0, :N_EXPERTS].astype(jnp.int32)
    tiles = (c + tm - 1) // tm
    tile_end = jnp.cumsum(tiles)
    offs = (tile_end - tiles) * tm
    n_used = tile_end[-1]
    t = jnp.minimum(jnp.arange(n_tiles, dtype=jnp.int32), n_used - 1)
    te = jnp.minimum(jnp.sum((tile_end[None, :] <= t[:, None]).astype(jnp.int32), axis=1), N_EXPERTS - 1)
    e = jnp.arange(N_EXPERTS, dtype=jnp.int32)
    later = (e[None, :] > e[:, None]) & (tiles[None, :] > 0)
    nxt = jnp.min(jnp.where(later, e[None, :], N_EXPERTS), axis=1)
    nxt = jnp.where(nxt == N_EXPERTS, -1, nxt)
    parity = (jnp.cumsum((tiles > 0).astype(jnp.int32)) - 1) % 2
    return (offs, tile_end.astype(jnp.int32), te, nxt[te].astype(jnp.int32),
            parity[te].astype(jnp.int32), n_used.reshape(1).astype(jnp.int32))


def kernel(x_prompt, x_sample, cache_k, cache_v, state_lru, c, c_ctx, w_mod, b_mod, norm_mix_w, w_in, conf_conv_w, conf_conv_b, conf_ln_w, conf_ln_b, q_norm_w, k_norm_w, lru_conv_w, lru_conv_b, lru_wa, lru_ba, lru_wx, lru_bx, lru_lambda, w_out, norm_ffn_w, router_w, router_b, w_gu, b_gu, w_down, b_down, final_norm_w):
    batch, seq, d = x_prompt.shape
    dec_batch, dec_seq, _ = x_sample.shape
    depth = w_mod.shape[0]
    n_ctx, n_lat = batch * seq, dec_batch * dec_seq
    n = n_ctx + n_lat
    conf_w = conf_conv_w.shape[-1]
    lru_w = lru_conv_w.shape[-1]
    kv_w = N_KV_HEADS * HEAD_DIM
    attn_w = N_HEADS * HEAD_DIM
    widths = (2 * conf_w, attn_w, kv_w, kv_w, lru_w, lru_w)
    past = cache_k.shape[2]

    ctx_tiles = n_ctx // TOKEN_TILE
    lat_tiles_per_seq = dec_seq // TOKEN_TILE
    mod_row = lambda i: jnp.where(i < ctx_tiles, 0, 1 + (i - ctx_tiles) // lat_tiles_per_seq)

    n_cond = 8
    cvec = jnp.zeros((n_cond, d), F32).at[0].set(c_ctx).at[1:1 + dec_batch].set(c)
    mod = _modulation(cvec, w_mod, b_mod).reshape(depth, n_cond, 6, d)

    x = (x_prompt.reshape(n_ctx, d), x_sample.reshape(n_lat, d))
    rope = _rope_tables(dec_seq)
    cache_k4 = cache_k.reshape(dec_batch, depth, past, kv_w)
    cache_v4 = cache_v.reshape(dec_batch, depth, past, kv_w)
    h0_ctx = jnp.zeros((batch, 1, 2, lru_w), F32)
    assert d == SUBLANES * LANES, "row tables are moved as one (8, 128) tile per token"
    assert N_EXPERTS % SUBLANES == 0, "expert biases are fetched in blocks of 8 experts"
    n_sorted_tiles = n * TOP_K // EXPERT_TILE + N_EXPERTS

    row = lambda a: a.reshape(depth, 1, -1)
    p = {
        'conf_conv_w': conf_conv_w, 'conf_conv_b': row(conf_conv_b),
        'conf_ln_w': row(conf_ln_w), 'conf_ln_b': row(conf_ln_b),
        'lru_conv_w': lru_conv_w, 'lru_conv_b': row(lru_conv_b), 'lru_wa': lru_wa, 'lru_wx': lru_wx,
        'lru_bg': jnp.concatenate([row(lru_ba), row(lru_bx)], axis=-1), 'lru_lam': row(lru_lambda),
    }
    norm_mix, norm_ffn = row(norm_mix_w), row(norm_ffn_w)
    qw, kw = row(q_norm_w), row(k_norm_w)
    qw8, kw2 = jnp.tile(qw, (1, 1, N_HEADS)), jnp.tile(kw, (1, 1, N_KV_HEADS))
    router_w128 = jnp.pad(router_w, ((0, 0), (0, 0), (0, LANES - N_EXPERTS)))
    router_b128 = row(jnp.pad(router_b, ((0, 0), (0, LANES - N_EXPERTS))))

    new_k, new_v, new_h = [], [], []
    for l in range(depth):
        conf_u, q, k, v, lru_x, lru_g = _pre_mixer(l, x, mod, mod_row, norm_mix, w_in, widths)

        conf_c, lru_c, h_last = _seq_mixers(l, seq, batch, 0, conf_u, lru_x, lru_g, h0_ctx, 0, p)
        conf_l, lru_l, _ = _seq_mixers(l, dec_seq, dec_batch, n_ctx, conf_u, lru_x, lru_g, state_lru, l, p)

        attn_c, k_ctx = _attention_ctx(l, seq, batch, q, k, v, qw, kw)
        attn_l = _attention_lat(dec_seq, dec_batch, n_ctx, l, q, k, v, cache_k4, cache_v4, qw8, kw2, rope)
        new_k.append(k_ctx.reshape(batch, seq, N_KV_HEADS, HEAD_DIM))
        new_v.append(v[:n_ctx].reshape(batch, seq, N_KV_HEADS, HEAD_DIM))
        new_h.append(h_last)

        x1, h2, topi, gates, rank, counts = _post_mixer(
            l, x, (conf_c, attn_c, lru_c), (conf_l, attn_l, lru_l), mod, mod_row, w_out, norm_ffn,
            router_w128, router_b128)
        offs, tile_end, tile_expert, next_expert, parity, n_used = _routing_tables(counts, n_sorted_tiles)
        pos = _positions(topi, rank, offs)
        xs = _dispatch(h2, pos, tile_end, n_sorted_tiles)
        ys = _experts(l, tile_expert, next_expert, parity, n_used, xs, w_gu, b_gu, w_down, b_down)
        x = tuple(_combine(l, l == depth - 1, row0, rows, x1, gates, mod, mod_row, final_norm_w, pos, ys)
                  for row0, rows in ((0, n_ctx), (n_ctx, n_lat)))

    y_prompt = x[0].reshape(batch, seq, d)
    y_sample = x[1].reshape(dec_batch, dec_seq, d)
    return (y_prompt, y_sample, jnp.stack(new_k, axis=1), jnp.stack(new_v, axis=1), jnp.stack(new_h, axis=1))
```

```python
import functools

import jax
import jax.numpy as jnp
import numpy as np
from jax import lax
from jax.experimental import pallas as pl
from jax.experimental.pallas import tpu as pltpu

F32 = jnp.float32
BF16 = jnp.bfloat16

HEAD_DIM = 64
N_HEADS = 8
N_KV_HEADS = 2
HEADS_PER_KV = N_HEADS // N_KV_HEADS
CONF_KERNEL = 31
LRU_CONV = 4
LRU_C = 8.0
N_EXPERTS = 32
TOP_K = 4
SWIGLU_ALPHA = 1.702
SWIGLU_LIMIT = 7.0
ROPE_THETA = 10000.0
GRID_W = 64
EPS = 1e-6

LANES = 128
SUBLANES = 8
ISSUE_UNROLL = 4
TOKEN_TILE = 256
EXPERT_TILE = 256
TILES_PER_STEP = 2
CONV_HALO = 16
ROW_CHUNK = 64
SCAN_CHUNK = 32
VMEM_LIMIT = 56 * 1024 * 1024


def _params(sem, vmem=None):
    return pltpu.CompilerParams(dimension_semantics=sem, vmem_limit_bytes=vmem)


def _split_bf16(x):
    hi = x.astype(BF16)
    lo = (x - hi.astype(F32)).astype(BF16)
    return hi, lo


def _dot(a, b):
    return jnp.dot(a, b, preferred_element_type=F32)


def _dot3(a, b):
    a_hi, a_lo = _split_bf16(a)
    b_hi, b_lo = _split_bf16(b)
    return _dot(a_hi, b_hi) + (_dot(a_hi, b_lo) + _dot(a_lo, b_hi))


def _mod_kernel(c_ref, w_ref, b_ref, o_ref):
    c = c_ref[...]
    s = c * jax.nn.sigmoid(c)
    o_ref[...] = _dot3(s, w_ref[...]) + b_ref[...]


def _modulation(cvec, w_mod, b_mod):
    depth, d, d6 = w_mod.shape
    tn = 768
    return pl.pallas_call(
        _mod_kernel,
        out_shape=jax.ShapeDtypeStruct((depth, cvec.shape[0], d6), F32),
        grid=(depth, d6 // tn),
        in_specs=[
            pl.BlockSpec(cvec.shape, lambda l, j: (0, 0)),
            pl.BlockSpec((None, d, tn), lambda l, j: (l, 0, j)),
            pl.BlockSpec((None, 1, tn), lambda l, j: (l, 0, j)),
        ],
        out_specs=pl.BlockSpec((None, cvec.shape[0], tn), lambda l, j: (l, 0, j)),
        compiler_params=_params(("arbitrary", "arbitrary")),
        name="modulation",
    )(cvec, w_mod, b_mod.reshape(depth, 1, d6))


def _pre_kernel(ctx_tiles, xc_ref, xl_ref, mod_ref, nw_ref, win_f32_ref, conf_ref, q_ref, k_ref, v_ref, lx_ref,
                lg_ref, win_ref):
    @pl.when(pl.program_id(0) == 0)
    def _():
        win_ref[...] = win_f32_ref[...].astype(BF16)

    x = jnp.where(pl.program_id(0) < ctx_tiles, xc_ref[...], xl_ref[...])
    m = mod_ref[...]
    shift, scale = m[0:1], m[1:2]
    h = x * lax.rsqrt(jnp.mean(x * x, axis=-1, keepdims=True) + EPS) * nw_ref[...]
    h = h * (1.0 + scale) + shift
    proj = _dot(h.astype(BF16), win_ref[...])
    col = 0
    for ref in (conf_ref, q_ref, k_ref, v_ref, lx_ref, lg_ref):
        w = ref.shape[-1]
        ref[...] = proj[:, col:col + w]
        col += w


def _path_specs(x_pair, tm):
    ctx_tiles = x_pair[0].shape[0] // tm
    return ctx_tiles, [
        pl.BlockSpec((tm, x_pair[0].shape[-1]), lambda i: (jnp.minimum(i, ctx_tiles - 1), 0)),
        pl.BlockSpec((tm, x_pair[1].shape[-1]), lambda i: (jnp.maximum(i - ctx_tiles, 0), 0))]


def _layer_spec(a, layer):
    zeros = (0,) * (a.ndim - 1)
    return pl.BlockSpec((None,) + a.shape[1:], lambda *_: (layer,) + zeros)


def _mod_spec(mod, layer, mod_row, tile0=0):
    return pl.BlockSpec((None, None) + mod.shape[2:], lambda i, *_: (layer, mod_row(tile0 + i), 0, 0))


def _pre_mixer(layer, x_pair, mod, mod_row, norm_w, w_in, widths):
    n, d = x_pair[0].shape[0] + x_pair[1].shape[0], x_pair[0].shape[1]
    ctx_tiles, x_specs = _path_specs(x_pair, TOKEN_TILE)
    return pl.pallas_call(
        functools.partial(_pre_kernel, ctx_tiles),
        out_shape=[jax.ShapeDtypeStruct((n, w), F32) for w in widths],
        grid=(n // TOKEN_TILE,),
        in_specs=x_specs + [_mod_spec(mod, layer, mod_row), _layer_spec(norm_w, layer),
                            _layer_spec(w_in, layer)],
        out_specs=[pl.BlockSpec((TOKEN_TILE, w), lambda i: (i, 0)) for w in widths],
        scratch_shapes=[pltpu.VMEM(w_in.shape[1:], BF16)],
        compiler_params=_params(("arbitrary",), VMEM_LIMIT),
        name="pre_mixer",
    )(*x_pair, mod, norm_w, w_in)


def _gelu_tanh(x):
    return 0.5 * x * (1.0 + jnp.tanh(0.7978845608028654 * (x + 0.044715 * (x * x * x))))


def _seq_kernel(seq_len, conf_ref, lx_ref, lg_ref, h0_ref, ccw_ref, ccb_ref, lnw_ref, lnb_ref,
                lcw_ref, lcb_ref, wa_ref, wx_ref, bg_ref, lam_ref,
                co_ref, lo_ref, hl_ref, pad_ref, rot_ref, af_ref, uf_ref, ab_ref, ub_ref, wg_ref):
    L = seq_len
    W = co_ref.shape[-1]
    pad = L // 2
    zeros_halo = jnp.zeros((CONV_HALO, W), F32)

    @pl.when(pl.program_id(0) == 0)
    def _():
        dirs, heads, blk, _ = wa_ref.shape
        zero_blk = jnp.zeros((blk, blk), F32)
        for h in range(heads):
            cols = [w_ref[d, h] if g == h else zero_blk
                    for w_ref in (wa_ref, wx_ref) for d in range(dirs) for g in range(heads)]
            wg_ref[h * blk:(h + 1) * blk, :] = jnp.concatenate(cols, axis=-1).astype(BF16)

    pad_ref[0:CONV_HALO, :] = zeros_halo
    pad_ref[CONV_HALO + L:2 * CONV_HALO + L, :] = zeros_halo
    for c in range(L // ROW_CHUNK):
        r0 = c * ROW_CHUNK
        u = conf_ref[r0:r0 + ROW_CHUNK, :]
        pad_ref[CONV_HALO + r0:CONV_HALO + r0 + ROW_CHUNK, :] = u[:, :W] * jax.nn.sigmoid(u[:, W:])
    shifted_rows = L + 2 * CONV_HALO - SUBLANES
    for s in range(1, SUBLANES):
        for r0 in range(0, shifted_rows, ROW_CHUNK):
            rows = min(ROW_CHUNK, shifted_rows - r0)
            rot_ref[s - 1, r0:r0 + rows, :] = pad_ref[r0 + s:r0 + s + rows, :]
    left = CONF_KERNEL // 2
    for c in range(L // ROW_CHUNK):
        r0 = c * ROW_CHUNK
        acc = jnp.zeros((ROW_CHUNK, W), F32) + ccb_ref[...]
        for k in range(CONF_KERNEL):
            whole, s = divmod(CONV_HALO - left + k, SUBLANES)
            start = whole * SUBLANES + r0
            src = pad_ref if s == 0 else rot_ref.at[s - 1]
            acc = acc + ccw_ref[k:k + 1, :] * src[start:start + ROW_CHUNK, :]
        mu = jnp.mean(acc, axis=-1, keepdims=True)
        cen = acc - mu
        var = jnp.mean(cen * cen, axis=-1, keepdims=True)
        y = cen * lax.rsqrt(var + EPS) * lnw_ref[...] + lnb_ref[...]
        co_ref[r0:r0 + ROW_CHUNK, :] = y * jax.nn.sigmoid(y)

    for c in range(L // ROW_CHUNK):
        r0 = c * ROW_CHUNK
        pad_ref[CONV_HALO + r0:CONV_HALO + r0 + ROW_CHUNK, :] = lx_ref[r0:r0 + ROW_CHUNK, :]
    ones_pad = jnp.ones((pad, W), F32)
    zeros_pad = jnp.zeros((pad, W), F32)
    af_ref[0:pad, :] = ones_pad
    uf_ref[0:pad, :] = zeros_pad
    ab_ref[L:L + pad, :] = ones_pad
    ub_ref[L:L + pad, :] = zeros_pad
    sp = jax.nn.softplus(-lam_ref[...])
    h0 = h0_ref[...]
    left = LRU_CONV // 2
    n_chunks = L // ROW_CHUNK
    for c in range(n_chunks):
        r0 = c * ROW_CHUNK
        xc = jnp.zeros((ROW_CHUNK, W), F32) + lcb_ref[...]
        for k in range(LRU_CONV):
            start = CONV_HALO - left + k + r0
            xc = xc + lcw_ref[k:k + 1, :] * pad_ref[start:start + ROW_CHUNK, :]
        g = _dot(xc.astype(BF16), wg_ref[...]) + bg_ref[...]
        r = jax.nn.sigmoid(g[:, :2 * W])
        i = jax.nn.sigmoid(g[:, 2 * W:])
        a = jnp.exp((-LRU_C) * r * sp)
        xc2 = jnp.concatenate([xc, xc], axis=-1)
        u = jnp.sqrt(1.0 - a * a) * i * xc2
        a_f, a_b, u_f, u_b = a[:, :W], a[:, W:], u[:, :W], u[:, W:]
        row = lax.broadcasted_iota(jnp.int32, (ROW_CHUNK, W), 0)
        if c == 0:
            u_f = jnp.where(row == 0, u_f + a_f * h0[0:1], u_f)
        if c == n_chunks - 1:
            u_b = jnp.where(row == ROW_CHUNK - 1, u_b + a_b * h0[1:2], u_b)
        af_ref[pad + r0:pad + r0 + ROW_CHUNK, :] = a_f
        uf_ref[pad + r0:pad + r0 + ROW_CHUNK, :] = u_f
        ab_ref[r0:r0 + ROW_CHUNK, :] = a_b
        ub_ref[r0:r0 + ROW_CHUNK, :] = u_b

    n_sc = L // SCAN_CHUNK
    s = 1
    while s < L:
        for c in reversed(range(n_sc)):
            r0 = c * SCAN_CHUNK
            if r0 + SCAN_CHUNK <= s:
                continue
            cur = slice(pad + r0, pad + r0 + SCAN_CHUNK)
            sh = slice(pad + r0 - s, pad + r0 - s + SCAN_CHUNK)
            a_cur = af_ref[cur, :]
            uf_ref[cur, :] = uf_ref[cur, :] + a_cur * uf_ref[sh, :]
            af_ref[cur, :] = a_cur * af_ref[sh, :]
        for c in range(n_sc):
            r0 = c * SCAN_CHUNK
            if r0 >= L - s:
                continue
            cur = slice(r0, r0 + SCAN_CHUNK)
            sh = slice(r0 + s, r0 + s + SCAN_CHUNK)
            a_cur = ab_ref[cur, :]
            ub_ref[cur, :] = ub_ref[cur, :] + a_cur * ub_ref[sh, :]
            ab_ref[cur, :] = a_cur * ab_ref[sh, :]
        s *= 2

    for c in range(n_chunks):
        r0 = c * ROW_CHUNK
        h = uf_ref[pad + r0:pad + r0 + ROW_CHUNK, :] + ub_ref[r0:r0 + ROW_CHUNK, :]
        lo_ref[r0:r0 + ROW_CHUNK, :] = h * _gelu_tanh(lg_ref[r0:r0 + ROW_CHUNK, :])
    hl_ref[0:1, :] = uf_ref[pad + L - 1:pad + L, :]
    hl_ref[1:2, :] = ub_ref[0:1, :]


def _seq_mixers(layer, seq_len, n_seq, row0, conf_u, lru_x, lru_g, h0, h0_layer, p):
    w = lru_x.shape[-1]
    b0 = row0 // seq_len
    pad = seq_len // 2
    in_spec = lambda width: pl.BlockSpec((seq_len, width), lambda i: (b0 + i, 0))
    out_spec = pl.BlockSpec((seq_len, w), lambda i: (i, 0))
    h0_spec = pl.BlockSpec((None, None, 2, w), lambda i: (i, h0_layer, 0, 0))
    weights = (p['conf_conv_w'], p['conf_conv_b'], p['conf_ln_w'], p['conf_ln_b'],
               p['lru_conv_w'], p['lru_conv_b'], p['lru_wa'], p['lru_wx'], p['lru_bg'], p['lru_lam'])
    padded = seq_len + 2 * CONV_HALO
    return pl.pallas_call(
        functools.partial(_seq_kernel, seq_len),
        out_shape=[jax.ShapeDtypeStruct((n_seq * seq_len, w), F32)] * 2
        + [jax.ShapeDtypeStruct((n_seq, 2, w), F32)],
        grid=(n_seq,),
        in_specs=[in_spec(2 * w), in_spec(w), in_spec(w), h0_spec] + [_layer_spec(a, layer) for a in weights],
        out_specs=[out_spec, out_spec, pl.BlockSpec((None, 2, w), lambda i: (i, 0, 0))],
        scratch_shapes=[pltpu.VMEM((padded, w), F32), pltpu.VMEM((SUBLANES - 1, padded, w), F32)]
        + [pltpu.VMEM((seq_len + pad, w), F32)] * 4 + [pltpu.VMEM((w, 4 * w), BF16)],
        compiler_params=_params(("arbitrary",), VMEM_LIMIT),
        name=f"seq_mixers_{seq_len}",
    )(conf_u, lru_x, lru_g, h0, *weights)


def _head_rms(x):
    return lax.rsqrt(jnp.mean(x * x, axis=-1, keepdims=True) + EPS)


def _swap_halves(x):
    width = x.shape[-1]
    lane = lax.broadcasted_iota(jnp.int32, x.shape, x.ndim - 1)
    up = pltpu.roll(x, width - HEAD_DIM // 4, x.ndim - 1)
    down = pltpu.roll(x, HEAD_DIM // 4, x.ndim - 1)
    return jnp.where((lane % (HEAD_DIM // 2)) < HEAD_DIM // 4, up, down)


def _attend(q_heads, k_bf16, v_bf16):
    assert HEAD_DIM == 64, "1/sqrt(head_dim) is a power of two, so scaling q first is exact"
    s = lax.dot_general((q_heads * HEAD_DIM ** -0.5).astype(BF16), k_bf16, (((1,), (1,)), ((), ())),
                        preferred_element_type=F32)
    p = jnp.exp(s - jnp.max(s, axis=-1, keepdims=True))
    denom = jnp.sum(p, axis=-1, keepdims=True)
    return _dot(p.astype(BF16), v_bf16) / denom


def _attn_ctx_kernel(q_ref, k_ref, v_ref, qw_ref, kw_ref, o_ref, kn_ref):
    L = q_ref.shape[0]
    q, k, v = q_ref[...], k_ref[...], v_ref[...]
    qw, kw = qw_ref[...], kw_ref[...]
    k_out, o_out = [], []
    for g in range(N_KV_HEADS):
        kh = k[:, g * HEAD_DIM:(g + 1) * HEAD_DIM]
        kh = kh * _head_rms(kh) * kw
        k_out.append(kh)
        qs = []
        for j in range(HEADS_PER_KV):
            h = g * HEADS_PER_KV + j
            qh = q[:, h * HEAD_DIM:(h + 1) * HEAD_DIM]
            qs.append(qh * _head_rms(qh) * qw)
        o = _attend(jnp.concatenate(qs, axis=0), kh.astype(BF16),
                    v[:, g * HEAD_DIM:(g + 1) * HEAD_DIM].astype(BF16))
        o_out += [o[j * L:(j + 1) * L] for j in range(HEADS_PER_KV)]
    kn_ref[...] = jnp.concatenate(k_out, axis=-1)
    o_ref[...] = jnp.concatenate(o_out, axis=-1)


def _attention_ctx(layer, seq_len, n_seq, q, k, v, qw, kw):
    kvw = k.shape[-1]
    row_spec = lambda width: pl.BlockSpec((seq_len, width), lambda i: (i, 0))
    return pl.pallas_call(
        _attn_ctx_kernel,
        out_shape=[jax.ShapeDtypeStruct((n_seq * seq_len, q.shape[-1]), F32),
                   jax.ShapeDtypeStruct((n_seq * seq_len, kvw), F32)],
        grid=(n_seq,),
        in_specs=[row_spec(q.shape[-1]), row_spec(kvw), row_spec(kvw), _layer_spec(qw, layer),
                  _layer_spec(kw, layer)],
        out_specs=[row_spec(q.shape[-1]), row_spec(kvw)],
        compiler_params=_params(("arbitrary",), VMEM_LIMIT),
        name="attention_ctx",
    )(q, k, v, qw, kw)


def _attn_lat_kernel(q_ref, k_ref, v_ref, ck_ref, cv_ref, qw_ref, kw_ref, cq_ref, sq_ref, ck_t_ref,
                     sk_t_ref, o_ref, kall_ref, vall_ref):
    L = k_ref.shape[0]
    tq = q_ref.shape[0]

    @pl.when(pl.program_id(1) == 0)
    def _():
        k = k_ref[...]
        t = k * kw_ref[...]
        rot = t * ck_t_ref[...] + _swap_halves(t) * sk_t_ref[...]
        parts = []
        for g in range(N_KV_HEADS):
            sl = slice(g * HEAD_DIM, (g + 1) * HEAD_DIM)
            parts.append(rot[:, sl] * _head_rms(k[:, sl]))
        kall_ref[0:L, :] = jnp.concatenate(parts, axis=-1).astype(BF16)
        kall_ref[L:, :] = ck_ref[...].astype(BF16)
        vall_ref[0:L, :] = v_ref[...].astype(BF16)
        vall_ref[L:, :] = cv_ref[...].astype(BF16)

    q = q_ref[...]
    t = q * qw_ref[...]
    rot = t * cq_ref[...] + _swap_halves(t) * sq_ref[...]
    o_out = []
    for g in range(N_KV_HEADS):
        qs = []
        for j in range(HEADS_PER_KV):
            sl = slice((g * HEADS_PER_KV + j) * HEAD_DIM, (g * HEADS_PER_KV + j + 1) * HEAD_DIM)
            qs.append(rot[:, sl] * _head_rms(q[:, sl]))
        sl = slice(g * HEAD_DIM, (g + 1) * HEAD_DIM)
        o = _attend(jnp.concatenate(qs, axis=0), kall_ref[:, sl], vall_ref[:, sl])
        o_out += [o[j * tq:(j + 1) * tq] for j in range(HEADS_PER_KV)]
    o_ref[...] = jnp.concatenate(o_out, axis=-1)


def _attention_lat(seq_len, n_seq, row0, layer, q, k, v, cache_k, cache_v, qw8, kw2, rope):
    tq = TOKEN_TILE
    nq = seq_len // tq
    qwid, kvw = q.shape[-1], k.shape[-1]
    past = cache_k.shape[2]
    b0q = row0 // tq
    b0s = row0 // seq_len
    cq, sq, ck, sk = rope
    full = lambda a: pl.BlockSpec(a.shape, lambda b, j: (0,) * a.ndim)
    seq_spec = pl.BlockSpec((seq_len, kvw), lambda b, j: (b0s + b, 0))
    cache_spec = pl.BlockSpec((None, None, past, kvw), lambda b, j: (b, layer, 0, 0))
    q_spec = pl.BlockSpec((tq, qwid), lambda b, j: (b0q + b * nq + j, 0))
    rope_q_spec = pl.BlockSpec((tq, qwid), lambda b, j: (j, 0))
    return pl.pallas_call(
        _attn_lat_kernel,
        out_shape=jax.ShapeDtypeStruct((n_seq * seq_len, qwid), F32),
        grid=(n_seq, nq),
        in_specs=[q_spec, seq_spec, seq_spec, cache_spec, cache_spec, _layer_spec(qw8, layer),
                  _layer_spec(kw2, layer), rope_q_spec, rope_q_spec, full(ck), full(sk)],
        out_specs=pl.BlockSpec((tq, qwid), lambda b, j: (b * nq + j, 0)),
        scratch_shapes=[pltpu.VMEM((seq_len + past, kvw), BF16)] * 2,
        compiler_params=_params(("arbitrary", "arbitrary"), VMEM_LIMIT),
        name="attention_lat",
    )(q, k, v, cache_k, cache_v, qw8, kw2, cq, sq, ck, sk)


def _rope_tables(seq_len):
    t = np.arange(seq_len)
    row = (t // GRID_W).astype(np.float32)
    col = (t % GRID_W).astype(np.float32)
    half = HEAD_DIM // 2
    freqs = (np.float32(ROPE_THETA) ** (-np.arange(0, half, 2, dtype=np.float32) / np.float32(half)))
    ang_r, ang_c = row[:, None] * freqs, col[:, None] * freqs
    cos = np.concatenate([np.cos(ang_r)] * 2 + [np.cos(ang_c)] * 2, axis=-1)
    sin = np.concatenate([-np.sin(ang_r), np.sin(ang_r), -np.sin(ang_c), np.sin(ang_c)], axis=-1)
    return tuple(jnp.asarray(a, F32) for a in
                 (np.tile(cos, (1, N_HEADS)), np.tile(sin, (1, N_HEADS)),
                  np.tile(cos, (1, N_KV_HEADS)), np.tile(sin, (1, N_KV_HEADS))))


def _slab_copy(src_ref, src_tok, dst_ref, dst_tok, sem):
    src = src_ref.at[pl.ds(pl.multiple_of(src_tok * SUBLANES, SUBLANES), SUBLANES)]
    dst = dst_ref.at[pl.ds(pl.multiple_of(dst_tok * SUBLANES, SUBLANES), SUBLANES)]
    return pltpu.make_async_copy(src, dst, sem)


def _load_slabs(ref, n_tok, lead=()):
    return jnp.concatenate(
        [ref[(*lead, pl.ds(j, n_tok, stride=SUBLANES), slice(None))] for j in range(SUBLANES)], axis=-1)


def _store_slabs(ref, val):
    for j in range(SUBLANES):
        ref[pl.ds(j, val.shape[0], stride=SUBLANES), :] = val[:, j * LANES:(j + 1) * LANES]


def _wait_slabs(hbm_ref, n_tok, sem):
    span = hbm_ref.at[pl.ds(0, n_tok * SUBLANES)]
    pltpu.make_async_copy(span, span, sem).wait()


def _fetch_positions(pos_ref, idx_ref, sem, tile):
    n = idx_ref.shape[0]
    copy = pltpu.make_async_copy(pos_ref.at[pl.ds(tile * n, n)], idx_ref, sem)
    copy.start()
    copy.wait()


def _choice_major(x):
    t = jnp.transpose(x)
    return jnp.concatenate([t[k:k + 1, :] for k in range(TOP_K)], axis=1)


def _post_kernel(ctx_tiles, xc_ref, xl_ref, co_c_ref, ao_c_ref, lo_c_ref, co_l_ref, ao_l_ref, lo_l_ref, mod_ref,
                 wout_f32_ref, nw_ref, rw_ref, rb_ref,
                 x1_ref, h2_ref, topi_ref, gates_ref, rank_ref, counts_ref, carry_ref, wout_ref):
    tm = xc_ref.shape[0]
    cw, aw = co_c_ref.shape[-1], ao_c_ref.shape[-1]
    is_ctx = pl.program_id(0) < ctx_tiles

    @pl.when(pl.program_id(0) == 0)
    def _():
        carry_ref[...] = jnp.zeros_like(carry_ref)
        wout_ref[...] = wout_f32_ref[...].astype(BF16)

    m = mod_ref[...]
    gate1, shift2, scale2 = m[2:3], m[3:4], m[4:5]
    pick = lambda c_ref, l_ref: jnp.where(is_ctx, c_ref[...], l_ref[...]).astype(BF16)
    x = jnp.where(is_ctx, xc_ref[...], xl_ref[...])
    mixed = (_dot(pick(co_c_ref, co_l_ref), wout_ref[0:cw, :])
             + _dot(pick(ao_c_ref, ao_l_ref), wout_ref[cw:cw + aw, :])
             + _dot(pick(lo_c_ref, lo_l_ref), wout_ref[cw + aw:, :]))
    x1 = x + gate1 * mixed
    x1_ref[...] = x1
    h2 = x1 * lax.rsqrt(jnp.mean(x1 * x1, axis=-1, keepdims=True) + EPS) * nw_ref[...]
    h2 = h2 * (1.0 + scale2) + shift2
    _store_slabs(h2_ref, h2)

    lane = lax.broadcasted_iota(jnp.int32, (tm, LANES), 1)
    lane_f = lane.astype(F32)
    logits = jnp.where(lane < N_EXPERTS, _dot3(h2, rw_ref[...]) + rb_ref[...], -jnp.inf)
    top_v, onehots = [], []
    topi = jnp.zeros((tm, LANES), F32)
    for k in range(TOP_K):
        mx = jnp.max(logits, axis=-1, keepdims=True)
        idx = jnp.min(jnp.where(logits == mx, lane_f, float(LANES)), axis=-1, keepdims=True)
        hit = lane_f == idx
        logits = jnp.where(hit, -jnp.inf, logits)
        top_v.append(mx)
        onehots.append(hit)
        topi = jnp.where(lane == k, idx, topi)
    topi_ref[...] = _choice_major(topi).astype(jnp.int32)
    exps = [jnp.exp(v - top_v[0]) for v in top_v]
    denom = exps[0] + exps[1] + exps[2] + exps[3]
    gates = jnp.zeros((tm, LANES), F32)
    for k in range(TOP_K):
        gates = jnp.where(lane == k, exps[k] / denom, gates)
    gates_ref[...] = gates

    chosen = jnp.zeros((tm, LANES), F32)
    for hit in onehots:
        chosen = jnp.where(hit, 1.0, chosen)
    r_i = lax.broadcasted_iota(jnp.int32, (tm, tm), 0)
    c_i = lax.broadcasted_iota(jnp.int32, (tm, tm), 1)
    lower = jnp.where(c_i < r_i, 1.0, 0.0).astype(BF16)
    before = _dot(lower, chosen.astype(BF16)) + carry_ref[...]
    rank = jnp.zeros((tm, LANES), F32)
    for k, hit in enumerate(onehots):
        rk = jnp.sum(jnp.where(hit, before, 0.0), axis=-1, keepdims=True)
        rank = jnp.where(lane == k, rk, rank)
    rank_ref[...] = _choice_major(rank).astype(jnp.int32)
    carry = carry_ref[...] + jnp.sum(chosen, axis=0, keepdims=True)
    carry_ref[...] = carry
    counts_ref[...] = carry


def _post_mixer(layer, x_pair, ctx_outs, lat_outs, mod, mod_row, w_out, norm_w, router_w, router_b):
    n, d = x_pair[0].shape[0] + x_pair[1].shape[0], x_pair[0].shape[1]
    tm = TOKEN_TILE
    ctx_tiles, x_specs = _path_specs(x_pair, tm)
    row_spec = lambda width: pl.BlockSpec((tm, width), lambda i: (i, 0))
    ctx_spec = lambda a: pl.BlockSpec((tm, a.shape[-1]), lambda i: (jnp.minimum(i, ctx_tiles - 1), 0))
    lat_spec = lambda a: pl.BlockSpec((tm, a.shape[-1]), lambda i: (jnp.maximum(i - ctx_tiles, 0), 0))
    lane_tile = pl.BlockSpec((tm, LANES), lambda i: (i, 0))
    flat_tile = pl.BlockSpec((None, 1, TOP_K * tm), lambda i: (i, 0, 0))
    flat_shape = jax.ShapeDtypeStruct((n // tm, 1, TOP_K * tm), jnp.int32)
    return pl.pallas_call(
        functools.partial(_post_kernel, ctx_tiles),
        out_shape=[jax.ShapeDtypeStruct((n, d), F32), jax.ShapeDtypeStruct((n * SUBLANES, LANES), F32),
                   flat_shape, jax.ShapeDtypeStruct((n, LANES), F32),
                   flat_shape, jax.ShapeDtypeStruct((1, LANES), F32)],
        grid=(n // tm,),
        in_specs=x_specs + [ctx_spec(a) for a in ctx_outs] + [lat_spec(a) for a in lat_outs]
        + [_mod_spec(mod, layer, mod_row)]
        + [_layer_spec(a, layer) for a in (w_out, norm_w, router_w, router_b)],
        out_specs=[row_spec(d), pl.BlockSpec((tm * SUBLANES, LANES), lambda i: (i, 0)),
                   flat_tile, lane_tile, flat_tile,
                   pl.BlockSpec((1, LANES), lambda i: (0, 0))],
        scratch_shapes=[pltpu.VMEM((1, LANES), F32), pltpu.VMEM(w_out.shape[1:], BF16)],
        compiler_params=_params(("arbitrary",), VMEM_LIMIT),
        name="post_mixer",
    )(*x_pair, *ctx_outs, *lat_outs, mod, w_out, norm_w, router_w, router_b)


def _pos_kernel(offs_ref, topi_ref, rank_ref, pos_ref):
    topi = topi_ref[...]
    pos = rank_ref[...]
    for e in range(N_EXPERTS):
        pos = pos + jnp.where(topi == e, offs_ref[e], 0)
    pos_ref[...] = pos


def _positions(topi_flat, rank_flat, offs):
    tiles, _, per_tile = topi_flat.shape
    full = pl.BlockSpec((tiles, per_tile), lambda i, offs: (0, 0))
    pos = pl.pallas_call(
        _pos_kernel,
        out_shape=jax.ShapeDtypeStruct((tiles, per_tile), jnp.int32),
        grid_spec=pltpu.PrefetchScalarGridSpec(num_scalar_prefetch=1, grid=(1,), in_specs=[full, full],
                                               out_specs=full),
        compiler_params=_params(("arbitrary",)),
        name="moe_positions",
    )(offs, topi_flat.reshape(tiles, per_tile), rank_flat.reshape(tiles, per_tile))
    return pos.reshape(-1)


def _dispatch_kernel(tend_ref, h2_ref, pos_ref, xs_ref, idx_ref, zero_ref, idx_sem, row_sem):
    tm = h2_ref.shape[0] // SUBLANES
    tile_rows = EXPERT_TILE * SUBLANES

    @pl.when(pl.program_id(0) == 0)
    def _():
        zero_ref[...] = jnp.zeros_like(zero_ref)

        def last_tile_copy(e):
            start = pl.multiple_of((tend_ref[e] - 1) * tile_rows, tile_rows)
            return pltpu.make_async_copy(zero_ref, xs_ref.at[pl.ds(start, tile_rows)], row_sem)

        def has_tiles(e):
            return tend_ref[e] > (tend_ref[e - 1] if e else 0)

        for e in range(N_EXPERTS):
            pl.when(has_tiles(e))(lambda e=e: last_tile_copy(e).start())
        for e in range(N_EXPERTS):
            pl.when(has_tiles(e))(lambda e=e: last_tile_copy(e).wait())

        def spare_tile_copy(t):
            return pltpu.make_async_copy(
                zero_ref, xs_ref.at[pl.ds(pl.multiple_of(t * tile_rows, tile_rows), tile_rows)], row_sem)

        n_used, n_tiles = tend_ref[N_EXPERTS - 1], xs_ref.shape[0] // tile_rows
        lax.fori_loop(n_used, n_tiles, lambda t, c: (spare_tile_copy(t).start(), c)[1], 0)
        lax.fori_loop(n_used, n_tiles, lambda t, c: (spare_tile_copy(t).wait(), c)[1], 0)

    _fetch_positions(pos_ref, idx_ref, idx_sem, pl.program_id(0))

    def issue(r, carry):
        for k in range(TOP_K):
            _slab_copy(h2_ref, r, xs_ref, idx_ref[k * tm + r], row_sem).start(priority=k % 2)
        return carry

    lax.fori_loop(0, tm, issue, 0, unroll=ISSUE_UNROLL)
    _wait_slabs(xs_ref, tm * TOP_K, row_sem)


def _dispatch(h2_slabs, pos_flat, tile_end, n_tiles):
    tm = TOKEN_TILE
    n = h2_slabs.shape[0] // SUBLANES
    hbm = pl.BlockSpec(memory_space=pl.ANY)
    grid_spec = pltpu.PrefetchScalarGridSpec(
        num_scalar_prefetch=1,
        grid=(n // tm,),
        in_specs=[pl.BlockSpec((tm * SUBLANES, LANES), lambda i, tend: (i, 0)), hbm],
        out_specs=hbm,
        scratch_shapes=[pltpu.SMEM((tm * TOP_K,), jnp.int32),
                        pltpu.VMEM((EXPERT_TILE * SUBLANES, LANES), F32),
                        pltpu.SemaphoreType.DMA, pltpu.SemaphoreType.DMA],
    )
    return pl.pallas_call(
        _dispatch_kernel,
        out_shape=jax.ShapeDtypeStruct((n_tiles * EXPERT_TILE * SUBLANES, LANES), F32),
        grid_spec=grid_spec,
        compiler_params=_params(("arbitrary",)),
        name="moe_dispatch",
    )(tile_end, h2_slabs, pos_flat)


def _expert_kernel(layer, te_ref, nxt_ref, par_ref, nu_ref, xs_ref, wgu_hbm, wd_hbm, *refs):
    bias_refs = refs[:2 * TILES_PER_STEP]
    ys_ref, wgu_buf, wd_buf, wgu_bf, wd_bf, w_sem = refs[2 * TILES_PER_STEP:]
    dff = wd_hbm.shape[-2]
    tile_rows = EXPERT_TILE * SUBLANES

    def weight_copies(e, slot):
        return (pltpu.make_async_copy(wgu_hbm.at[layer, e], wgu_buf.at[slot], w_sem.at[slot]),
                pltpu.make_async_copy(wd_hbm.at[layer, e], wd_buf.at[slot], w_sem.at[2 + slot]))

    for half in range(TILES_PER_STEP):
        t = pl.program_id(0) * TILES_PER_STEP + half
        xs_tile = xs_ref.at[pl.ds(half * tile_rows, tile_rows)]
        ys_tile = ys_ref.at[pl.ds(half * tile_rows, tile_rows)]
        bgu_ref, bd_ref = bias_refs[2 * half:2 * half + 2]

        @pl.when(t < nu_ref[0])
        def _(t=t, xs_tile=xs_tile, ys_tile=ys_tile, bgu_ref=bgu_ref, bd_ref=bd_ref):
            prev = te_ref[jnp.maximum(t - 1, 0)]

            @pl.when((t == 0) | (te_ref[t] != prev))
            def _():
                slot = par_ref[t]

                @pl.when(t == 0)
                def _():
                    for copy in weight_copies(te_ref[0], slot):
                        copy.start()

                for copy in weight_copies(te_ref[t], slot):
                    copy.wait()

                @pl.when(nxt_ref[t] >= 0)
                def _():
                    for copy in weight_copies(nxt_ref[t], 1 - slot):
                        copy.start()

                wgu_bf[...] = wgu_buf[slot].astype(BF16)
                wd_bf[...] = wd_buf[slot].astype(BF16)

            bias_row = pl.ds(te_ref[t] % SUBLANES, 1)
            x = _load_slabs(xs_tile, EXPERT_TILE)
            gu = _dot(x.astype(BF16), wgu_bf[...]) + bgu_ref[bias_row, :]
            x_glu = jnp.minimum(gu[:, :dff], SWIGLU_LIMIT)
            x_lin = jnp.clip(gu[:, dff:], -SWIGLU_LIMIT, SWIGLU_LIMIT)
            act = x_glu * jax.nn.sigmoid(SWIGLU_ALPHA * x_glu) * (x_lin + 1.0)
            _store_slabs(ys_tile, _dot(act.astype(BF16), wd_bf[...]) + bd_ref[bias_row, :])

        @pl.when(t >= nu_ref[0])
        def _(ys_tile=ys_tile):
            ys_tile[...] = jnp.zeros(ys_tile.shape, F32)


def _experts(layer, tile_expert, next_expert, parity, n_used, xs, w_gu, b_gu, w_down, b_down):
    tm = EXPERT_TILE
    d, dff2 = w_gu.shape[-2:]
    dff = w_down.shape[-2]
    step_rows = TILES_PER_STEP * tm * SUBLANES
    n_tiles = tile_expert.shape[0]
    assert n_tiles % TILES_PER_STEP == 0 and xs.shape[0] == n_tiles * tm * SUBLANES
    hbm = pl.BlockSpec(memory_space=pl.ANY)

    def bias_specs(half):
        group = lambda i, te: te[i * TILES_PER_STEP + half] // SUBLANES
        return [pl.BlockSpec((None, SUBLANES, dff2), lambda i, te, nx, pa, nu: (layer, group(i, te), 0)),
                pl.BlockSpec((None, SUBLANES, d), lambda i, te, nx, pa, nu: (layer, group(i, te), 0))]

    last_step = lambda nu: (nu[0] - 1) // TILES_PER_STEP
    grid_spec = pltpu.PrefetchScalarGridSpec(
        num_scalar_prefetch=4,
        grid=(n_tiles // TILES_PER_STEP,),
        in_specs=[pl.BlockSpec((step_rows, LANES), lambda i, te, nx, pa, nu: (jnp.minimum(i, last_step(nu)), 0)),
                  hbm, hbm] + [s for half in range(TILES_PER_STEP) for s in bias_specs(half)],
        out_specs=pl.BlockSpec((step_rows, LANES), lambda i, te, nx, pa, nu: (i, 0)),
        scratch_shapes=[pltpu.VMEM((2, d, dff2), F32), pltpu.VMEM((2, dff, d), F32),
                        pltpu.VMEM((d, dff2), BF16), pltpu.VMEM((dff, d), BF16),
                        pltpu.SemaphoreType.DMA((4,))],
    )
    biases = (b_gu, b_down) * TILES_PER_STEP
    return pl.pallas_call(
        functools.partial(_expert_kernel, layer),
        out_shape=jax.ShapeDtypeStruct(xs.shape, F32),
        grid_spec=grid_spec,
        compiler_params=_params(("arbitrary",), VMEM_LIMIT),
        name="moe_experts",
    )(tile_expert, next_expert, parity, n_used, xs, w_gu, w_down, *biases)


def _combine_kernel(final, tile0, x1_ref, gates_ref, mod_ref, fw_ref, pos_ref, ys_ref, out_ref,
                    idx_ref, rows_ref, idx_sem, row_sem):
    tm = x1_ref.shape[0]
    _fetch_positions(pos_ref, idx_ref, idx_sem, pl.program_id(0) + tile0)

    def issue(r, carry):
        for k in range(TOP_K):
            _slab_copy(ys_ref, idx_ref[k * tm + r], rows_ref.at[k], r, row_sem).start(priority=k % 2)
        return carry

    lax.fori_loop(0, tm, issue, 0, unroll=ISSUE_UNROLL)
    _wait_slabs(ys_ref, tm * TOP_K, row_sem)

    gates = gates_ref[...]
    moe = gates[:, 0:1] * _load_slabs(rows_ref, tm, (0,))
    for k in range(1, TOP_K):
        moe = moe + gates[:, k:k + 1] * _load_slabs(rows_ref, tm, (k,))
    x2 = x1_ref[...] + mod_ref[5:6, :] * moe
    if final:
        x2 = x2 * lax.rsqrt(jnp.mean(x2 * x2, axis=-1, keepdims=True) + EPS) * fw_ref[...]
    out_ref[...] = x2


def _combine(layer, final, row0, n_rows, x1, gates, mod, mod_row, final_w, pos_flat, ys):
    d = x1.shape[1]
    tm = TOKEN_TILE
    tile0 = row0 // tm
    hbm = pl.BlockSpec(memory_space=pl.ANY)
    return pl.pallas_call(
        functools.partial(_combine_kernel, final, tile0),
        out_shape=jax.ShapeDtypeStruct((n_rows, d), F32),
        grid=(n_rows // tm,),
        in_specs=[pl.BlockSpec((tm, d), lambda i: (tile0 + i, 0)),
                  pl.BlockSpec((tm, LANES), lambda i: (tile0 + i, 0)),
                  _mod_spec(mod, layer, mod_row, tile0),
                  pl.BlockSpec((1, d), lambda i: (0, 0)),
                  hbm, hbm],
        out_specs=pl.BlockSpec((tm, d), lambda i: (i, 0)),
        scratch_shapes=[pltpu.SMEM((tm * TOP_K,), jnp.int32),
                        pltpu.VMEM((TOP_K, tm * SUBLANES, LANES), F32),
                        pltpu.SemaphoreType.DMA, pltpu.SemaphoreType.DMA],
        compiler_params=_params(("arbitrary",), VMEM_LIMIT),
        name="moe_combine",
    )(x1, gates, mod, final_w.reshape(1, d), pos_flat, ys)


def _routing_tables(counts, n_tiles):
    tm = EXPERT_TILE
    c = counts[0, :N_EXPERTS].astype(jnp.int32)
    tiles = (c + tm - 1) // tm
    tile_end = jnp.cumsum(tiles)
    offs = (tile_end - tiles) * tm
    n_used = tile_end[-1]
    t = jnp.minimum(jnp.arange(n_tiles, dtype=jnp.int32), n_used - 1)[:, None]
    e = jnp.arange(N_EXPERTS, dtype=jnp.int32)[None, :]
    tile_start, has_rows = (tile_end - tiles)[None, :], (tiles > 0)[None, :]
    te = jnp.minimum(jnp.sum((tile_end[None, :] <= t).astype(jnp.int32), axis=1), N_EXPERTS - 1)
    nxt = jnp.min(jnp.where(has_rows & (tile_start > t), e, N_EXPERTS), axis=1)
    nxt = jnp.where(nxt == N_EXPERTS, -1, nxt)
    parity = (jnp.sum((has_rows & (tile_start <= t)).astype(jnp.int32), axis=1) - 1) % 2
    return (offs, tile_end.astype(jnp.int32), te, nxt.astype(jnp.int32), parity.astype(jnp.int32),
            n_used.reshape(1).astype(jnp.int32))


def kernel(x_prompt, x_sample, cache_k, cache_v, state_lru, c, c_ctx, w_mod, b_mod, norm_mix_w, w_in, conf_conv_w, conf_conv_b, conf_ln_w, conf_ln_b, q_norm_w, k_norm_w, lru_conv_w, lru_conv_b, lru_wa, lru_ba, lru_wx, lru_bx, lru_lambda, w_out, norm_ffn_w, router_w, router_b, w_gu, b_gu, w_down, b_down, final_norm_w):
    batch, seq, d = x_prompt.shape
    dec_batch, dec_seq, _ = x_sample.shape
    depth = w_mod.shape[0]
    n_ctx, n_lat = batch * seq, dec_batch * dec_seq
    n = n_ctx + n_lat
    conf_w = conf_conv_w.shape[-1]
    lru_w = lru_conv_w.shape[-1]
    kv_w = N_KV_HEADS * HEAD_DIM
    attn_w = N_HEADS * HEAD_DIM
    widths = (2 * conf_w, attn_w, kv_w, kv_w, lru_w, lru_w)
    past = cache_k.shape[2]

    ctx_tiles = n_ctx // TOKEN_TILE
    lat_tiles_per_seq = dec_seq // TOKEN_TILE
    mod_row = lambda i: jnp.where(i < ctx_tiles, 0, 1 + (i - ctx_tiles) // lat_tiles_per_seq)

    n_cond = 8
    cvec = jnp.zeros((n_cond, d), F32).at[0].set(c_ctx).at[1:1 + dec_batch].set(c)
    mod = _modulation(cvec, w_mod, b_mod).reshape(depth, n_cond, 6, d)

    x = (x_prompt.reshape(n_ctx, d), x_sample.reshape(n_lat, d))
    rope = _rope_tables(dec_seq)
    cache_k4 = cache_k.reshape(dec_batch, depth, past, kv_w)
    cache_v4 = cache_v.reshape(dec_batch, depth, past, kv_w)
    h0_ctx = jnp.zeros((batch, 1, 2, lru_w), F32)
    assert d == SUBLANES * LANES, "row tables are moved as one (8, 128) tile per token"
    assert N_EXPERTS % SUBLANES == 0, "expert biases are fetched in blocks of 8 experts"
    n_sorted_tiles = n * TOP_K // EXPERT_TILE + N_EXPERTS

    row = lambda a: a.reshape(depth, 1, -1)
    p = {
        'conf_conv_w': conf_conv_w, 'conf_conv_b': row(conf_conv_b),
        'conf_ln_w': row(conf_ln_w), 'conf_ln_b': row(conf_ln_b),
        'lru_conv_w': lru_conv_w, 'lru_conv_b': row(lru_conv_b), 'lru_wa': lru_wa, 'lru_wx': lru_wx,
        'lru_bg': jnp.concatenate([row(lru_ba), row(lru_bx)], axis=-1), 'lru_lam': row(lru_lambda),
    }
    norm_mix, norm_ffn = row(norm_mix_w), row(norm_ffn_w)
    qw, kw = row(q_norm_w), row(k_norm_w)
    qw8, kw2 = jnp.tile(qw, (1, 1, N_HEADS)), jnp.tile(kw, (1, 1, N_KV_HEADS))
    router_w128 = jnp.pad(router_w, ((0, 0), (0, 0), (0, LANES - N_EXPERTS)))
    router_b128 = row(jnp.pad(router_b, ((0, 0), (0, LANES - N_EXPERTS))))

    new_k, new_v, new_h = [], [], []
    for l in range(depth):
        conf_u, q, k, v, lru_x, lru_g = _pre_mixer(l, x, mod, mod_row, norm_mix, w_in, widths)

        conf_c, lru_c, h_last = _seq_mixers(l, seq, batch, 0, conf_u, lru_x, lru_g, h0_ctx, 0, p)
        conf_l, lru_l, _ = _seq_mixers(l, dec_seq, dec_batch, n_ctx, conf_u, lru_x, lru_g, state_lru, l, p)

        attn_c, k_ctx = _attention_ctx(l, seq, batch, q, k, v, qw, kw)
        attn_l = _attention_lat(dec_seq, dec_batch, n_ctx, l, q, k, v, cache_k4, cache_v4, qw8, kw2, rope)
        new_k.append(k_ctx.reshape(batch, seq, N_KV_HEADS, HEAD_DIM))
        new_v.append(v[:n_ctx].reshape(batch, seq, N_KV_HEADS, HEAD_DIM))
        new_h.append(h_last)

        x1, h2, topi, gates, rank, counts = _post_mixer(
            l, x, (conf_c, attn_c, lru_c), (conf_l, attn_l, lru_l), mod, mod_row, w_out, norm_ffn,
            router_w128, router_b128)
        offs, tile_end, tile_expert, next_expert, parity, n_used = _routing_tables(counts, n_sorted_tiles)
        pos = _positions(topi, rank, offs)
        xs = _dispatch(h2, pos, tile_end, n_sorted_tiles)
        ys = _experts(l, tile_expert, next_expert, parity, n_used, xs, w_gu, b_gu, w_down, b_down)
        x = tuple(_combine(l, l == depth - 1, row0, rows, x1, gates, mod, mod_row, final_norm_w, pos, ys)
                  for row0, rows in ((0, n_ctx), (n_ctx, n_lat)))

    y_prompt = x[0].reshape(batch, seq, d)
    y_sample = x[1].reshape(dec_batch, dec_seq, d)
    return (y_prompt, y_sample, jnp.stack(new_k, axis=1), jnp.stack(new_v, axis=1), jnp.stack(new_h, axis=1))
```

```python
import functools

import jax
import jax.numpy as jnp
import numpy as np
from jax import lax
from jax.experimental import pallas as pl
from jax.experimental.pallas import tpu as pltpu

F32 = jnp.float32
BF16 = jnp.bfloat16

HEAD_DIM = 64
N_HEADS = 8
N_KV_HEADS = 2
HEADS_PER_KV = N_HEADS // N_KV_HEADS
CONF_KERNEL = 31
LRU_CONV = 4
LRU_C = 8.0
N_EXPERTS = 32
TOP_K = 4
SWIGLU_ALPHA = 1.702
SWIGLU_LIMIT = 7.0
ROPE_THETA = 10000.0
GRID_W = 64
EPS = 1e-6

LANES = 128
SUBLANES = 8
ISSUE_UNROLL = 4
TOKEN_TILE = 256
EXPERT_TILE = 256
TILES_PER_STEP = 4
CONV_HALO = 16
ROW_CHUNK = 64
SCAN_CHUNK = 32
VMEM_LIMIT = 56 * 1024 * 1024


def _params(sem, vmem=None):
    return pltpu.CompilerParams(dimension_semantics=sem, vmem_limit_bytes=vmem)


def _split_bf16(x):
    hi = x.astype(BF16)
    lo = (x - hi.astype(F32)).astype(BF16)
    return hi, lo


def _dot(a, b):
    return jnp.dot(a, b, preferred_element_type=F32)


def _dot3(a, b):
    a_hi, a_lo = _split_bf16(a)
    b_hi, b_lo = _split_bf16(b)
    return _dot(a_hi, b_hi) + (_dot(a_hi, b_lo) + _dot(a_lo, b_hi))


def _mod_kernel(c_ref, w_ref, b_ref, o_ref):
    c = c_ref[...]
    s = c * jax.nn.sigmoid(c)
    o_ref[...] = _dot3(s, w_ref[...]) + b_ref[...]


def _modulation(cvec, w_mod, b_mod):
    depth, d, d6 = w_mod.shape
    tn = 768
    return pl.pallas_call(
        _mod_kernel,
        out_shape=jax.ShapeDtypeStruct((depth, cvec.shape[0], d6), F32),
        grid=(depth, d6 // tn),
        in_specs=[
            pl.BlockSpec(cvec.shape, lambda l, j: (0, 0)),
            pl.BlockSpec((None, d, tn), lambda l, j: (l, 0, j)),
            pl.BlockSpec((None, 1, tn), lambda l, j: (l, 0, j)),
        ],
        out_specs=pl.BlockSpec((None, cvec.shape[0], tn), lambda l, j: (l, 0, j)),
        compiler_params=_params(("arbitrary", "arbitrary")),
        name="modulation",
    )(cvec, w_mod, b_mod.reshape(depth, 1, d6))


def _pre_kernel(ctx_tiles, xc_ref, xl_ref, mod_ref, nw_ref, win_f32_ref, conf_ref, q_ref, k_ref, v_ref, lx_ref,
                lg_ref, win_ref):
    @pl.when(pl.program_id(0) == 0)
    def _():
        win_ref[...] = win_f32_ref[...].astype(BF16)

    x = jnp.where(pl.program_id(0) < ctx_tiles, xc_ref[...], xl_ref[...])
    m = mod_ref[...]
    shift, scale = m[0:1], m[1:2]
    h = x * lax.rsqrt(jnp.mean(x * x, axis=-1, keepdims=True) + EPS) * nw_ref[...]
    h = h * (1.0 + scale) + shift
    proj = _dot(h.astype(BF16), win_ref[...])
    col = 0
    for ref in (conf_ref, q_ref, k_ref, v_ref, lx_ref, lg_ref):
        w = ref.shape[-1]
        ref[...] = proj[:, col:col + w]
        col += w


def _path_specs(x_pair, tm):
    ctx_tiles = x_pair[0].shape[0] // tm
    return ctx_tiles, [
        pl.BlockSpec((tm, x_pair[0].shape[-1]), lambda i: (jnp.minimum(i, ctx_tiles - 1), 0)),
        pl.BlockSpec((tm, x_pair[1].shape[-1]), lambda i: (jnp.maximum(i - ctx_tiles, 0), 0))]


def _layer_spec(a, layer):
    zeros = (0,) * (a.ndim - 1)
    return pl.BlockSpec((None,) + a.shape[1:], lambda *_: (layer,) + zeros)


def _mod_spec(mod, layer, mod_row, tile0=0):
    return pl.BlockSpec((None, None) + mod.shape[2:], lambda i, *_: (layer, mod_row(tile0 + i), 0, 0))


def _pre_mixer(layer, x_pair, mod, mod_row, norm_w, w_in, widths):
    n, d = x_pair[0].shape[0] + x_pair[1].shape[0], x_pair[0].shape[1]
    ctx_tiles, x_specs = _path_specs(x_pair, TOKEN_TILE)
    return pl.pallas_call(
        functools.partial(_pre_kernel, ctx_tiles),
        out_shape=[jax.ShapeDtypeStruct((n, w), F32) for w in widths],
        grid=(n // TOKEN_TILE,),
        in_specs=x_specs + [_mod_spec(mod, layer, mod_row), _layer_spec(norm_w, layer),
                            _layer_spec(w_in, layer)],
        out_specs=[pl.BlockSpec((TOKEN_TILE, w), lambda i: (i, 0)) for w in widths],
        scratch_shapes=[pltpu.VMEM(w_in.shape[1:], BF16)],
        compiler_params=_params(("arbitrary",), VMEM_LIMIT),
        name="pre_mixer",
    )(*x_pair, mod, norm_w, w_in)


def _gelu_tanh(x):
    return 0.5 * x * (1.0 + jnp.tanh(0.7978845608028654 * (x + 0.044715 * (x * x * x))))


def _seq_kernel(seq_len, conf_ref, lx_ref, lg_ref, h0_ref, ccw_ref, ccb_ref, lnw_ref, lnb_ref,
                lcw_ref, lcb_ref, wa_ref, wx_ref, bg_ref, lam_ref,
                co_ref, lo_ref, hl_ref, pad_ref, rot_ref, af_ref, uf_ref, ab_ref, ub_ref, wg_ref):
    L = seq_len
    W = co_ref.shape[-1]
    pad = L // 2
    zeros_halo = jnp.zeros((CONV_HALO, W), F32)

    @pl.when(pl.program_id(0) == 0)
    def _():
        dirs, heads, blk, _ = wa_ref.shape
        zero_blk = jnp.zeros((blk, blk), F32)
        for h in range(heads):
            cols = [w_ref[d, h] if g == h else zero_blk
                    for w_ref in (wa_ref, wx_ref) for d in range(dirs) for g in range(heads)]
            wg_ref[h * blk:(h + 1) * blk, :] = jnp.concatenate(cols, axis=-1).astype(BF16)

    pad_ref[0:CONV_HALO, :] = zeros_halo
    pad_ref[CONV_HALO + L:2 * CONV_HALO + L, :] = zeros_halo
    for c in range(L // ROW_CHUNK):
        r0 = c * ROW_CHUNK
        u = conf_ref[r0:r0 + ROW_CHUNK, :]
        pad_ref[CONV_HALO + r0:CONV_HALO + r0 + ROW_CHUNK, :] = u[:, :W] * jax.nn.sigmoid(u[:, W:])
    shifted_rows = L + 2 * CONV_HALO - SUBLANES
    for s in range(1, SUBLANES):
        for r0 in range(0, shifted_rows, ROW_CHUNK):
            rows = min(ROW_CHUNK, shifted_rows - r0)
            rot_ref[s - 1, r0:r0 + rows, :] = pad_ref[r0 + s:r0 + s + rows, :]
    left = CONF_KERNEL // 2
    for c in range(L // ROW_CHUNK):
        r0 = c * ROW_CHUNK
        acc = jnp.zeros((ROW_CHUNK, W), F32) + ccb_ref[...]
        for k in range(CONF_KERNEL):
            whole, s = divmod(CONV_HALO - left + k, SUBLANES)
            start = whole * SUBLANES + r0
            src = pad_ref if s == 0 else rot_ref.at[s - 1]
            acc = acc + ccw_ref[k:k + 1, :] * src[start:start + ROW_CHUNK, :]
        mu = jnp.mean(acc, axis=-1, keepdims=True)
        cen = acc - mu
        var = jnp.mean(cen * cen, axis=-1, keepdims=True)
        y = cen * lax.rsqrt(var + EPS) * lnw_ref[...] + lnb_ref[...]
        co_ref[r0:r0 + ROW_CHUNK, :] = y * jax.nn.sigmoid(y)

    for c in range(L // ROW_CHUNK):
        r0 = c * ROW_CHUNK
        pad_ref[CONV_HALO + r0:CONV_HALO + r0 + ROW_CHUNK, :] = lx_ref[r0:r0 + ROW_CHUNK, :]
    ones_pad = jnp.ones((pad, W), F32)
    zeros_pad = jnp.zeros((pad, W), F32)
    af_ref[0:pad, :] = ones_pad
    uf_ref[0:pad, :] = zeros_pad
    ab_ref[L:L + pad, :] = ones_pad
    ub_ref[L:L + pad, :] = zeros_pad
    sp = jax.nn.softplus(-lam_ref[...])
    h0 = h0_ref[...]
    left = LRU_CONV // 2
    n_chunks = L // ROW_CHUNK
    for c in range(n_chunks):
        r0 = c * ROW_CHUNK
        xc = jnp.zeros((ROW_CHUNK, W), F32) + lcb_ref[...]
        for k in range(LRU_CONV):
            start = CONV_HALO - left + k + r0
            xc = xc + lcw_ref[k:k + 1, :] * pad_ref[start:start + ROW_CHUNK, :]
        g = _dot(xc.astype(BF16), wg_ref[...]) + bg_ref[...]
        r = jax.nn.sigmoid(g[:, :2 * W])
        i = jax.nn.sigmoid(g[:, 2 * W:])
        a = jnp.exp((-LRU_C) * r * sp)
        xc2 = jnp.concatenate([xc, xc], axis=-1)
        u = jnp.sqrt(1.0 - a * a) * i * xc2
        a_f, a_b, u_f, u_b = a[:, :W], a[:, W:], u[:, :W], u[:, W:]
        row = lax.broadcasted_iota(jnp.int32, (ROW_CHUNK, W), 0)
        if c == 0:
            u_f = jnp.where(row == 0, u_f + a_f * h0[0:1], u_f)
        if c == n_chunks - 1:
            u_b = jnp.where(row == ROW_CHUNK - 1, u_b + a_b * h0[1:2], u_b)
        af_ref[pad + r0:pad + r0 + ROW_CHUNK, :] = a_f
        uf_ref[pad + r0:pad + r0 + ROW_CHUNK, :] = u_f
        ab_ref[r0:r0 + ROW_CHUNK, :] = a_b
        ub_ref[r0:r0 + ROW_CHUNK, :] = u_b

    n_sc = L // SCAN_CHUNK
    s = 1
    while s < L:
        for c in reversed(range(n_sc)):
            r0 = c * SCAN_CHUNK
            if r0 + SCAN_CHUNK <= s:
                continue
            cur = slice(pad + r0, pad + r0 + SCAN_CHUNK)
            sh = slice(pad + r0 - s, pad + r0 - s + SCAN_CHUNK)
            a_cur = af_ref[cur, :]
            uf_ref[cur, :] = uf_ref[cur, :] + a_cur * uf_ref[sh, :]
            af_ref[cur, :] = a_cur * af_ref[sh, :]
        for c in range(n_sc):
            r0 = c * SCAN_CHUNK
            if r0 >= L - s:
                continue
            cur = slice(r0, r0 + SCAN_CHUNK)
            sh = slice(r0 + s, r0 + s + SCAN_CHUNK)
            a_cur = ab_ref[cur, :]
            ub_ref[cur, :] = ub_ref[cur, :] + a_cur * ub_ref[sh, :]
            ab_ref[cur, :] = a_cur * ab_ref[sh, :]
        s *= 2

    for c in range(n_chunks):
        r0 = c * ROW_CHUNK
        h = uf_ref[pad + r0:pad + r0 + ROW_CHUNK, :] + ub_ref[r0:r0 + ROW_CHUNK, :]
        lo_ref[r0:r0 + ROW_CHUNK, :] = h * _gelu_tanh(lg_ref[r0:r0 + ROW_CHUNK, :])
    hl_ref[0:1, :] = uf_ref[pad + L - 1:pad + L, :]
    hl_ref[1:2, :] = ub_ref[0:1, :]


def _seq_mixers(layer, seq_len, n_seq, row0, conf_u, lru_x, lru_g, h0, h0_layer, p):
    w = lru_x.shape[-1]
    b0 = row0 // seq_len
    pad = seq_len // 2
    in_spec = lambda width: pl.BlockSpec((seq_len, width), lambda i: (b0 + i, 0))
    out_spec = pl.BlockSpec((seq_len, w), lambda i: (i, 0))
    h0_spec = pl.BlockSpec((None, None, 2, w), lambda i: (i, h0_layer, 0, 0))
    weights = (p['conf_conv_w'], p['conf_conv_b'], p['conf_ln_w'], p['conf_ln_b'],
               p['lru_conv_w'], p['lru_conv_b'], p['lru_wa'], p['lru_wx'], p['lru_bg'], p['lru_lam'])
    padded = seq_len + 2 * CONV_HALO
    return pl.pallas_call(
        functools.partial(_seq_kernel, seq_len),
        out_shape=[jax.ShapeDtypeStruct((n_seq * seq_len, w), F32)] * 2
        + [jax.ShapeDtypeStruct((n_seq, 2, w), F32)],
        grid=(n_seq,),
        in_specs=[in_spec(2 * w), in_spec(w), in_spec(w), h0_spec] + [_layer_spec(a, layer) for a in weights],
        out_specs=[out_spec, out_spec, pl.BlockSpec((None, 2, w), lambda i: (i, 0, 0))],
        scratch_shapes=[pltpu.VMEM((padded, w), F32), pltpu.VMEM((SUBLANES - 1, padded, w), F32)]
        + [pltpu.VMEM((seq_len + pad, w), F32)] * 4 + [pltpu.VMEM((w, 4 * w), BF16)],
        compiler_params=_params(("arbitrary",), VMEM_LIMIT),
        name=f"seq_mixers_{seq_len}",
    )(conf_u, lru_x, lru_g, h0, *weights)


def _head_rms(x):
    return lax.rsqrt(jnp.mean(x * x, axis=-1, keepdims=True) + EPS)


def _swap_halves(x):
    width = x.shape[-1]
    lane = lax.broadcasted_iota(jnp.int32, x.shape, x.ndim - 1)
    up = pltpu.roll(x, width - HEAD_DIM // 4, x.ndim - 1)
    down = pltpu.roll(x, HEAD_DIM // 4, x.ndim - 1)
    return jnp.where((lane % (HEAD_DIM // 2)) < HEAD_DIM // 4, up, down)


def _attend(q_heads, k_bf16, v_bf16):
    scale = HEAD_DIM ** -0.5
    s = lax.dot_general(q_heads.astype(BF16), k_bf16, (((1,), (1,)), ((), ())),
                        preferred_element_type=F32) * scale
    p = jnp.exp(s - jnp.max(s, axis=-1, keepdims=True))
    denom = jnp.sum(p, axis=-1, keepdims=True)
    return _dot(p.astype(BF16), v_bf16) / denom


def _attn_ctx_kernel(q_ref, k_ref, v_ref, qw_ref, kw_ref, o_ref, kn_ref):
    L = q_ref.shape[0]
    q, k, v = q_ref[...], k_ref[...], v_ref[...]
    qw, kw = qw_ref[...], kw_ref[...]
    k_out, o_out = [], []
    for g in range(N_KV_HEADS):
        kh = k[:, g * HEAD_DIM:(g + 1) * HEAD_DIM]
        kh = kh * _head_rms(kh) * kw
        k_out.append(kh)
        qs = []
        for j in range(HEADS_PER_KV):
            h = g * HEADS_PER_KV + j
            qh = q[:, h * HEAD_DIM:(h + 1) * HEAD_DIM]
            qs.append(qh * _head_rms(qh) * qw)
        o = _attend(jnp.concatenate(qs, axis=0), kh.astype(BF16),
                    v[:, g * HEAD_DIM:(g + 1) * HEAD_DIM].astype(BF16))
        o_out += [o[j * L:(j + 1) * L] for j in range(HEADS_PER_KV)]
    kn_ref[...] = jnp.concatenate(k_out, axis=-1)
    o_ref[...] = jnp.concatenate(o_out, axis=-1)


def _attention_ctx(layer, seq_len, n_seq, q, k, v, qw, kw):
    kvw = k.shape[-1]
    row_spec = lambda width: pl.BlockSpec((seq_len, width), lambda i: (i, 0))
    return pl.pallas_call(
        _attn_ctx_kernel,
        out_shape=[jax.ShapeDtypeStruct((n_seq * seq_len, q.shape[-1]), F32),
                   jax.ShapeDtypeStruct((n_seq * seq_len, kvw), F32)],
        grid=(n_seq,),
        in_specs=[row_spec(q.shape[-1]), row_spec(kvw), row_spec(kvw), _layer_spec(qw, layer),
                  _layer_spec(kw, layer)],
        out_specs=[row_spec(q.shape[-1]), row_spec(kvw)],
        compiler_params=_params(("arbitrary",), VMEM_LIMIT),
        name="attention_ctx",
    )(q, k, v, qw, kw)


def _attn_lat_kernel(q_ref, k_ref, v_ref, ck_ref, cv_ref, qw_ref, kw_ref, cq_ref, sq_ref, ck_t_ref,
                     sk_t_ref, o_ref, kall_ref, vall_ref):
    L = k_ref.shape[0]
    tq = q_ref.shape[0]

    @pl.when(pl.program_id(1) == 0)
    def _():
        k = k_ref[...]
        t = k * kw_ref[...]
        rot = t * ck_t_ref[...] + _swap_halves(t) * sk_t_ref[...]
        parts = []
        for g in range(N_KV_HEADS):
            sl = slice(g * HEAD_DIM, (g + 1) * HEAD_DIM)
            parts.append(rot[:, sl] * _head_rms(k[:, sl]))
        kall_ref[0:L, :] = jnp.concatenate(parts, axis=-1).astype(BF16)
        kall_ref[L:, :] = ck_ref[...].astype(BF16)
        vall_ref[0:L, :] = v_ref[...].astype(BF16)
        vall_ref[L:, :] = cv_ref[...].astype(BF16)

    q = q_ref[...]
    t = q * qw_ref[...]
    rot = t * cq_ref[...] + _swap_halves(t) * sq_ref[...]
    o_out = []
    for g in range(N_KV_HEADS):
        qs = []
        for j in range(HEADS_PER_KV):
            sl = slice((g * HEADS_PER_KV + j) * HEAD_DIM, (g * HEADS_PER_KV + j + 1) * HEAD_DIM)
            qs.append(rot[:, sl] * _head_rms(q[:, sl]))
        sl = slice(g * HEAD_DIM, (g + 1) * HEAD_DIM)
        o = _attend(jnp.concatenate(qs, axis=0), kall_ref[:, sl], vall_ref[:, sl])
        o_out += [o[j * tq:(j + 1) * tq] for j in range(HEADS_PER_KV)]
    o_ref[...] = jnp.concatenate(o_out, axis=-1)


def _attention_lat(seq_len, n_seq, row0, layer, q, k, v, cache_k, cache_v, qw8, kw2, rope):
    tq = TOKEN_TILE
    nq = seq_len // tq
    qwid, kvw = q.shape[-1], k.shape[-1]
    past = cache_k.shape[2]
    b0q = row0 // tq
    b0s = row0 // seq_len
    cq, sq, ck, sk = rope
    full = lambda a: pl.BlockSpec(a.shape, lambda b, j: (0,) * a.ndim)
    seq_spec = pl.BlockSpec((seq_len, kvw), lambda b, j: (b0s + b, 0))
    cache_spec = pl.BlockSpec((None, None, past, kvw), lambda b, j: (b, layer, 0, 0))
    q_spec = pl.BlockSpec((tq, qwid), lambda b, j: (b0q + b * nq + j, 0))
    rope_q_spec = pl.BlockSpec((tq, qwid), lambda b, j: (j, 0))
    return pl.pallas_call(
        _attn_lat_kernel,
        out_shape=jax.ShapeDtypeStruct((n_seq * seq_len, qwid), F32),
        grid=(n_seq, nq),
        in_specs=[q_spec, seq_spec, seq_spec, cache_spec, cache_spec, _layer_spec(qw8, layer),
                  _layer_spec(kw2, layer), rope_q_spec, rope_q_spec, full(ck), full(sk)],
        out_specs=pl.BlockSpec((tq, qwid), lambda b, j: (b * nq + j, 0)),
        scratch_shapes=[pltpu.VMEM((seq_len + past, kvw), BF16)] * 2,
        compiler_params=_params(("arbitrary", "arbitrary"), VMEM_LIMIT),
        name="attention_lat",
    )(q, k, v, cache_k, cache_v, qw8, kw2, cq, sq, ck, sk)


def _rope_tables(seq_len):
    t = np.arange(seq_len)
    row = (t // GRID_W).astype(np.float32)
    col = (t % GRID_W).astype(np.float32)
    half = HEAD_DIM // 2
    freqs = (np.float32(ROPE_THETA) ** (-np.arange(0, half, 2, dtype=np.float32) / np.float32(half)))
    ang_r, ang_c = row[:, None] * freqs, col[:, None] * freqs
    cos = np.concatenate([np.cos(ang_r)] * 2 + [np.cos(ang_c)] * 2, axis=-1)
    sin = np.concatenate([-np.sin(ang_r), np.sin(ang_r), -np.sin(ang_c), np.sin(ang_c)], axis=-1)
    return tuple(jnp.asarray(a, F32) for a in
                 (np.tile(cos, (1, N_HEADS)), np.tile(sin, (1, N_HEADS)),
                  np.tile(cos, (1, N_KV_HEADS)), np.tile(sin, (1, N_KV_HEADS))))


def _slab_copy(src_ref, src_tok, dst_ref, dst_tok, sem):
    src = src_ref.at[pl.ds(pl.multiple_of(src_tok * SUBLANES, SUBLANES), SUBLANES)]
    dst = dst_ref.at[pl.ds(pl.multiple_of(dst_tok * SUBLANES, SUBLANES), SUBLANES)]
    return pltpu.make_async_copy(src, dst, sem)


def _load_slabs(ref, n_tok, lead=()):
    return jnp.concatenate(
        [ref[(*lead, pl.ds(j, n_tok, stride=SUBLANES), slice(None))] for j in range(SUBLANES)], axis=-1)


def _store_slabs(ref, val):
    for j in range(SUBLANES):
        ref[pl.ds(j, val.shape[0], stride=SUBLANES), :] = val[:, j * LANES:(j + 1) * LANES]


def _wait_slabs(hbm_ref, n_tok, sem):
    span = hbm_ref.at[pl.ds(0, n_tok * SUBLANES)]
    pltpu.make_async_copy(span, span, sem).wait()


def _fetch_positions(pos_ref, idx_ref, sem, tile):
    n = idx_ref.shape[0]
    copy = pltpu.make_async_copy(pos_ref.at[pl.ds(tile * n, n)], idx_ref, sem)
    copy.start()
    copy.wait()


def _choice_major(x):
    t = jnp.transpose(x)
    return jnp.concatenate([t[k:k + 1, :] for k in range(TOP_K)], axis=1)


def _post_kernel(ctx_tiles, xc_ref, xl_ref, co_c_ref, ao_c_ref, lo_c_ref, co_l_ref, ao_l_ref, lo_l_ref, mod_ref,
                 wout_f32_ref, nw_ref, rw_ref, rb_ref,
                 x1_ref, h2_ref, topi_ref, gates_ref, rank_ref, counts_ref, carry_ref, wout_ref):
    tm = xc_ref.shape[0]
    cw, aw = co_c_ref.shape[-1], ao_c_ref.shape[-1]
    is_ctx = pl.program_id(0) < ctx_tiles

    @pl.when(pl.program_id(0) == 0)
    def _():
        carry_ref[...] = jnp.zeros_like(carry_ref)
        wout_ref[...] = wout_f32_ref[...].astype(BF16)

    m = mod_ref[...]
    gate1, shift2, scale2 = m[2:3], m[3:4], m[4:5]
    pick = lambda c_ref, l_ref: jnp.where(is_ctx, c_ref[...], l_ref[...]).astype(BF16)
    x = jnp.where(is_ctx, xc_ref[...], xl_ref[...])
    mixed = (_dot(pick(co_c_ref, co_l_ref), wout_ref[0:cw, :])
             + _dot(pick(ao_c_ref, ao_l_ref), wout_ref[cw:cw + aw, :])
             + _dot(pick(lo_c_ref, lo_l_ref), wout_ref[cw + aw:, :]))
    x1 = x + gate1 * mixed
    x1_ref[...] = x1
    h2 = x1 * lax.rsqrt(jnp.mean(x1 * x1, axis=-1, keepdims=True) + EPS) * nw_ref[...]
    h2 = h2 * (1.0 + scale2) + shift2
    _store_slabs(h2_ref, h2)

    lane = lax.broadcasted_iota(jnp.int32, (tm, LANES), 1)
    lane_f = lane.astype(F32)
    logits = jnp.where(lane < N_EXPERTS, _dot3(h2, rw_ref[...]) + rb_ref[...], -jnp.inf)
    top_v, onehots = [], []
    topi = jnp.zeros((tm, LANES), F32)
    for k in range(TOP_K):
        mx = jnp.max(logits, axis=-1, keepdims=True)
        idx = jnp.min(jnp.where(logits == mx, lane_f, float(LANES)), axis=-1, keepdims=True)
        hit = lane_f == idx
        logits = jnp.where(hit, -jnp.inf, logits)
        top_v.append(mx)
        onehots.append(hit)
        topi = jnp.where(lane == k, idx, topi)
    topi_ref[...] = _choice_major(topi).astype(jnp.int32)
    exps = [jnp.exp(v - top_v[0]) for v in top_v]
    denom = exps[0] + exps[1] + exps[2] + exps[3]
    gates = jnp.zeros((tm, LANES), F32)
    for k in range(TOP_K):
        gates = jnp.where(lane == k, exps[k] / denom, gates)
    gates_ref[...] = gates

    chosen = jnp.zeros((tm, LANES), F32)
    for hit in onehots:
        chosen = jnp.where(hit, 1.0, chosen)
    r_i = lax.broadcasted_iota(jnp.int32, (tm, tm), 0)
    c_i = lax.broadcasted_iota(jnp.int32, (tm, tm), 1)
    lower = jnp.where(c_i < r_i, 1.0, 0.0).astype(BF16)
    before = _dot(lower, chosen.astype(BF16)) + carry_ref[...]
    rank = jnp.zeros((tm, LANES), F32)
    for k, hit in enumerate(onehots):
        rk = jnp.sum(jnp.where(hit, before, 0.0), axis=-1, keepdims=True)
        rank = jnp.where(lane == k, rk, rank)
    rank_ref[...] = _choice_major(rank).astype(jnp.int32)
    carry = carry_ref[...] + jnp.sum(chosen, axis=0, keepdims=True)
    carry_ref[...] = carry
    counts_ref[...] = carry


def _post_mixer(layer, x_pair, ctx_outs, lat_outs, mod, mod_row, w_out, norm_w, router_w, router_b):
    n, d = x_pair[0].shape[0] + x_pair[1].shape[0], x_pair[0].shape[1]
    tm = TOKEN_TILE
    ctx_tiles, x_specs = _path_specs(x_pair, tm)
    row_spec = lambda width: pl.BlockSpec((tm, width), lambda i: (i, 0))
    ctx_spec = lambda a: pl.BlockSpec((tm, a.shape[-1]), lambda i: (jnp.minimum(i, ctx_tiles - 1), 0))
    lat_spec = lambda a: pl.BlockSpec((tm, a.shape[-1]), lambda i: (jnp.maximum(i - ctx_tiles, 0), 0))
    lane_tile = pl.BlockSpec((tm, LANES), lambda i: (i, 0))
    flat_tile = pl.BlockSpec((None, 1, TOP_K * tm), lambda i: (i, 0, 0))
    flat_shape = jax.ShapeDtypeStruct((n // tm, 1, TOP_K * tm), jnp.int32)
    return pl.pallas_call(
        functools.partial(_post_kernel, ctx_tiles),
        out_shape=[jax.ShapeDtypeStruct((n, d), F32), jax.ShapeDtypeStruct((n * SUBLANES, LANES), F32),
                   flat_shape, jax.ShapeDtypeStruct((n, LANES), F32),
                   flat_shape, jax.ShapeDtypeStruct((1, LANES), F32)],
        grid=(n // tm,),
        in_specs=x_specs + [ctx_spec(a) for a in ctx_outs] + [lat_spec(a) for a in lat_outs]
        + [_mod_spec(mod, layer, mod_row)]
        + [_layer_spec(a, layer) for a in (w_out, norm_w, router_w, router_b)],
        out_specs=[row_spec(d), pl.BlockSpec((tm * SUBLANES, LANES), lambda i: (i, 0)),
                   flat_tile, lane_tile, flat_tile,
                   pl.BlockSpec((1, LANES), lambda i: (0, 0))],
        scratch_shapes=[pltpu.VMEM((1, LANES), F32), pltpu.VMEM(w_out.shape[1:], BF16)],
        compiler_params=_params(("arbitrary",), VMEM_LIMIT),
        name="post_mixer",
    )(*x_pair, *ctx_outs, *lat_outs, mod, w_out, norm_w, router_w, router_b)


def _pos_kernel(offs_ref, topi_ref, rank_ref, pos_ref):
    topi = topi_ref[...]
    pos = rank_ref[...]
    for e in range(N_EXPERTS):
        pos = pos + jnp.where(topi == e, offs_ref[e], 0)
    pos_ref[...] = pos


def _positions(topi_flat, rank_flat, offs):
    tiles, _, per_tile = topi_flat.shape
    full = pl.BlockSpec((tiles, per_tile), lambda i, offs: (0, 0))
    pos = pl.pallas_call(
        _pos_kernel,
        out_shape=jax.ShapeDtypeStruct((tiles, per_tile), jnp.int32),
        grid_spec=pltpu.PrefetchScalarGridSpec(num_scalar_prefetch=1, grid=(1,), in_specs=[full, full],
                                               out_specs=full),
        compiler_params=_params(("arbitrary",)),
        name="moe_positions",
    )(offs, topi_flat.reshape(tiles, per_tile), rank_flat.reshape(tiles, per_tile))
    return pos.reshape(-1)


def _dispatch_kernel(tend_ref, h2_ref, pos_ref, xs_ref, idx_ref, zero_ref, idx_sem, row_sem):
    tm = h2_ref.shape[0] // SUBLANES
    tile_rows = EXPERT_TILE * SUBLANES

    @pl.when(pl.program_id(0) == 0)
    def _():
        zero_ref[...] = jnp.zeros_like(zero_ref)

        def last_tile_copy(e):
            start = pl.multiple_of((tend_ref[e] - 1) * tile_rows, tile_rows)
            return pltpu.make_async_copy(zero_ref, xs_ref.at[pl.ds(start, tile_rows)], row_sem)

        def has_tiles(e):
            return tend_ref[e] > (tend_ref[e - 1] if e else 0)

        for e in range(N_EXPERTS):
            pl.when(has_tiles(e))(lambda e=e: last_tile_copy(e).start())
        for e in range(N_EXPERTS):
            pl.when(has_tiles(e))(lambda e=e: last_tile_copy(e).wait())

        def spare_tile_copy(t):
            return pltpu.make_async_copy(
                zero_ref, xs_ref.at[pl.ds(pl.multiple_of(t * tile_rows, tile_rows), tile_rows)], row_sem)

        n_used, n_tiles = tend_ref[N_EXPERTS - 1], xs_ref.shape[0] // tile_rows
        lax.fori_loop(n_used, n_tiles, lambda t, c: (spare_tile_copy(t).start(), c)[1], 0)
        lax.fori_loop(n_used, n_tiles, lambda t, c: (spare_tile_copy(t).wait(), c)[1], 0)

    _fetch_positions(pos_ref, idx_ref, idx_sem, pl.program_id(0))

    def issue(r, carry):
        for k in range(TOP_K):
            _slab_copy(h2_ref, r, xs_ref, idx_ref[k * tm + r], row_sem).start(priority=k % 2)
        return carry

    lax.fori_loop(0, tm, issue, 0, unroll=ISSUE_UNROLL)
    _wait_slabs(xs_ref, tm * TOP_K, row_sem)


def _dispatch(h2_slabs, pos_flat, tile_end, n_tiles):
    tm = TOKEN_TILE
    n = h2_slabs.shape[0] // SUBLANES
    hbm = pl.BlockSpec(memory_space=pl.ANY)
    grid_spec = pltpu.PrefetchScalarGridSpec(
        num_scalar_prefetch=1,
        grid=(n // tm,),
        in_specs=[pl.BlockSpec((tm * SUBLANES, LANES), lambda i, tend: (i, 0)), hbm],
        out_specs=hbm,
        scratch_shapes=[pltpu.SMEM((tm * TOP_K,), jnp.int32),
                        pltpu.VMEM((EXPERT_TILE * SUBLANES, LANES), F32),
                        pltpu.SemaphoreType.DMA, pltpu.SemaphoreType.DMA],
    )
    return pl.pallas_call(
        _dispatch_kernel,
        out_shape=jax.ShapeDtypeStruct((n_tiles * EXPERT_TILE * SUBLANES, LANES), F32),
        grid_spec=grid_spec,
        compiler_params=_params(("arbitrary",)),
        name="moe_dispatch",
    )(tile_end, h2_slabs, pos_flat)


def _expert_kernel(layer, te_ref, nxt_ref, nu_ref, xs_ref, wgu_hbm, wd_hbm, *refs):
    bias_refs = refs[:2 * TILES_PER_STEP]
    ys_ref, wgu_buf, wd_buf, wgu_bf, wd_bf, w_sem = refs[2 * TILES_PER_STEP:]
    dff = wd_hbm.shape[-2]
    tile_rows = EXPERT_TILE * SUBLANES

    def weight_copies(e):
        return (pltpu.make_async_copy(wgu_hbm.at[layer, e], wgu_buf, w_sem.at[0]),
                pltpu.make_async_copy(wd_hbm.at[layer, e], wd_buf, w_sem.at[1]))

    for half in range(TILES_PER_STEP):
        t = pl.program_id(0) * TILES_PER_STEP + half
        xs_tile = xs_ref.at[pl.ds(half * tile_rows, tile_rows)]
        ys_tile = ys_ref.at[pl.ds(half * tile_rows, tile_rows)]
        bgu_ref, bd_ref = bias_refs[2 * half:2 * half + 2]

        @pl.when(t < nu_ref[0])
        def _(t=t, xs_tile=xs_tile, ys_tile=ys_tile, bgu_ref=bgu_ref, bd_ref=bd_ref):
            prev = te_ref[jnp.maximum(t - 1, 0)]

            @pl.when((t == 0) | (te_ref[t] != prev))
            def _():
                @pl.when(t == 0)
                def _():
                    for copy in weight_copies(te_ref[0]):
                        copy.start()

                for copy in weight_copies(te_ref[t]):
                    copy.wait()
                wgu_bf[...] = wgu_buf[...].astype(BF16)
                wd_bf[...] = wd_buf[...].astype(BF16)

                @pl.when(nxt_ref[t] >= 0)
                def _():
                    for copy in weight_copies(nxt_ref[t]):
                        copy.start()

            bias_row = pl.ds(te_ref[t] % SUBLANES, 1)
            x = _load_slabs(xs_tile, EXPERT_TILE)
            gu = _dot(x.astype(BF16), wgu_bf[...]) + bgu_ref[bias_row, :]
            x_glu = jnp.minimum(gu[:, :dff], SWIGLU_LIMIT)
            x_lin = jnp.clip(gu[:, dff:], -SWIGLU_LIMIT, SWIGLU_LIMIT)
            act = x_glu * jax.nn.sigmoid(SWIGLU_ALPHA * x_glu) * (x_lin + 1.0)
            _store_slabs(ys_tile, _dot(act.astype(BF16), wd_bf[...]) + bd_ref[bias_row, :])

        @pl.when(t >= nu_ref[0])
        def _(ys_tile=ys_tile):
            ys_tile[...] = jnp.zeros(ys_tile.shape, F32)


def _experts(layer, tile_expert, next_expert, n_used, xs, w_gu, b_gu, w_down, b_down):
    tm = EXPERT_TILE
    d, dff2 = w_gu.shape[-2:]
    dff = w_down.shape[-2]
    step_rows = TILES_PER_STEP * tm * SUBLANES
    n_tiles = tile_expert.shape[0]
    assert n_tiles % TILES_PER_STEP == 0 and xs.shape[0] == n_tiles * tm * SUBLANES
    hbm = pl.BlockSpec(memory_space=pl.ANY)

    def bias_specs(half):
        group = lambda i, te: te[i * TILES_PER_STEP + half] // SUBLANES
        return [pl.BlockSpec((None, SUBLANES, dff2), lambda i, te, nx, nu: (layer, group(i, te), 0)),
                pl.BlockSpec((None, SUBLANES, d), lambda i, te, nx, nu: (layer, group(i, te), 0))]

    last_step = lambda nu: (nu[0] - 1) // TILES_PER_STEP
    grid_spec = pltpu.PrefetchScalarGridSpec(
        num_scalar_prefetch=3,
        grid=(n_tiles // TILES_PER_STEP,),
        in_specs=[pl.BlockSpec((step_rows, LANES), lambda i, te, nx, nu: (jnp.minimum(i, last_step(nu)), 0)),
                  hbm, hbm] + [s for half in range(TILES_PER_STEP) for s in bias_specs(half)],
        out_specs=pl.BlockSpec((step_rows, LANES), lambda i, te, nx, nu: (i, 0)),
        scratch_shapes=[pltpu.VMEM((d, dff2), F32), pltpu.VMEM((dff, d), F32),
                        pltpu.VMEM((d, dff2), BF16), pltpu.VMEM((dff, d), BF16),
                        pltpu.SemaphoreType.DMA((2,))],
    )
    biases = (b_gu, b_down) * TILES_PER_STEP
    return pl.pallas_call(
        functools.partial(_expert_kernel, layer),
        out_shape=jax.ShapeDtypeStruct(xs.shape, F32),
        grid_spec=grid_spec,
        compiler_params=_params(("arbitrary",), VMEM_LIMIT),
        name="moe_experts",
    )(tile_expert, next_expert, n_used, xs, w_gu, w_down, *biases)


def _combine_kernel(final, tile0, x1_ref, gates_ref, mod_ref, fw_ref, pos_ref, ys_ref, out_ref,
                    idx_ref, rows_ref, idx_sem, row_sem):
    tm = x1_ref.shape[0]
    _fetch_positions(pos_ref, idx_ref, idx_sem, pl.program_id(0) + tile0)

    def issue(r, carry):
        for k in range(TOP_K):
            _slab_copy(ys_ref, idx_ref[k * tm + r], rows_ref.at[k], r, row_sem).start(priority=k % 2)
        return carry

    lax.fori_loop(0, tm, issue, 0, unroll=ISSUE_UNROLL)
    _wait_slabs(ys_ref, tm * TOP_K, row_sem)

    gates = gates_ref[...]
    moe = gates[:, 0:1] * _load_slabs(rows_ref, tm, (0,))
    for k in range(1, TOP_K):
        moe = moe + gates[:, k:k + 1] * _load_slabs(rows_ref, tm, (k,))
    x2 = x1_ref[...] + mod_ref[5:6, :] * moe
    if final:
        x2 = x2 * lax.rsqrt(jnp.mean(x2 * x2, axis=-1, keepdims=True) + EPS) * fw_ref[...]
    out_ref[...] = x2


def _combine(layer, final, row0, n_rows, x1, gates, mod, mod_row, final_w, pos_flat, ys):
    d = x1.shape[1]
    tm = TOKEN_TILE
    tile0 = row0 // tm
    hbm = pl.BlockSpec(memory_space=pl.ANY)
    return pl.pallas_call(
        functools.partial(_combine_kernel, final, tile0),
        out_shape=jax.ShapeDtypeStruct((n_rows, d), F32),
        grid=(n_rows // tm,),
        in_specs=[pl.BlockSpec((tm, d), lambda i: (tile0 + i, 0)),
                  pl.BlockSpec((tm, LANES), lambda i: (tile0 + i, 0)),
                  _mod_spec(mod, layer, mod_row, tile0),
                  pl.BlockSpec((1, d), lambda i: (0, 0)),
                  hbm, hbm],
        out_specs=pl.BlockSpec((tm, d), lambda i: (i, 0)),
        scratch_shapes=[pltpu.SMEM((tm * TOP_K,), jnp.int32),
                        pltpu.VMEM((TOP_K, tm * SUBLANES, LANES), F32),
                        pltpu.SemaphoreType.DMA, pltpu.SemaphoreType.DMA],
        compiler_params=_params(("arbitrary",), VMEM_LIMIT),
        name="moe_combine",
    )(x1, gates, mod, final_w.reshape(1, d), pos_flat, ys)


def _routing_tables(counts, n_tiles):
    tm = EXPERT_TILE
    c = counts[0, :N_EXPERTS].astype(jnp.int32)
    tiles = (c + tm - 1) // tm
    tile_end = jnp.cumsum(tiles)
    offs = (tile_end - tiles) * tm
    n_used = tile_end[-1]
    t = jnp.minimum(jnp.arange(n_tiles, dtype=jnp.int32), n_used - 1)[:, None]
    e = jnp.arange(N_EXPERTS, dtype=jnp.int32)[None, :]
    tile_start, has_rows = (tile_end - tiles)[None, :], (tiles > 0)[None, :]
    te = jnp.minimum(jnp.sum((tile_end[None, :] <= t).astype(jnp.int32), axis=1), N_EXPERTS - 1)
    nxt = jnp.min(jnp.where(has_rows & (tile_start > t), e, N_EXPERTS), axis=1)
    nxt = jnp.where(nxt == N_EXPERTS, -1, nxt)
    return offs, tile_end.astype(jnp.int32), te, nxt.astype(jnp.int32), n_used.reshape(1).astype(jnp.int32)


def kernel(x_prompt, x_sample, cache_k, cache_v, state_lru, c, c_ctx, w_mod, b_mod, norm_mix_w, w_in, conf_conv_w, conf_conv_b, conf_ln_w, conf_ln_b, q_norm_w, k_norm_w, lru_conv_w, lru_conv_b, lru_wa, lru_ba, lru_wx, lru_bx, lru_lambda, w_out, norm_ffn_w, router_w, router_b, w_gu, b_gu, w_down, b_down, final_norm_w):
    batch, seq, d = x_prompt.shape
    dec_batch, dec_seq, _ = x_sample.shape
    depth = w_mod.shape[0]
    n_ctx, n_lat = batch * seq, dec_batch * dec_seq
    n = n_ctx + n_lat
    conf_w = conf_conv_w.shape[-1]
    lru_w = lru_conv_w.shape[-1]
    kv_w = N_KV_HEADS * HEAD_DIM
    attn_w = N_HEADS * HEAD_DIM
    widths = (2 * conf_w, attn_w, kv_w, kv_w, lru_w, lru_w)
    past = cache_k.shape[2]

    ctx_tiles = n_ctx // TOKEN_TILE
    lat_tiles_per_seq = dec_seq // TOKEN_TILE
    mod_row = lambda i: jnp.where(i < ctx_tiles, 0, 1 + (i - ctx_tiles) // lat_tiles_per_seq)

    n_cond = 8
    cvec = jnp.zeros((n_cond, d), F32).at[0].set(c_ctx).at[1:1 + dec_batch].set(c)
    mod = _modulation(cvec, w_mod, b_mod).reshape(depth, n_cond, 6, d)

    x = (x_prompt.reshape(n_ctx, d), x_sample.reshape(n_lat, d))
    rope = _rope_tables(dec_seq)
    cache_k4 = cache_k.reshape(dec_batch, depth, past, kv_w)
    cache_v4 = cache_v.reshape(dec_batch, depth, past, kv_w)
    h0_ctx = jnp.zeros((batch, 1, 2, lru_w), F32)
    assert d == SUBLANES * LANES, "row tables are moved as one (8, 128) tile per token"
    assert N_EXPERTS % SUBLANES == 0, "expert biases are fetched in blocks of 8 experts"
    n_sorted_tiles = n * TOP_K // EXPERT_TILE + N_EXPERTS

    row = lambda a: a.reshape(depth, 1, -1)
    p = {
        'conf_conv_w': conf_conv_w, 'conf_conv_b': row(conf_conv_b),
        'conf_ln_w': row(conf_ln_w), 'conf_ln_b': row(conf_ln_b),
        'lru_conv_w': lru_conv_w, 'lru_conv_b': row(lru_conv_b), 'lru_wa': lru_wa, 'lru_wx': lru_wx,
        'lru_bg': jnp.concatenate([row(lru_ba), row(lru_bx)], axis=-1), 'lru_lam': row(lru_lambda),
    }
    norm_mix, norm_ffn = row(norm_mix_w), row(norm_ffn_w)
    qw, kw = row(q_norm_w), row(k_norm_w)
    qw8, kw2 = jnp.tile(qw, (1, 1, N_HEADS)), jnp.tile(kw, (1, 1, N_KV_HEADS))
    router_w128 = jnp.pad(router_w, ((0, 0), (0, 0), (0, LANES - N_EXPERTS)))
    router_b128 = row(jnp.pad(router_b, ((0, 0), (0, LANES - N_EXPERTS))))

    new_k, new_v, new_h = [], [], []
    for l in range(depth):
        conf_u, q, k, v, lru_x, lru_g = _pre_mixer(l, x, mod, mod_row, norm_mix, w_in, widths)

        conf_c, lru_c, h_last = _seq_mixers(l, seq, batch, 0, conf_u, lru_x, lru_g, h0_ctx, 0, p)
        conf_l, lru_l, _ = _seq_mixers(l, dec_seq, dec_batch, n_ctx, conf_u, lru_x, lru_g, state_lru, l, p)

        attn_c, k_ctx = _attention_ctx(l, seq, batch, q, k, v, qw, kw)
        attn_l = _attention_lat(dec_seq, dec_batch, n_ctx, l, q, k, v, cache_k4, cache_v4, qw8, kw2, rope)
        new_k.append(k_ctx.reshape(batch, seq, N_KV_HEADS, HEAD_DIM))
        new_v.append(v[:n_ctx].reshape(batch, seq, N_KV_HEADS, HEAD_DIM))
        new_h.append(h_last)

        x1, h2, topi, gates, rank, counts = _post_mixer(
            l, x, (conf_c, attn_c, lru_c), (conf_l, attn_l, lru_l), mod, mod_row, w_out, norm_ffn,
            router_w128, router_b128)
        offs, tile_end, tile_expert, next_expert, n_used = _routing_tables(counts, n_sorted_tiles)
        pos = _positions(topi, rank, offs)
        xs = _dispatch(h2, pos, tile_end, n_sorted_tiles)
        ys = _experts(l, tile_expert, next_expert, n_used, xs, w_gu, b_gu, w_down, b_down)
        x = tuple(_combine(l, l == depth - 1, row0, rows, x1, gates, mod, mod_row, final_norm_w, pos, ys)
                  for row0, rows in ((0, n_ctx), (n_ctx, n_lat)))

    y_prompt = x[0].reshape(batch, seq, d)
    y_sample = x[1].reshape(dec_batch, dec_seq, d)
    return (y_prompt, y_sample, jnp.stack(new_k, axis=1), jnp.stack(new_v, axis=1), jnp.stack(new_h, axis=1))
```

```python
import functools

import jax
import jax.numpy as jnp
import numpy as np
from jax import lax
from jax.experimental import pallas as pl
from jax.experimental.pallas import tpu as pltpu

F32 = jnp.float32
BF16 = jnp.bfloat16

HEAD_DIM = 64
N_HEADS = 8
N_KV_HEADS = 2
HEADS_PER_KV = N_HEADS // N_KV_HEADS
CONF_KERNEL = 31
LRU_CONV = 4
LRU_C = 8.0
N_EXPERTS = 32
TOP_K = 4
SWIGLU_ALPHA = 1.702
SWIGLU_LIMIT = 7.0
ROPE_THETA = 10000.0
GRID_W = 64
EPS = 1e-6

LANES = 128
SUBLANES = 8
ISSUE_UNROLL = 4
TOKEN_TILE = 256
EXPERT_TILE = 256
TILES_PER_STEP = 4
CONV_HALO = 16
ROW_CHUNK = 64
SCAN_CHUNK = 32
VMEM_LIMIT = 56 * 1024 * 1024


def _params(sem, vmem=None):
    return pltpu.CompilerParams(dimension_semantics=sem, vmem_limit_bytes=vmem)


def _split_bf16(x):
    hi = x.astype(BF16)
    lo = (x - hi.astype(F32)).astype(BF16)
    return hi, lo


def _dot(a, b):
    return jnp.dot(a, b, preferred_element_type=F32)


def _dot3(a, b):
    a_hi, a_lo = _split_bf16(a)
    b_hi, b_lo = _split_bf16(b)
    return _dot(a_hi, b_hi) + (_dot(a_hi, b_lo) + _dot(a_lo, b_hi))


def _mod_kernel(c_ref, w_ref, b_ref, o_ref):
    c = c_ref[...]
    s = c * jax.nn.sigmoid(c)
    o_ref[...] = _dot3(s, w_ref[...]) + b_ref[...]


def _modulation(cvec, w_mod, b_mod):
    depth, d, d6 = w_mod.shape
    tn = 768
    return pl.pallas_call(
        _mod_kernel,
        out_shape=jax.ShapeDtypeStruct((depth, cvec.shape[0], d6), F32),
        grid=(depth, d6 // tn),
        in_specs=[
            pl.BlockSpec(cvec.shape, lambda l, j: (0, 0)),
            pl.BlockSpec((None, d, tn), lambda l, j: (l, 0, j)),
            pl.BlockSpec((None, 1, tn), lambda l, j: (l, 0, j)),
        ],
        out_specs=pl.BlockSpec((None, cvec.shape[0], tn), lambda l, j: (l, 0, j)),
        compiler_params=_params(("arbitrary", "arbitrary")),
        name="modulation",
    )(cvec, w_mod, b_mod.reshape(depth, 1, d6))


def _pre_kernel(ctx_tiles, xc_ref, xl_ref, mod_ref, nw_ref, win_f32_ref, conf_ref, q_ref, k_ref, v_ref, lx_ref,
                lg_ref, win_ref):
    @pl.when(pl.program_id(0) == 0)
    def _():
        win_ref[...] = win_f32_ref[...].astype(BF16)

    x = jnp.where(pl.program_id(0) < ctx_tiles, xc_ref[...], xl_ref[...])
    m = mod_ref[...]
    shift, scale = m[0:1], m[1:2]
    h = x * lax.rsqrt(jnp.mean(x * x, axis=-1, keepdims=True) + EPS) * nw_ref[...]
    h = h * (1.0 + scale) + shift
    proj = _dot(h.astype(BF16), win_ref[...])
    col = 0
    for ref in (conf_ref, q_ref, k_ref, v_ref, lx_ref, lg_ref):
        w = ref.shape[-1]
        ref[...] = proj[:, col:col + w]
        col += w


def _path_specs(x_pair, tm):
    ctx_tiles = x_pair[0].shape[0] // tm
    return ctx_tiles, [
        pl.BlockSpec((tm, x_pair[0].shape[-1]), lambda i: (jnp.minimum(i, ctx_tiles - 1), 0)),
        pl.BlockSpec((tm, x_pair[1].shape[-1]), lambda i: (jnp.maximum(i - ctx_tiles, 0), 0))]


def _layer_spec(a, layer):
    zeros = (0,) * (a.ndim - 1)
    return pl.BlockSpec((None,) + a.shape[1:], lambda *_: (layer,) + zeros)


def _mod_spec(mod, layer, mod_row, tile0=0):
    return pl.BlockSpec((None, None) + mod.shape[2:], lambda i, *_: (layer, mod_row(tile0 + i), 0, 0))


def _pre_mixer(layer, x_pair, mod, mod_row, norm_w, w_in, widths):
    n, d = x_pair[0].shape[0] + x_pair[1].shape[0], x_pair[0].shape[1]
    ctx_tiles, x_specs = _path_specs(x_pair, TOKEN_TILE)
    return pl.pallas_call(
        functools.partial(_pre_kernel, ctx_tiles),
        out_shape=[jax.ShapeDtypeStruct((n, w), F32) for w in widths],
        grid=(n // TOKEN_TILE,),
        in_specs=x_specs + [_mod_spec(mod, layer, mod_row), _layer_spec(norm_w, layer),
                            _layer_spec(w_in, layer)],
        out_specs=[pl.BlockSpec((TOKEN_TILE, w), lambda i: (i, 0)) for w in widths],
        scratch_shapes=[pltpu.VMEM(w_in.shape[1:], BF16)],
        compiler_params=_params(("arbitrary",), VMEM_LIMIT),
        name="pre_mixer",
    )(*x_pair, mod, norm_w, w_in)


def _gelu_tanh(x):
    return 0.5 * x * (1.0 + jnp.tanh(0.7978845608028654 * (x + 0.044715 * (x * x * x))))


def _seq_kernel(seq_len, conf_ref, lx_ref, lg_ref, h0_ref, ccw_ref, ccb_ref, lnw_ref, lnb_ref,
                lcw_ref, lcb_ref, wa_ref, wx_ref, bg_ref, lam_ref,
                co_ref, lo_ref, hl_ref, pad_ref, rot_ref, af_ref, uf_ref, ab_ref, ub_ref, wg_ref):
    L = seq_len
    W = co_ref.shape[-1]
    pad = L // 2
    zeros_halo = jnp.zeros((CONV_HALO, W), F32)

    @pl.when(pl.program_id(0) == 0)
    def _():
        dirs, heads, blk, _ = wa_ref.shape
        zero_blk = jnp.zeros((blk, blk), F32)
        for h in range(heads):
            cols = [w_ref[d, h] if g == h else zero_blk
                    for w_ref in (wa_ref, wx_ref) for d in range(dirs) for g in range(heads)]
            wg_ref[h * blk:(h + 1) * blk, :] = jnp.concatenate(cols, axis=-1).astype(BF16)

    pad_ref[0:CONV_HALO, :] = zeros_halo
    pad_ref[CONV_HALO + L:2 * CONV_HALO + L, :] = zeros_halo
    for c in range(L // ROW_CHUNK):
        r0 = c * ROW_CHUNK
        u = conf_ref[r0:r0 + ROW_CHUNK, :]
        pad_ref[CONV_HALO + r0:CONV_HALO + r0 + ROW_CHUNK, :] = u[:, :W] * jax.nn.sigmoid(u[:, W:])
    shifted_rows = L + 2 * CONV_HALO - SUBLANES
    for s in range(1, SUBLANES):
        for r0 in range(0, shifted_rows, ROW_CHUNK):
            rows = min(ROW_CHUNK, shifted_rows - r0)
            rot_ref[s - 1, r0:r0 + rows, :] = pad_ref[r0 + s:r0 + s + rows, :]
    left = CONF_KERNEL // 2
    for c in range(L // ROW_CHUNK):
        r0 = c * ROW_CHUNK
        acc = jnp.zeros((ROW_CHUNK, W), F32) + ccb_ref[...]
        for k in range(CONF_KERNEL):
            whole, s = divmod(CONV_HALO - left + k, SUBLANES)
            start = whole * SUBLANES + r0
            src = pad_ref if s == 0 else rot_ref.at[s - 1]
            acc = acc + ccw_ref[k:k + 1, :] * src[start:start + ROW_CHUNK, :]
        mu = jnp.mean(acc, axis=-1, keepdims=True)
        cen = acc - mu
        var = jnp.mean(cen * cen, axis=-1, keepdims=True)
        y = cen * lax.rsqrt(var + EPS) * lnw_ref[...] + lnb_ref[...]
        co_ref[r0:r0 + ROW_CHUNK, :] = y * jax.nn.sigmoid(y)

    for c in range(L // ROW_CHUNK):
        r0 = c * ROW_CHUNK
        pad_ref[CONV_HALO + r0:CONV_HALO + r0 + ROW_CHUNK, :] = lx_ref[r0:r0 + ROW_CHUNK, :]
    ones_pad = jnp.ones((pad, W), F32)
    zeros_pad = jnp.zeros((pad, W), F32)
    af_ref[0:pad, :] = ones_pad
    uf_ref[0:pad, :] = zeros_pad
    ab_ref[L:L + pad, :] = ones_pad
    ub_ref[L:L + pad, :] = zeros_pad
    sp = jax.nn.softplus(-lam_ref[...])
    h0 = h0_ref[...]
    left = LRU_CONV // 2
    n_chunks = L // ROW_CHUNK
    for c in range(n_chunks):
        r0 = c * ROW_CHUNK
        xc = jnp.zeros((ROW_CHUNK, W), F32) + lcb_ref[...]
        for k in range(LRU_CONV):
            start = CONV_HALO - left + k + r0
            xc = xc + lcw_ref[k:k + 1, :] * pad_ref[start:start + ROW_CHUNK, :]
        g = _dot(xc.astype(BF16), wg_ref[...]) + bg_ref[...]
        r = jax.nn.sigmoid(g[:, :2 * W])
        i = jax.nn.sigmoid(g[:, 2 * W:])
        a = jnp.exp((-LRU_C) * r * sp)
        xc2 = jnp.concatenate([xc, xc], axis=-1)
        u = jnp.sqrt(1.0 - a * a) * i * xc2
        a_f, a_b, u_f, u_b = a[:, :W], a[:, W:], u[:, :W], u[:, W:]
        row = lax.broadcasted_iota(jnp.int32, (ROW_CHUNK, W), 0)
        if c == 0:
            u_f = jnp.where(row == 0, u_f + a_f * h0[0:1], u_f)
        if c == n_chunks - 1:
            u_b = jnp.where(row == ROW_CHUNK - 1, u_b + a_b * h0[1:2], u_b)
        af_ref[pad + r0:pad + r0 + ROW_CHUNK, :] = a_f
        uf_ref[pad + r0:pad + r0 + ROW_CHUNK, :] = u_f
        ab_ref[r0:r0 + ROW_CHUNK, :] = a_b
        ub_ref[r0:r0 + ROW_CHUNK, :] = u_b

    n_sc = L // SCAN_CHUNK
    s = 1
    while s < L:
        for c in reversed(range(n_sc)):
            r0 = c * SCAN_CHUNK
            if r0 + SCAN_CHUNK <= s:
                continue
            cur = slice(pad + r0, pad + r0 + SCAN_CHUNK)
            sh = slice(pad + r0 - s, pad + r0 - s + SCAN_CHUNK)
            a_cur = af_ref[cur, :]
            uf_ref[cur, :] = uf_ref[cur, :] + a_cur * uf_ref[sh, :]
            af_ref[cur, :] = a_cur * af_ref[sh, :]
        for c in range(n_sc):
            r0 = c * SCAN_CHUNK
            if r0 >= L - s:
                continue
            cur = slice(r0, r0 + SCAN_CHUNK)
            sh = slice(r0 + s, r0 + s + SCAN_CHUNK)
            a_cur = ab_ref[cur, :]
            ub_ref[cur, :] = ub_ref[cur, :] + a_cur * ub_ref[sh, :]
            ab_ref[cur, :] = a_cur * ab_ref[sh, :]
        s *= 2

    for c in range(n_chunks):
        r0 = c * ROW_CHUNK
        h = uf_ref[pad + r0:pad + r0 + ROW_CHUNK, :] + ub_ref[r0:r0 + ROW_CHUNK, :]
        lo_ref[r0:r0 + ROW_CHUNK, :] = h * _gelu_tanh(lg_ref[r0:r0 + ROW_CHUNK, :])
    hl_ref[0:1, :] = uf_ref[pad + L - 1:pad + L, :]
    hl_ref[1:2, :] = ub_ref[0:1, :]


def _seq_mixers(layer, seq_len, n_seq, row0, conf_u, lru_x, lru_g, h0, h0_layer, p):
    w = lru_x.shape[-1]
    b0 = row0 // seq_len
    pad = seq_len // 2
    in_spec = lambda width: pl.BlockSpec((seq_len, width), lambda i: (b0 + i, 0))
    out_spec = pl.BlockSpec((seq_len, w), lambda i: (i, 0))
    h0_spec = pl.BlockSpec((None, None, 2, w), lambda i: (i, h0_layer, 0, 0))
    weights = (p['conf_conv_w'], p['conf_conv_b'], p['conf_ln_w'], p['conf_ln_b'],
               p['lru_conv_w'], p['lru_conv_b'], p['lru_wa'], p['lru_wx'], p['lru_bg'], p['lru_lam'])
    padded = seq_len + 2 * CONV_HALO
    return pl.pallas_call(
        functools.partial(_seq_kernel, seq_len),
        out_shape=[jax.ShapeDtypeStruct((n_seq * seq_len, w), F32)] * 2
        + [jax.ShapeDtypeStruct((n_seq, 2, w), F32)],
        grid=(n_seq,),
        in_specs=[in_spec(2 * w), in_spec(w), in_spec(w), h0_spec] + [_layer_spec(a, layer) for a in weights],
        out_specs=[out_spec, out_spec, pl.BlockSpec((None, 2, w), lambda i: (i, 0, 0))],
        scratch_shapes=[pltpu.VMEM((padded, w), F32), pltpu.VMEM((SUBLANES - 1, padded, w), F32)]
        + [pltpu.VMEM((seq_len + pad, w), F32)] * 4 + [pltpu.VMEM((w, 4 * w), BF16)],
        compiler_params=_params(("arbitrary",), VMEM_LIMIT),
        name=f"seq_mixers_{seq_len}",
    )(conf_u, lru_x, lru_g, h0, *weights)


def _head_rms(x):
    return lax.rsqrt(jnp.mean(x * x, axis=-1, keepdims=True) + EPS)


def _swap_halves(x):
    width = x.shape[-1]
    lane = lax.broadcasted_iota(jnp.int32, x.shape, x.ndim - 1)
    up = pltpu.roll(x, width - HEAD_DIM // 4, x.ndim - 1)
    down = pltpu.roll(x, HEAD_DIM // 4, x.ndim - 1)
    return jnp.where((lane % (HEAD_DIM // 2)) < HEAD_DIM // 4, up, down)


def _attend(q_heads, k_bf16, v_bf16):
    scale = HEAD_DIM ** -0.5
    s = lax.dot_general(q_heads.astype(BF16), k_bf16, (((1,), (1,)), ((), ())),
                        preferred_element_type=F32) * scale
    p = jnp.exp(s - jnp.max(s, axis=-1, keepdims=True))
    denom = jnp.sum(p, axis=-1, keepdims=True)
    return _dot(p.astype(BF16), v_bf16) / denom


def _attn_ctx_kernel(q_ref, k_ref, v_ref, qw_ref, kw_ref, o_ref, kn_ref):
    L = q_ref.shape[0]
    q, k, v = q_ref[...], k_ref[...], v_ref[...]
    qw, kw = qw_ref[...], kw_ref[...]
    k_out, o_out = [], []
    for g in range(N_KV_HEADS):
        kh = k[:, g * HEAD_DIM:(g + 1) * HEAD_DIM]
        kh = kh * _head_rms(kh) * kw
        k_out.append(kh)
        qs = []
        for j in range(HEADS_PER_KV):
            h = g * HEADS_PER_KV + j
            qh = q[:, h * HEAD_DIM:(h + 1) * HEAD_DIM]
            qs.append(qh * _head_rms(qh) * qw)
        o = _attend(jnp.concatenate(qs, axis=0), kh.astype(BF16),
                    v[:, g * HEAD_DIM:(g + 1) * HEAD_DIM].astype(BF16))
        o_out += [o[j * L:(j + 1) * L] for j in range(HEADS_PER_KV)]
    kn_ref[...] = jnp.concatenate(k_out, axis=-1)
    o_ref[...] = jnp.concatenate(o_out, axis=-1)


def _attention_ctx(layer, seq_len, n_seq, q, k, v, qw, kw):
    kvw = k.shape[-1]
    row_spec = lambda width: pl.BlockSpec((seq_len, width), lambda i: (i, 0))
    return pl.pallas_call(
        _attn_ctx_kernel,
        out_shape=[jax.ShapeDtypeStruct((n_seq * seq_len, q.shape[-1]), F32),
                   jax.ShapeDtypeStruct((n_seq * seq_len, kvw), F32)],
        grid=(n_seq,),
        in_specs=[row_spec(q.shape[-1]), row_spec(kvw), row_spec(kvw), _layer_spec(qw, layer),
                  _layer_spec(kw, layer)],
        out_specs=[row_spec(q.shape[-1]), row_spec(kvw)],
        compiler_params=_params(("arbitrary",), VMEM_LIMIT),
        name="attention_ctx",
    )(q, k, v, qw, kw)


def _attn_lat_kernel(q_ref, k_ref, v_ref, ck_ref, cv_ref, qw_ref, kw_ref, cq_ref, sq_ref, ck_t_ref,
                     sk_t_ref, o_ref, kall_ref, vall_ref):
    L = k_ref.shape[0]
    tq = q_ref.shape[0]

    @pl.when(pl.program_id(1) == 0)
    def _():
        k = k_ref[...]
        t = k * kw_ref[...]
        rot = t * ck_t_ref[...] + _swap_halves(t) * sk_t_ref[...]
        parts = []
        for g in range(N_KV_HEADS):
            sl = slice(g * HEAD_DIM, (g + 1) * HEAD_DIM)
            parts.append(rot[:, sl] * _head_rms(k[:, sl]))
        kall_ref[0:L, :] = jnp.concatenate(parts, axis=-1).astype(BF16)
        kall_ref[L:, :] = ck_ref[...].astype(BF16)
        vall_ref[0:L, :] = v_ref[...].astype(BF16)
        vall_ref[L:, :] = cv_ref[...].astype(BF16)

    q = q_ref[...]
    t = q * qw_ref[...]
    rot = t * cq_ref[...] + _swap_halves(t) * sq_ref[...]
    o_out = []
    for g in range(N_KV_HEADS):
        qs = []
        for j in range(HEADS_PER_KV):
            sl = slice((g * HEADS_PER_KV + j) * HEAD_DIM, (g * HEADS_PER_KV + j + 1) * HEAD_DIM)
            qs.append(rot[:, sl] * _head_rms(q[:, sl]))
        sl = slice(g * HEAD_DIM, (g + 1) * HEAD_DIM)
        o = _attend(jnp.concatenate(qs, axis=0), kall_ref[:, sl], vall_ref[:, sl])
        o_out += [o[j * tq:(j + 1) * tq] for j in range(HEADS_PER_KV)]
    o_ref[...] = jnp.concatenate(o_out, axis=-1)


def _attention_lat(seq_len, n_seq, row0, layer, q, k, v, cache_k, cache_v, qw8, kw2, rope):
    tq = TOKEN_TILE
    nq = seq_len // tq
    qwid, kvw = q.shape[-1], k.shape[-1]
    past = cache_k.shape[2]
    b0q = row0 // tq
    b0s = row0 // seq_len
    cq, sq, ck, sk = rope
    full = lambda a: pl.BlockSpec(a.shape, lambda b, j: (0,) * a.ndim)
    seq_spec = pl.BlockSpec((seq_len, kvw), lambda b, j: (b0s + b, 0))
    cache_spec = pl.BlockSpec((None, None, past, kvw), lambda b, j: (b, layer, 0, 0))
    q_spec = pl.BlockSpec((tq, qwid), lambda b, j: (b0q + b * nq + j, 0))
    rope_q_spec = pl.BlockSpec((tq, qwid), lambda b, j: (j, 0))
    return pl.pallas_call(
        _attn_lat_kernel,
        out_shape=jax.ShapeDtypeStruct((n_seq * seq_len, qwid), F32),
        grid=(n_seq, nq),
        in_specs=[q_spec, seq_spec, seq_spec, cache_spec, cache_spec, _layer_spec(qw8, layer),
                  _layer_spec(kw2, layer), rope_q_spec, rope_q_spec, full(ck), full(sk)],
        out_specs=pl.BlockSpec((tq, qwid), lambda b, j: (b * nq + j, 0)),
        scratch_shapes=[pltpu.VMEM((seq_len + past, kvw), BF16)] * 2,
        compiler_params=_params(("arbitrary", "arbitrary"), VMEM_LIMIT),
        name="attention_lat",
    )(q, k, v, cache_k, cache_v, qw8, kw2, cq, sq, ck, sk)


def _rope_tables(seq_len):
    t = np.arange(seq_len)
    row = (t // GRID_W).astype(np.float32)
    col = (t % GRID_W).astype(np.float32)
    half = HEAD_DIM // 2
    freqs = (np.float32(ROPE_THETA) ** (-np.arange(0, half, 2, dtype=np.float32) / np.float32(half)))
    ang_r, ang_c = row[:, None] * freqs, col[:, None] * freqs
    cos = np.concatenate([np.cos(ang_r)] * 2 + [np.cos(ang_c)] * 2, axis=-1)
    sin = np.concatenate([-np.sin(ang_r), np.sin(ang_r), -np.sin(ang_c), np.sin(ang_c)], axis=-1)
    return tuple(jnp.asarray(a, F32) for a in
                 (np.tile(cos, (1, N_HEADS)), np.tile(sin, (1, N_HEADS)),
                  np.tile(cos, (1, N_KV_HEADS)), np.tile(sin, (1, N_KV_HEADS))))


def _slab_copy(src_ref, src_tok, dst_ref, dst_tok, sem):
    src = src_ref.at[pl.ds(pl.multiple_of(src_tok * SUBLANES, SUBLANES), SUBLANES)]
    dst = dst_ref.at[pl.ds(pl.multiple_of(dst_tok * SUBLANES, SUBLANES), SUBLANES)]
    return pltpu.make_async_copy(src, dst, sem)


def _load_slabs(ref, n_tok, lead=()):
    return jnp.concatenate(
        [ref[(*lead, pl.ds(j, n_tok, stride=SUBLANES), slice(None))] for j in range(SUBLANES)], axis=-1)


def _store_slabs(ref, val):
    for j in range(SUBLANES):
        ref[pl.ds(j, val.shape[0], stride=SUBLANES), :] = val[:, j * LANES:(j + 1) * LANES]


def _wait_slabs(hbm_ref, n_tok, sem):
    span = hbm_ref.at[pl.ds(0, n_tok * SUBLANES)]
    pltpu.make_async_copy(span, span, sem).wait()


def _fetch_positions(pos_ref, idx_ref, sem, tile):
    n = idx_ref.shape[0]
    copy = pltpu.make_async_copy(pos_ref.at[pl.ds(tile * n, n)], idx_ref, sem)
    copy.start()
    copy.wait()


def _choice_major(x):
    t = jnp.transpose(x)
    return jnp.concatenate([t[k:k + 1, :] for k in range(TOP_K)], axis=1)


def _post_kernel(ctx_tiles, xc_ref, xl_ref, co_c_ref, ao_c_ref, lo_c_ref, co_l_ref, ao_l_ref, lo_l_ref, mod_ref,
                 wout_f32_ref, nw_ref, rw_ref, rb_ref,
                 x1_ref, h2_ref, topi_ref, gates_ref, rank_ref, counts_ref, carry_ref, wout_ref):
    tm = xc_ref.shape[0]
    cw, aw = co_c_ref.shape[-1], ao_c_ref.shape[-1]
    is_ctx = pl.program_id(0) < ctx_tiles

    @pl.when(pl.program_id(0) == 0)
    def _():
        carry_ref[...] = jnp.zeros_like(carry_ref)
        wout_ref[...] = wout_f32_ref[...].astype(BF16)

    m = mod_ref[...]
    gate1, shift2, scale2 = m[2:3], m[3:4], m[4:5]
    pick = lambda c_ref, l_ref: jnp.where(is_ctx, c_ref[...], l_ref[...]).astype(BF16)
    x = jnp.where(is_ctx, xc_ref[...], xl_ref[...])
    mixed = (_dot(pick(co_c_ref, co_l_ref), wout_ref[0:cw, :])
             + _dot(pick(ao_c_ref, ao_l_ref), wout_ref[cw:cw + aw, :])
             + _dot(pick(lo_c_ref, lo_l_ref), wout_ref[cw + aw:, :]))
    x1 = x + gate1 * mixed
    x1_ref[...] = x1
    h2 = x1 * lax.rsqrt(jnp.mean(x1 * x1, axis=-1, keepdims=True) + EPS) * nw_ref[...]
    h2 = h2 * (1.0 + scale2) + shift2
    _store_slabs(h2_ref, h2)

    lane = lax.broadcasted_iota(jnp.int32, (tm, LANES), 1)
    lane_f = lane.astype(F32)
    logits = jnp.where(lane < N_EXPERTS, _dot3(h2, rw_ref[...]) + rb_ref[...], -jnp.inf)
    top_v, onehots = [], []
    topi = jnp.zeros((tm, LANES), F32)
    for k in range(TOP_K):
        mx = jnp.max(logits, axis=-1, keepdims=True)
        idx = jnp.min(jnp.where(logits == mx, lane_f, float(LANES)), axis=-1, keepdims=True)
        hit = lane_f == idx
        logits = jnp.where(hit, -jnp.inf, logits)
        top_v.append(mx)
        onehots.append(hit)
        topi = jnp.where(lane == k, idx, topi)
    topi_ref[...] = _choice_major(topi).astype(jnp.int32)
    exps = [jnp.exp(v - top_v[0]) for v in top_v]
    denom = exps[0] + exps[1] + exps[2] + exps[3]
    gates = jnp.zeros((tm, LANES), F32)
    for k in range(TOP_K):
        gates = jnp.where(lane == k, exps[k] / denom, gates)
    gates_ref[...] = gates

    chosen = jnp.zeros((tm, LANES), F32)
    for hit in onehots:
        chosen = jnp.where(hit, 1.0, chosen)
    r_i = lax.broadcasted_iota(jnp.int32, (tm, tm), 0)
    c_i = lax.broadcasted_iota(jnp.int32, (tm, tm), 1)
    lower = jnp.where(c_i < r_i, 1.0, 0.0).astype(BF16)
    before = _dot(lower, chosen.astype(BF16)) + carry_ref[...]
    rank = jnp.zeros((tm, LANES), F32)
    for k, hit in enumerate(onehots):
        rk = jnp.sum(jnp.where(hit, before, 0.0), axis=-1, keepdims=True)
        rank = jnp.where(lane == k, rk, rank)
    rank_ref[...] = _choice_major(rank).astype(jnp.int32)
    carry = carry_ref[...] + jnp.sum(chosen, axis=0, keepdims=True)
    carry_ref[...] = carry
    counts_ref[...] = carry


def _post_mixer(layer, x_pair, ctx_outs, lat_outs, mod, mod_row, w_out, norm_w, router_w, router_b):
    n, d = x_pair[0].shape[0] + x_pair[1].shape[0], x_pair[0].shape[1]
    tm = TOKEN_TILE
    ctx_tiles, x_specs = _path_specs(x_pair, tm)
    row_spec = lambda width: pl.BlockSpec((tm, width), lambda i: (i, 0))
    ctx_spec = lambda a: pl.BlockSpec((tm, a.shape[-1]), lambda i: (jnp.minimum(i, ctx_tiles - 1), 0))
    lat_spec = lambda a: pl.BlockSpec((tm, a.shape[-1]), lambda i: (jnp.maximum(i - ctx_tiles, 0), 0))
    lane_tile = pl.BlockSpec((tm, LANES), lambda i: (i, 0))
    flat_tile = pl.BlockSpec((None, 1, TOP_K * tm), lambda i: (i, 0, 0))
    flat_shape = jax.ShapeDtypeStruct((n // tm, 1, TOP_K * tm), jnp.int32)
    return pl.pallas_call(
        functools.partial(_post_kernel, ctx_tiles),
        out_shape=[jax.ShapeDtypeStruct((n, d), F32), jax.ShapeDtypeStruct((n * SUBLANES, LANES), F32),
                   flat_shape, jax.ShapeDtypeStruct((n, LANES), F32),
                   flat_shape, jax.ShapeDtypeStruct((1, LANES), F32)],
        grid=(n // tm,),
        in_specs=x_specs + [ctx_spec(a) for a in ctx_outs] + [lat_spec(a) for a in lat_outs]
        + [_mod_spec(mod, layer, mod_row)]
        + [_layer_spec(a, layer) for a in (w_out, norm_w, router_w, router_b)],
        out_specs=[row_spec(d), pl.BlockSpec((tm * SUBLANES, LANES), lambda i: (i, 0)),
                   flat_tile, lane_tile, flat_tile,
                   pl.BlockSpec((1, LANES), lambda i: (0, 0))],
        scratch_shapes=[pltpu.VMEM((1, LANES), F32), pltpu.VMEM(w_out.shape[1:], BF16)],
        compiler_params=_params(("arbitrary",), VMEM_LIMIT),
        name="post_mixer",
    )(*x_pair, *ctx_outs, *lat_outs, mod, w_out, norm_w, router_w, router_b)


def _pos_kernel(offs_ref, topi_ref, rank_ref, pos_ref):
    topi = topi_ref[...]
    pos = rank_ref[...]
    for e in range(N_EXPERTS):
        pos = pos + jnp.where(topi == e, offs_ref[e], 0)
    pos_ref[...] = pos


def _positions(topi_flat, rank_flat, offs):
    tiles, _, per_tile = topi_flat.shape
    full = pl.BlockSpec((tiles, per_tile), lambda i, offs: (0, 0))
    pos = pl.pallas_call(
        _pos_kernel,
        out_shape=jax.ShapeDtypeStruct((tiles, per_tile), jnp.int32),
        grid_spec=pltpu.PrefetchScalarGridSpec(num_scalar_prefetch=1, grid=(1,), in_specs=[full, full],
                                               out_specs=full),
        compiler_params=_params(("arbitrary",)),
        name="moe_positions",
    )(offs, topi_flat.reshape(tiles, per_tile), rank_flat.reshape(tiles, per_tile))
    return pos.reshape(-1)


def _dispatch_kernel(tend_ref, h2_ref, pos_ref, xs_ref, idx_ref, zero_ref, idx_sem, row_sem):
    tm = h2_ref.shape[0] // SUBLANES
    tile_rows = EXPERT_TILE * SUBLANES

    @pl.when(pl.program_id(0) == 0)
    def _():
        zero_ref[...] = jnp.zeros_like(zero_ref)

        def last_tile_copy(e):
            start = pl.multiple_of((tend_ref[e] - 1) * tile_rows, tile_rows)
            return pltpu.make_async_copy(zero_ref, xs_ref.at[pl.ds(start, tile_rows)], row_sem)

        def has_tiles(e):
            return tend_ref[e] > (tend_ref[e - 1] if e else 0)

        for e in range(N_EXPERTS):
            pl.when(has_tiles(e))(lambda e=e: last_tile_copy(e).start())
        for e in range(N_EXPERTS):
            pl.when(has_tiles(e))(lambda e=e: last_tile_copy(e).wait())

        def spare_tile_copy(t):
            return pltpu.make_async_copy(
                zero_ref, xs_ref.at[pl.ds(pl.multiple_of(t * tile_rows, tile_rows), tile_rows)], row_sem)

        n_used, n_tiles = tend_ref[N_EXPERTS - 1], xs_ref.shape[0] // tile_rows
        lax.fori_loop(n_used, n_tiles, lambda t, c: (spare_tile_copy(t).start(), c)[1], 0)
        lax.fori_loop(n_used, n_tiles, lambda t, c: (spare_tile_copy(t).wait(), c)[1], 0)

    _fetch_positions(pos_ref, idx_ref, idx_sem, pl.program_id(0))

    def issue(r, carry):
        for k in range(TOP_K):
            _slab_copy(h2_ref, r, xs_ref, idx_ref[k * tm + r], row_sem).start(priority=k % 2)
        return carry

    lax.fori_loop(0, tm, issue, 0, unroll=ISSUE_UNROLL)
    _wait_slabs(xs_ref, tm * TOP_K, row_sem)


def _dispatch(h2_slabs, pos_flat, tile_end, n_tiles):
    tm = TOKEN_TILE
    n = h2_slabs.shape[0] // SUBLANES
    hbm = pl.BlockSpec(memory_space=pl.ANY)
    grid_spec = pltpu.PrefetchScalarGridSpec(
        num_scalar_prefetch=1,
        grid=(n // tm,),
        in_specs=[pl.BlockSpec((tm * SUBLANES, LANES), lambda i, tend: (i, 0)), hbm],
        out_specs=hbm,
        scratch_shapes=[pltpu.SMEM((tm * TOP_K,), jnp.int32),
                        pltpu.VMEM((EXPERT_TILE * SUBLANES, LANES), F32),
                        pltpu.SemaphoreType.DMA, pltpu.SemaphoreType.DMA],
    )
    return pl.pallas_call(
        _dispatch_kernel,
        out_shape=jax.ShapeDtypeStruct((n_tiles * EXPERT_TILE * SUBLANES, LANES), F32),
        grid_spec=grid_spec,
        compiler_params=_params(("arbitrary",)),
        name="moe_dispatch",
    )(tile_end, h2_slabs, pos_flat)


def _expert_kernel(layer, te_ref, nxt_ref, par_ref, nu_ref, xs_ref, wgu_hbm, wd_hbm, *refs):
    bias_refs = refs[:2 * TILES_PER_STEP]
    ys_ref, wgu_buf, wd_buf, wgu_bf, wd_bf, w_sem = refs[2 * TILES_PER_STEP:]
    dff = wd_hbm.shape[-2]
    tile_rows = EXPERT_TILE * SUBLANES

    def weight_copies(e, slot):
        return (pltpu.make_async_copy(wgu_hbm.at[layer, e], wgu_buf.at[slot], w_sem.at[slot]),
                pltpu.make_async_copy(wd_hbm.at[layer, e], wd_buf.at[slot], w_sem.at[2 + slot]))

    for half in range(TILES_PER_STEP):
        t = pl.program_id(0) * TILES_PER_STEP + half
        xs_tile = xs_ref.at[pl.ds(half * tile_rows, tile_rows)]
        ys_tile = ys_ref.at[pl.ds(half * tile_rows, tile_rows)]
        bgu_ref, bd_ref = bias_refs[2 * half:2 * half + 2]

        @pl.when(t < nu_ref[0])
        def _(t=t, xs_tile=xs_tile, ys_tile=ys_tile, bgu_ref=bgu_ref, bd_ref=bd_ref):
            prev = te_ref[jnp.maximum(t - 1, 0)]

            @pl.when((t == 0) | (te_ref[t] != prev))
            def _():
                slot = par_ref[t]

                @pl.when(t == 0)
                def _():
                    for copy in weight_copies(te_ref[0], slot):
                        copy.start()

                for copy in weight_copies(te_ref[t], slot):
                    copy.wait()

                @pl.when(nxt_ref[t] >= 0)
                def _():
                    for copy in weight_copies(nxt_ref[t], 1 - slot):
                        copy.start()

                wgu_bf[...] = wgu_buf[slot].astype(BF16)
                wd_bf[...] = wd_buf[slot].astype(BF16)

            bias_row = pl.ds(te_ref[t] % SUBLANES, 1)
            x = _load_slabs(xs_tile, EXPERT_TILE)
            gu = _dot(x.astype(BF16), wgu_bf[...]) + bgu_ref[bias_row, :]
            x_glu = jnp.minimum(gu[:, :dff], SWIGLU_LIMIT)
            x_lin = jnp.clip(gu[:, dff:], -SWIGLU_LIMIT, SWIGLU_LIMIT)
            act = x_glu * jax.nn.sigmoid(SWIGLU_ALPHA * x_glu) * (x_lin + 1.0)
            _store_slabs(ys_tile, _dot(act.astype(BF16), wd_bf[...]) + bd_ref[bias_row, :])

        @pl.when(t >= nu_ref[0])
        def _(ys_tile=ys_tile):
            ys_tile[...] = jnp.zeros(ys_tile.shape, F32)


def _experts(layer, tile_expert, next_expert, parity, n_used, xs, w_gu, b_gu, w_down, b_down):
    tm = EXPERT_TILE
    d, dff2 = w_gu.shape[-2:]
    dff = w_down.shape[-2]
    step_rows = TILES_PER_STEP * tm * SUBLANES
    n_tiles = tile_expert.shape[0]
    assert n_tiles % TILES_PER_STEP == 0 and xs.shape[0] == n_tiles * tm * SUBLANES
    hbm = pl.BlockSpec(memory_space=pl.ANY)

    def bias_specs(half):
        group = lambda i, te: te[i * TILES_PER_STEP + half] // SUBLANES
        return [pl.BlockSpec((None, SUBLANES, dff2), lambda i, te, nx, pa, nu: (layer, group(i, te), 0)),
                pl.BlockSpec((None, SUBLANES, d), lambda i, te, nx, pa, nu: (layer, group(i, te), 0))]

    last_step = lambda nu: (nu[0] - 1) // TILES_PER_STEP
    grid_spec = pltpu.PrefetchScalarGridSpec(
        num_scalar_prefetch=4,
        grid=(n_tiles // TILES_PER_STEP,),
        in_specs=[pl.BlockSpec((step_rows, LANES), lambda i, te, nx, pa, nu: (jnp.minimum(i, last_step(nu)), 0)),
                  hbm, hbm] + [s for half in range(TILES_PER_STEP) for s in bias_specs(half)],
        out_specs=pl.BlockSpec((step_rows, LANES), lambda i, te, nx, pa, nu: (i, 0)),
        scratch_shapes=[pltpu.VMEM((2, d, dff2), F32), pltpu.VMEM((2, dff, d), F32),
                        pltpu.VMEM((d, dff2), BF16), pltpu.VMEM((dff, d), BF16),
                        pltpu.SemaphoreType.DMA((4,))],
    )
    biases = (b_gu, b_down) * TILES_PER_STEP
    return pl.pallas_call(
        functools.partial(_expert_kernel, layer),
        out_shape=jax.ShapeDtypeStruct(xs.shape, F32),
        grid_spec=grid_spec,
        compiler_params=_params(("arbitrary",), VMEM_LIMIT),
        name="moe_experts",
    )(tile_expert, next_expert, parity, n_used, xs, w_gu, w_down, *biases)


def _combine_kernel(final, tile0, x1_ref, gates_ref, mod_ref, fw_ref, pos_ref, ys_ref, out_ref,
                    idx_ref, rows_ref, idx_sem, row_sem):
    tm = x1_ref.shape[0]
    _fetch_positions(pos_ref, idx_ref, idx_sem, pl.program_id(0) + tile0)

    def issue(r, carry):
        for k in range(TOP_K):
            _slab_copy(ys_ref, idx_ref[k * tm + r], rows_ref.at[k], r, row_sem).start(priority=k % 2)
        return carry

    lax.fori_loop(0, tm, issue, 0, unroll=ISSUE_UNROLL)
    _wait_slabs(ys_ref, tm * TOP_K, row_sem)

    gates = gates_ref[...]
    moe = gates[:, 0:1] * _load_slabs(rows_ref, tm, (0,))
    for k in range(1, TOP_K):
        moe = moe + gates[:, k:k + 1] * _load_slabs(rows_ref, tm, (k,))
    x2 = x1_ref[...] + mod_ref[5:6, :] * moe
    if final:
        x2 = x2 * lax.rsqrt(jnp.mean(x2 * x2, axis=-1, keepdims=True) + EPS) * fw_ref[...]
    out_ref[...] = x2


def _combine(layer, final, row0, n_rows, x1, gates, mod, mod_row, final_w, pos_flat, ys):
    d = x1.shape[1]
    tm = TOKEN_TILE
    tile0 = row0 // tm
    hbm = pl.BlockSpec(memory_space=pl.ANY)
    return pl.pallas_call(
        functools.partial(_combine_kernel, final, tile0),
        out_shape=jax.ShapeDtypeStruct((n_rows, d), F32),
        grid=(n_rows // tm,),
        in_specs=[pl.BlockSpec((tm, d), lambda i: (tile0 + i, 0)),
                  pl.BlockSpec((tm, LANES), lambda i: (tile0 + i, 0)),
                  _mod_spec(mod, layer, mod_row, tile0),
                  pl.BlockSpec((1, d), lambda i: (0, 0)),
                  hbm, hbm],
        out_specs=pl.BlockSpec((tm, d), lambda i: (i, 0)),
        scratch_shapes=[pltpu.SMEM((tm * TOP_K,), jnp.int32),
                        pltpu.VMEM((TOP_K, tm * SUBLANES, LANES), F32),
                        pltpu.SemaphoreType.DMA, pltpu.SemaphoreType.DMA],
        compiler_params=_params(("arbitrary",), VMEM_LIMIT),
        name="moe_combine",
    )(x1, gates, mod, final_w.reshape(1, d), pos_flat, ys)


def _routing_tables(counts, n_tiles):
    tm = EXPERT_TILE
    c = counts[0, :N_EXPERTS].astype(jnp.int32)
    tiles = (c + tm - 1) // tm
    tile_end = jnp.cumsum(tiles)
    offs = (tile_end - tiles) * tm
    n_used = tile_end[-1]
    t = jnp.minimum(jnp.arange(n_tiles, dtype=jnp.int32), n_used - 1)[:, None]
    e = jnp.arange(N_EXPERTS, dtype=jnp.int32)[None, :]
    tile_start, has_rows = (tile_end - tiles)[None, :], (tiles > 0)[None, :]
    te = jnp.minimum(jnp.sum((tile_end[None, :] <= t).astype(jnp.int32), axis=1), N_EXPERTS - 1)
    nxt = jnp.min(jnp.where(has_rows & (tile_start > t), e, N_EXPERTS), axis=1)
    nxt = jnp.where(nxt == N_EXPERTS, -1, nxt)
    parity = (jnp.sum((has_rows & (tile_start <= t)).astype(jnp.int32), axis=1) - 1) % 2
    return (offs, tile_end.astype(jnp.int32), te, nxt.astype(jnp.int32), parity.astype(jnp.int32),
            n_used.reshape(1).astype(jnp.int32))


def kernel(x_prompt, x_sample, cache_k, cache_v, state_lru, c, c_ctx, w_mod, b_mod, norm_mix_w, w_in, conf_conv_w, conf_conv_b, conf_ln_w, conf_ln_b, q_norm_w, k_norm_w, lru_conv_w, lru_conv_b, lru_wa, lru_ba, lru_wx, lru_bx, lru_lambda, w_out, norm_ffn_w, router_w, router_b, w_gu, b_gu, w_down, b_down, final_norm_w):
    batch, seq, d = x_prompt.shape
    dec_batch, dec_seq, _ = x_sample.shape
    depth = w_mod.shape[0]
    n_ctx, n_lat = batch * seq, dec_batch * dec_seq
    n = n_ctx + n_lat
    conf_w = conf_conv_w.shape[-1]
    lru_w = lru_conv_w.shape[-1]
    kv_w = N_KV_HEADS * HEAD_DIM
    attn_w = N_HEADS * HEAD_DIM
    widths = (2 * conf_w, attn_w, kv_w, kv_w, lru_w, lru_w)
    past = cache_k.shape[2]

    ctx_tiles = n_ctx // TOKEN_TILE
    lat_tiles_per_seq = dec_seq // TOKEN_TILE
    mod_row = lambda i: jnp.where(i < ctx_tiles, 0, 1 + (i - ctx_tiles) // lat_tiles_per_seq)

    n_cond = 8
    cvec = jnp.zeros((n_cond, d), F32).at[0].set(c_ctx).at[1:1 + dec_batch].set(c)
    mod = _modulation(cvec, w_mod, b_mod).reshape(depth, n_cond, 6, d)

    x = (x_prompt.reshape(n_ctx, d), x_sample.reshape(n_lat, d))
    rope = _rope_tables(dec_seq)
    cache_k4 = cache_k.reshape(dec_batch, depth, past, kv_w)
    cache_v4 = cache_v.reshape(dec_batch, depth, past, kv_w)
    h0_ctx = jnp.zeros((batch, 1, 2, lru_w), F32)
    assert d == SUBLANES * LANES, "row tables are moved as one (8, 128) tile per token"
    assert N_EXPERTS % SUBLANES == 0, "expert biases are fetched in blocks of 8 experts"
    n_sorted_tiles = n * TOP_K // EXPERT_TILE + N_EXPERTS

    row = lambda a: a.reshape(depth, 1, -1)
    p = {
        'conf_conv_w': conf_conv_w, 'conf_conv_b': row(conf_conv_b),
        'conf_ln_w': row(conf_ln_w), 'conf_ln_b': row(conf_ln_b),
        'lru_conv_w': lru_conv_w, 'lru_conv_b': row(lru_conv_b), 'lru_wa': lru_wa, 'lru_wx': lru_wx,
        'lru_bg': jnp.concatenate([row(lru_ba), row(lru_bx)], axis=-1), 'lru_lam': row(lru_lambda),
    }
    norm_mix, norm_ffn = row(norm_mix_w), row(norm_ffn_w)
    qw, kw = row(q_norm_w), row(k_norm_w)
    qw8, kw2 = jnp.tile(qw, (1, 1, N_HEADS)), jnp.tile(kw, (1, 1, N_KV_HEADS))
    router_w128 = jnp.pad(router_w, ((0, 0), (0, 0), (0, LANES - N_EXPERTS)))
    router_b128 = row(jnp.pad(router_b, ((0, 0), (0, LANES - N_EXPERTS))))

    new_k, new_v, new_h = [], [], []
    for l in range(depth):
        conf_u, q, k, v, lru_x, lru_g = _pre_mixer(l, x, mod, mod_row, norm_mix, w_in, widths)

        conf_c, lru_c, h_last = _seq_mixers(l, seq, batch, 0, conf_u, lru_x, lru_g, h0_ctx, 0, p)
        conf_l, lru_l, _ = _seq_mixers(l, dec_seq, dec_batch, n_ctx, conf_u, lru_x, lru_g, state_lru, l, p)

        attn_c, k_ctx = _attention_ctx(l, seq, batch, q, k, v, qw, kw)
        attn_l = _attention_lat(dec_seq, dec_batch, n_ctx, l, q, k, v, cache_k4, cache_v4, qw8, kw2, rope)
        new_k.append(k_ctx.reshape(batch, seq, N_KV_HEADS, HEAD_DIM))
        new_v.append(v[:n_ctx].reshape(batch, seq, N_KV_HEADS, HEAD_DIM))
        new_h.append(h_last)

        x1, h2, topi, gates, rank, counts = _post_mixer(
            l, x, (conf_c, attn_c, lru_c), (conf_l, attn_l, lru_l), mod, mod_row, w_out, norm_ffn,
            router_w128, router_b128)
        offs, tile_end, tile_expert, next_expert, parity, n_used = _routing_tables(counts, n_sorted_tiles)
        pos = _positions(topi, rank, offs)
        xs = _dispatch(h2, pos, tile_end, n_sorted_tiles)
        ys = _experts(l, tile_expert, next_expert, parity, n_used, xs, w_gu, b_gu, w_down, b_down)
        x = tuple(_combine(l, l == depth - 1, row0, rows, x1, gates, mod, mod_row, final_norm_w, pos, ys)
                  for row0, rows in ((0, n_ctx), (n_ctx, n_lat)))

    y_prompt = x[0].reshape(batch, seq, d)
    y_sample = x[1].reshape(dec_batch, dec_seq, d)
    return (y_prompt, y_sample, jnp.stack(new_k, axis=1), jnp.stack(new_v, axis=1), jnp.stack(new_h, axis=1))
```
